```python
import math
import jax
import jax.numpy as jnp
from jax import lax
import numpy as np

D_MODEL = 1024
BATCH = 32
SEQ = 2048
DEPTH = 2

F32 = jnp.float32
CTX_LEN = 256
GRID_W = 64
ROPE_THETA = 10000.0
NORM_EPS = 1e-6
HEAD_DIM = 64
Q_BLOCK = 128

MLA_HEADS = 8
MLA_NOPE = 64
MLA_ROPE = 32
MLA_V = 64
MLA_Q_RANK = 384
MLA_KV_RANK = 256
RWKV_HEADS = 8
RWKV_N = 64
RWKV_W = RWKV_HEADS * RWKV_N
DECAY_LORA = 64
ICLR_LORA = 64
GATE_LORA = 128
GN_EPS = 64e-5
RWKV_COLS = 3 * RWKV_W + DECAY_LORA + ICLR_LORA + GATE_LORA
SSM_HEADS = 16
SSM_P = 64
SSM_G = 2
SSM_N = 128
SSM_INNER = SSM_HEADS * SSM_P
SSM_XBC = SSM_INNER + 2 * SSM_G * SSM_N
SSM_CONV = 5
SSD_CHUNK = 128
SWA_HEADS = 8
SWA_KV_HEADS = 2
WINDOW = 128
D_FF = 2816
FFN_CONV = 3

AB_SPLITS = (MLA_Q_RANK, MLA_KV_RANK, MLA_ROPE, RWKV_COLS)
AB_IN = sum(AB_SPLITS)
AB_MIX = MLA_HEADS * MLA_V + RWKV_W
CD_SPLITS = (SSM_INNER, SSM_XBC, 2 * SSM_HEADS,
             SWA_HEADS * HEAD_DIM, SWA_KV_HEADS * HEAD_DIM, SWA_KV_HEADS * HEAD_DIM)
CD_IN = sum(CD_SPLITS)
CD_MIX = SSM_INNER + SWA_HEADS * HEAD_DIM
N_EVEN = (DEPTH + 1) // 2
N_ODD = DEPTH // 2

kernel_name = 'hybrid_mla_rwkv7_ssd_swa_prefix_dit'


def split_cols(t, sizes):
    return jnp.split(t, [int(s) for s in np.cumsum(sizes)[:-1]], axis=-1)


def rms_norm(x, g):
    x32 = x.astype(F32)
    y = x32 * lax.rsqrt(jnp.mean(x32 * x32, axis=-1, keepdims=True) + NORM_EPS)
    return y.astype(x.dtype) * g


def modulate(x, g, shift, scale):
    return rms_norm(x, g) * (1 + scale) + shift


def dwconv_centred(x, w, b):
    k = w.shape[0]
    y = lax.conv_general_dilated(x, w[:, None, :].astype(x.dtype), window_strides=(1,),
                                 padding=((k // 2, k // 2),),
                                 dimension_numbers=('NWC', 'WIO', 'NWC'),
                                 feature_group_count=x.shape[-1])
    return y + b


def token_shift(x, mu_prev, mu_next):
    x_prev = jnp.pad(x, ((0, 0), (1, 0), (0, 0)))[:, :-1]
    x_next = jnp.pad(x, ((0, 0), (0, 1), (0, 0)))[:, 1:]
    return x + mu_prev * (x_prev - x) + mu_next * (x_next - x)


def axial_rope(n_tok, rot_dim):
    rows = n_tok // GRID_W
    row_pos, col_pos = jnp.meshgrid(jnp.arange(rows, dtype=F32), jnp.arange(GRID_W, dtype=F32), indexing='ij')
    n_freq = rot_dim // 4
    inv_freq = ROPE_THETA ** (-jnp.arange(n_freq, dtype=F32) / n_freq)
    ang = jnp.concatenate([row_pos.reshape(-1, 1) * inv_freq, col_pos.reshape(-1, 1) * inv_freq], axis=-1)
    return jnp.cos(ang), jnp.sin(ang)


def apply_rope(t, cos, sin):
    t1, t2 = jnp.split(t, 2, axis=-1)
    cos = cos[None, :, None, :].astype(t.dtype)
    sin = sin[None, :, None, :].astype(t.dtype)
    return jnp.concatenate([t1 * cos - t2 * sin, t1 * sin + t2 * cos], axis=-1)


def attend(q, k, v):
    s = jnp.einsum('bqhd,bkhd->bhqk', q, k).astype(F32) * (q.shape[-1] ** -0.5)
    p = jax.nn.softmax(s, axis=-1).astype(v.dtype)
    return jnp.einsum('bhqk,bkhd->bqhd', p, v)


def attend_in_query_blocks(q, k, v):
    b, l, h, d = q.shape
    q_blocks = jnp.moveaxis(q.reshape(b, l // Q_BLOCK, Q_BLOCK, h, d), 1, 0)
    out = lax.map(lambda qb: attend(qb, k, v), q_blocks)
    return jnp.moveaxis(out, 0, 1).reshape(b, l, h * v.shape[-1])


def mla_heads(cq, ckv, k_rot, p, rope):
    b, l, _ = cq.shape
    q = (rms_norm(cq, p['mla_q_norm_g']) @ p['mla_w_q_up']).reshape(b, l, MLA_HEADS, MLA_NOPE + MLA_ROPE)
    kv = (rms_norm(ckv, p['mla_kv_norm_g']) @ p['mla_w_kv_up']).reshape(b, l, MLA_HEADS, MLA_NOPE + MLA_V)
    q_nope, q_rot = jnp.split(q, [MLA_NOPE], axis=-1)
    k_nope, v = jnp.split(kv, [MLA_NOPE], axis=-1)
    k_rot = k_rot[:, :, None, :]
    if rope is not None:
        q_rot = apply_rope(q_rot, *rope)
        k_rot = apply_rope(k_rot, *rope)
    k = jnp.concatenate([k_nope, jnp.broadcast_to(k_rot, (b, l, MLA_HEADS, MLA_ROPE))], axis=-1)
    return jnp.concatenate([q_nope, q_rot], axis=-1), k, v


def rwkv_features(cols, p):
    b, l, _ = cols.shape
    heads = lambda t: t.reshape(b, l, RWKV_HEADS, RWKV_N)
    xs = token_shift(cols, p['rwkv_mu_prev'], p['rwkv_mu_next'])
    r, k, v, xw, xa, xg = split_cols(xs, (RWKV_W, RWKV_W, RWKV_W, DECAY_LORA, ICLR_LORA, GATE_LORA))
    g = jax.nn.sigmoid(xg) @ p['rwkv_g2']
    kk = heads(k * p['rwkv_k_k']).astype(F32)
    kk = kk * lax.rsqrt(jnp.sum(kk * kk, axis=-1, keepdims=True) + 1e-12)
    dirs = []
    for d in range(2):
        w_log = -jax.nn.softplus(-(p['rwkv_w0'][d] + jnp.tanh(xw) @ p['rwkv_w2'][d])) - 0.5
        a = jax.nn.sigmoid(p['rwkv_a0'][d] + xa @ p['rwkv_a2'][d])
        k_eff = k * (1 + (a - 1) * p['rwkv_k_a'])
        dirs.append((heads(jnp.exp(-jnp.exp(w_log.astype(F32)))), heads(k_eff), heads(a)))
    return heads(r), heads(k), heads(v), g, kk, dirs


def rwkv_scan(r, w, k, v, kk, a, state0, reverse):
    def step(s, inp):
        r_t, w_t, k_t, v_t, kk_t, a_t = inp
        sa = jnp.einsum('bhij,bhj->bhi', s, -kk_t)
        s = s * w_t[:, :, None, :] + sa[..., None] * (kk_t * a_t)[:, :, None, :] + v_t[..., None] * k_t[:, :, None, :]
        return s, jnp.einsum('bhij,bhj->bhi', s, r_t)
    xs = tuple(jnp.moveaxis(t.astype(F32), 1, 0) for t in (r, w, k, v, kk, a))
    s_final, ys = lax.scan(step, state0, xs, reverse=reverse)
    return jnp.moveaxis(ys, 0, 1), s_final


def rwkv_bidirectional(f_ctx, f_lat):
    r_c, _, v_c, _, kk_c, dirs_c = f_ctx
    r_l, _, v_l, _, kk_l, dirs_l = f_lat
    s0 = jnp.zeros((r_c.shape[0], RWKV_HEADS, RWKV_N, RWKV_N), F32)
    ys_c, ys_l = [], []
    for (w_c, k_c, a_c), (w_l, k_l, a_l), rev in zip(dirs_c, dirs_l, (False, True)):
        y_c, s_ctx = rwkv_scan(r_c, w_c, k_c, v_c, kk_c, a_c, s0, rev)
        y_l, _ = rwkv_scan(r_l, w_l, k_l, v_l, kk_l, a_l, s_ctx, rev)
        ys_c.append(y_c)
        ys_l.append(y_l)
    return ys_c[0] + ys_c[1], ys_l[0] + ys_l[1]


def rwkv_output(y, feats, p):
    r, k, v, g = feats[:4]
    b, l = r.shape[:2]
    mu = jnp.mean(y, axis=-1, keepdims=True)
    var = jnp.mean(jnp.square(y - mu), axis=-1, keepdims=True)
    yn = ((y - mu) * lax.rsqrt(var + GN_EPS)).reshape(b, l, RWKV_W).astype(g.dtype)
    yn = yn * p['rwkv_ln_g'] + p['rwkv_ln_b']
    bonus = (jnp.sum(r * k * p['rwkv_r_k'], axis=-1, keepdims=True) * v).reshape(b, l, RWKV_W)
    return (yn + bonus) * g


def mixer_mla_rwkv(h_ctx, h_lat, p, rope, need_ctx):
    cq_c, ckv_c, kr_c, rw_c = split_cols(h_ctx @ p['w_in'], AB_SPLITS)
    cq_l, ckv_l, kr_l, rw_l = split_cols(h_lat @ p['w_in'], AB_SPLITS)
    q_c, k_c, v_c = mla_heads(cq_c, ckv_c, kr_c, p, None)
    q_l, k_l, v_l = mla_heads(cq_l, ckv_l, kr_l, p, rope)
    att_l = attend_in_query_blocks(q_l, jnp.concatenate([k_c, k_l], axis=1),
                                   jnp.concatenate([v_c, v_l], axis=1))
    f_c, f_l = rwkv_features(rw_c, p), rwkv_features(rw_l, p)
    y_c, y_l = rwkv_bidirectional(f_c, f_l)
    out_l = jnp.concatenate([att_l, rwkv_output(y_l, f_l, p)], axis=-1) @ p['w_out']
    if not need_ctx:
        return None, out_l
    b, lc, _ = h_ctx.shape
    att_c = attend(q_c, k_c, v_c).reshape(b, lc, MLA_HEADS * MLA_V)
    out_c = jnp.concatenate([att_c, rwkv_output(y_c, f_c, p)], axis=-1) @ p['w_out']
    return out_c, out_l


def ssm_features(xbc, dt_raw, p):
    b, l, _ = xbc.shape
    xbc = jax.nn.silu(dwconv_centred(xbc, p['ssm_conv_w'], p['ssm_conv_b']))
    xs, bm, cm = split_cols(xbc, (SSM_INNER, SSM_G * SSM_N, SSM_G * SSM_N))
    dt = jax.nn.softplus(dt_raw.reshape(b, l, 2, SSM_HEADS) + p['ssm_dt_bias'])
    return (xs.reshape(b, l, SSM_HEADS, SSM_P), dt,
            bm.reshape(b, l, SSM_G, SSM_N), cm.reshape(b, l, SSM_G, SSM_N))


def ssd_scan(x, dt, a_log, bm, cm, h0):
    b, l = x.shape[:2]
    nc, hg = l // SSD_CHUNK, SSM_HEADS // SSM_G
    xc = x.astype(F32).reshape(b, nc, SSD_CHUNK, SSM_G, hg, SSM_P)
    dtc = dt.astype(F32).reshape(b, nc, SSD_CHUNK, SSM_G, hg)
    bc = bm.astype(F32).reshape(b, nc, SSD_CHUNK, SSM_G, SSM_N)
    cc = cm.astype(F32).reshape(b, nc, SSD_CHUNK, SSM_G, SSM_N)
    a = -jnp.exp(a_log.astype(F32)).reshape(SSM_G, hg)
    a_cum = jnp.cumsum(dtc * a, axis=2)
    tri = jnp.tril(jnp.ones((SSD_CHUNK, SSD_CHUNK), bool))
    seg = a_cum[:, :, :, None] - a_cum[:, :, None, :]
    decay_qk = jnp.exp(jnp.where(tri[:, :, None, None], seg, -jnp.inf))
    cb = jnp.einsum('bcqgn,bckgn->bcqkg', cc, bc)
    y_diag = jnp.einsum('bcqkgh,bckghp->bcqghp', cb[..., None] * decay_qk * dtc[:, :, None], xc)
    decay_end = jnp.exp(a_cum[:, :, -1:] - a_cum)
    chunk_states = jnp.einsum('bcqgn,bcqgh,bcqghp->bcghpn', bc, decay_end * dtc, xc)
    chunk_decay = jnp.exp(a_cum[:, :, -1])

    def carry(h, inp):
        dec, st = inp
        return h * dec[..., None, None] + st, h
    h_last, h_in = lax.scan(carry, h0.astype(F32),
                            (jnp.moveaxis(chunk_decay, 1, 0), jnp.moveaxis(chunk_states, 1, 0)))
    h_in = jnp.moveaxis(h_in, 0, 1)
    y_off = jnp.einsum('bcqgn,bcghpn,bcqgh->bcqghp', cc, h_in, jnp.exp(a_cum))
    y = (y_diag + y_off).reshape(b, l, SSM_HEADS, SSM_P)
    return y.astype(x.dtype), h_last


def ssd_bidirectional(f_ctx, f_lat, p):
    flip = lambda t: jnp.flip(t, axis=1)
    x_c, dt_c, b_c, c_c = f_ctx
    x_l, dt_l, b_l, c_l = f_lat
    h0 = jnp.zeros((x_c.shape[0], SSM_G, SSM_HEADS // SSM_G, SSM_P, SSM_N), F32)
    a_log = p['ssm_a_log']
    yf_c, hf = ssd_scan(x_c, dt_c[:, :, 0], a_log[0], b_c, c_c, h0)
    yf_l, _ = ssd_scan(x_l, dt_l[:, :, 0], a_log[0], b_l, c_l, hf)
    yb_c, hb = ssd_scan(flip(x_c), flip(dt_c[:, :, 1]), a_log[1], flip(b_c), flip(c_c), h0)
    yb_l, _ = ssd_scan(flip(x_l), flip(dt_l[:, :, 1]), a_log[1], flip(b_l), flip(c_l), hb)
    d_skip = p['ssm_d'][:, None]
    return yf_c + flip(yb_c) + d_skip * x_c, yf_l + flip(yb_l) + d_skip * x_l


def gated_rms_norm(y, z, g):
    b, l, _ = z.shape
    u = (y.reshape(b, l, SSM_INNER) * jax.nn.silu(z)).reshape(b, l, SSM_G, SSM_INNER // SSM_G)
    return rms_norm(u, g.reshape(SSM_G, -1)).reshape(b, l, SSM_INNER)


def swa_heads(q, k, v, rope):
    b, l, _ = q.shape
    q = q.reshape(b, l, SWA_HEADS, HEAD_DIM)
    k = k.reshape(b, l, SWA_KV_HEADS, HEAD_DIM)
    v = v.reshape(b, l, SWA_KV_HEADS, HEAD_DIM)
    if rope is not None:
        q, k = apply_rope(q, *rope), apply_rope(k, *rope)
    return q.reshape(b, l, SWA_KV_HEADS, SWA_HEADS // SWA_KV_HEADS, HEAD_DIM), k, v


def sink_softmax(scores, sink):
    s_sink = jnp.broadcast_to(sink[:, :, None, None].astype(F32), scores.shape[:-1] + (1,))
    return jax.nn.softmax(jnp.concatenate([scores, s_sink], axis=-1), axis=-1)[..., :-1]


def swa_context(q, k, v, sink):
    s = jnp.einsum('bqghd,bkgd->bghqk', q, k).astype(F32) * (HEAD_DIM ** -0.5)
    p = sink_softmax(s, sink).astype(v.dtype)
    return jnp.einsum('bghqk,bkgd->bqghd', p, v)


def swa_latent(q, k, v, k_ctx, v_ctx, sink):
    b, l = q.shape[:2]
    nb, span, n_ctx = l // Q_BLOCK, Q_BLOCK + 2 * WINDOW, k_ctx.shape[1]
    pad = ((0, 0), (WINDOW, WINDOW), (0, 0), (0, 0))
    kp, vp = jnp.pad(k, pad), jnp.pad(v, pad)
    rel = jnp.arange(span)[None, :] - WINDOW - jnp.arange(Q_BLOCK)[:, None]
    band = jnp.abs(rel) <= WINDOW
    q_blocks = jnp.moveaxis(q.reshape(b, nb, Q_BLOCK, *q.shape[2:]), 1, 0)
    scale = HEAD_DIM ** -0.5

    def block(args):
        i, qb = args
        start = i * Q_BLOCK
        kw = lax.dynamic_slice_in_dim(kp, start, span, axis=1)
        vw = lax.dynamic_slice_in_dim(vp, start, span, axis=1)
        key_pos = start - WINDOW + jnp.arange(span)
        mask = band & ((key_pos >= 0) & (key_pos < l))[None, :]
        s_ctx = jnp.einsum('bqghd,bkgd->bghqk', qb, k_ctx).astype(F32) * scale
        s_win = jnp.einsum('bqghd,bkgd->bghqk', qb, kw).astype(F32) * scale
        s_win = jnp.where(mask, s_win, -jnp.inf)
        p = sink_softmax(jnp.concatenate([s_ctx, s_win], axis=-1), sink).astype(v.dtype)
        return (jnp.einsum('bghqk,bkgd->bqghd', p[..., :n_ctx], v_ctx)
                + jnp.einsum('bghqk,bkgd->bqghd', p[..., n_ctx:], vw))

    out = lax.map(block, (jnp.arange(nb), q_blocks))
    return jnp.moveaxis(out, 0, 1).reshape(b, l, SWA_HEADS * HEAD_DIM)


def mixer_ssd_swa(h_ctx, h_lat, p, rope, need_ctx):
    z_c, xbc_c, dt_c, q_c, k_c, v_c = split_cols(h_ctx @ p['w_in'], CD_SPLITS)
    z_l, xbc_l, dt_l, q_l, k_l, v_l = split_cols(h_lat @ p['w_in'], CD_SPLITS)
    y_c, y_l = ssd_bidirectional(ssm_features(xbc_c, dt_c, p), ssm_features(xbc_l, dt_l, p), p)
    sink = p['swa_sink'].reshape(SWA_KV_HEADS, SWA_HEADS // SWA_KV_HEADS)
    q_c, k_c, v_c = swa_heads(q_c, k_c, v_c, None)
    q_l, k_l, v_l = swa_heads(q_l, k_l, v_l, rope)
    att_l = swa_latent(q_l, k_l, v_l, k_c, v_c, sink)
    out_l = jnp.concatenate([gated_rms_norm(y_l, z_l, p['ssm_norm_g']), att_l], axis=-1) @ p['w_out']
    if not need_ctx:
        return None, out_l
    b, lc, _ = h_ctx.shape
    att_c = swa_context(q_c, k_c, v_c, sink).reshape(b, lc, SWA_HEADS * HEAD_DIM)
    out_c = jnp.concatenate([gated_rms_norm(y_c, z_c, p['ssm_norm_g']), att_c], axis=-1) @ p['w_out']
    return out_c, out_l


def conv_ffn(h, p):
    u = dwconv_centred(h @ p['ffn_w_up'], p['ffn_conv_w'], p['ffn_conv_b'])
    gate, val = jnp.split(u, 2, axis=-1)
    return (jax.nn.silu(gate) * val) @ p['ffn_w_down']


def adaln_mods(cond, p):
    return jnp.split(jax.nn.silu(cond) @ p['ada_w'] + p['ada_b'], 6, axis=-1)


def trunk_layer(x_lat, x_ctx, c, c_ctx, p, mixer, rope, need_ctx):
    m_lat = [m[:, None, :] for m in adaln_mods(c, p)]
    m_ctx = adaln_mods(c_ctx, p)
    o_ctx, o_lat = mixer(modulate(x_ctx, p['norm_mix_g'], m_ctx[0], m_ctx[1]),
                         modulate(x_lat, p['norm_mix_g'], m_lat[0], m_lat[1]), p, rope, need_ctx)
    x_lat = x_lat + m_lat[2] * o_lat
    x_lat = x_lat + m_lat[5] * conv_ffn(modulate(x_lat, p['norm_ffn_g'], m_lat[3], m_lat[4]), p)
    if need_ctx:
        x_ctx = x_ctx + m_ctx[2] * o_ctx
        x_ctx = x_ctx + m_ctx[5] * conv_ffn(modulate(x_ctx, p['norm_ffn_g'], m_ctx[3], m_ctx[4]), p)
    return x_lat, x_ctx


def setup_inputs(seed: int = 0) -> dict:
    key = jax.random.key(seed)
    keys = iter(jax.random.split(key, 64))

    def nrm(shape, scale):
        return jax.random.normal(next(keys), shape, F32) * scale

    def gain(shape):
        return 1.0 + nrm(shape, 0.02)

    def unif(shape, lo, hi):
        return jax.random.uniform(next(keys), shape, F32, lo, hi)

    ne, no = N_EVEN, N_ODD
    dt0 = jnp.exp(unif((no, 2, SSM_HEADS), math.log(1e-3), math.log(1e-1)))
    return {
        'x': nrm((BATCH, SEQ, D_MODEL), 1.0),
        'c': nrm((BATCH, D_MODEL), 1.0),
        'ctx': nrm((BATCH, CTX_LEN, D_MODEL), 1.0),
        'c_ctx': nrm((D_MODEL,), 1.0),
        'ada_w': nrm((DEPTH, D_MODEL, 6 * D_MODEL), 0.5 * D_MODEL ** -0.5),
        'ada_b': nrm((DEPTH, 6 * D_MODEL), 0.01),
        'norm_mix_g': gain((DEPTH, D_MODEL)),
        'norm_ffn_g': gain((DEPTH, D_MODEL)),
        'ffn_w_up': nrm((DEPTH, D_MODEL, 2 * D_FF), D_MODEL ** -0.5),
        'ffn_conv_w': nrm((DEPTH, FFN_CONV, 2 * D_FF), FFN_CONV ** -0.5),
        'ffn_conv_b': nrm((DEPTH, 2 * D_FF), 0.01),
        'ffn_w_down': nrm((DEPTH, D_FF, D_MODEL), D_FF ** -0.5),
        'final_norm_g': gain((D_MODEL,)),
        'ab_w_in': nrm((ne, D_MODEL, AB_IN), D_MODEL ** -0.5),
        'ab_w_out': nrm((ne, AB_MIX, D_MODEL), AB_MIX ** -0.5),
        'mla_q_norm_g': gain((ne, MLA_Q_RANK)),
        'mla_w_q_up': nrm((ne, MLA_Q_RANK, MLA_HEADS * (MLA_NOPE + MLA_ROPE)), MLA_Q_RANK ** -0.5),
        'mla_kv_norm_g': gain((ne, MLA_KV_RANK)),
        'mla_w_kv_up': nrm((ne, MLA_KV_RANK, MLA_HEADS * (MLA_NOPE + MLA_V)), MLA_KV_RANK ** -0.5),
        'rwkv_mu_prev': unif((ne, RWKV_COLS), 0.0, 0.5),
        'rwkv_mu_next': unif((ne, RWKV_COLS), 0.0, 0.5),
        'rwkv_w0': unif((ne, 2, RWKV_W), -6.0, 0.0),
        'rwkv_w2': nrm((ne, 2, DECAY_LORA, RWKV_W), 0.5 * DECAY_LORA ** -0.5),
        'rwkv_a0': nrm((ne, 2, RWKV_W), 0.1),
        'rwkv_a2': nrm((ne, 2, ICLR_LORA, RWKV_W), 0.5 * ICLR_LORA ** -0.5),
        'rwkv_g2': nrm((ne, GATE_LORA, RWKV_W), GATE_LORA ** -0.5),
        'rwkv_k_k': 0.85 + nrm((ne, RWKV_W), 0.02),
        'rwkv_k_a': gain((ne, RWKV_W)),
        'rwkv_r_k': nrm((ne, RWKV_HEADS, RWKV_N), 0.1),
        'rwkv_ln_g': gain((ne, RWKV_W)),
        'rwkv_ln_b': nrm((ne, RWKV_W), 0.01),
        'cd_w_in': nrm((no, D_MODEL, CD_IN), D_MODEL ** -0.5),
        'cd_w_out': nrm((no, CD_MIX, D_MODEL), CD_MIX ** -0.5),
        'ssm_conv_w': nrm((no, SSM_CONV, SSM_XBC), SSM_CONV ** -0.5),
        'ssm_conv_b': nrm((no, SSM_XBC), 0.01),
        'ssm_dt_bias': dt0 + jnp.log(-jnp.expm1(-dt0)),
        'ssm_a_log': jnp.log(unif((no, 2, SSM_HEADS), 1.0, 16.0)),
        'ssm_d': gain((no, SSM_HEADS)),
        'ssm_norm_g': gain((no, SSM_INNER)),
        'swa_sink': nrm((no, SWA_HEADS), 0.5),
    }


def reference(x, c, ctx, c_ctx, ada_w, ada_b, norm_mix_g, norm_ffn_g, ffn_w_up, ffn_conv_w, ffn_conv_b,
              ffn_w_down, final_norm_g, ab_w_in, ab_w_out, mla_q_norm_g, mla_w_q_up, mla_kv_norm_g,
              mla_w_kv_up, rwkv_mu_prev, rwkv_mu_next, rwkv_w0, rwkv_w2, rwkv_a0, rwkv_a2, rwkv_g2,
              rwkv_k_k, rwkv_k_a, rwkv_r_k, rwkv_ln_g, rwkv_ln_b, cd_w_in, cd_w_out, ssm_conv_w,
              ssm_conv_b, ssm_dt_bias, ssm_a_log, ssm_d, ssm_norm_g, swa_sink):
    n_tok = x.shape[1]
    rope_mla = axial_rope(n_tok, MLA_ROPE)
    rope_swa = axial_rope(n_tok, HEAD_DIM)
    x_lat, x_ctx = x, ctx
    for i in range(DEPTH):
        j = i // 2
        p = {'ada_w': ada_w[i], 'ada_b': ada_b[i], 'norm_mix_g': norm_mix_g[i], 'norm_ffn_g': norm_ffn_g[i],
             'ffn_w_up': ffn_w_up[i], 'ffn_conv_w': ffn_conv_w[i], 'ffn_conv_b': ffn_conv_b[i],
             'ffn_w_down': ffn_w_down[i]}
        if i % 2 == 0:
            p.update({'w_in': ab_w_in[j], 'w_out': ab_w_out[j], 'mla_q_norm_g': mla_q_norm_g[j],
                      'mla_w_q_up': mla_w_q_up[j], 'mla_kv_norm_g': mla_kv_norm_g[j],
                      'mla_w_kv_up': mla_w_kv_up[j], 'rwkv_mu_prev': rwkv_mu_prev[j],
                      'rwkv_mu_next': rwkv_mu_next[j], 'rwkv_w0': rwkv_w0[j], 'rwkv_w2': rwkv_w2[j],
                      'rwkv_a0': rwkv_a0[j], 'rwkv_a2': rwkv_a2[j], 'rwkv_g2': rwkv_g2[j],
                      'rwkv_k_k': rwkv_k_k[j], 'rwkv_k_a': rwkv_k_a[j], 'rwkv_r_k': rwkv_r_k[j],
                      'rwkv_ln_g': rwkv_ln_g[j], 'rwkv_ln_b': rwkv_ln_b[j]})
            mixer, rope = mixer_mla_rwkv, rope_mla
        else:
            p.update({'w_in': cd_w_in[j], 'w_out': cd_w_out[j], 'ssm_conv_w': ssm_conv_w[j],
                      'ssm_conv_b': ssm_conv_b[j], 'ssm_dt_bias': ssm_dt_bias[j], 'ssm_a_log': ssm_a_log[j],
                      'ssm_d': ssm_d[j], 'ssm_norm_g': ssm_norm_g[j], 'swa_sink': swa_sink[j]})
            mixer, rope = mixer_ssd_swa, rope_swa
        x_lat, x_ctx = trunk_layer(x_lat, x_ctx, c, c_ctx, p, mixer, rope, i < DEPTH - 1)
    return rms_norm(x_lat, final_norm_g)
```

```python
import functools
import math

import jax
import jax.numpy as jnp
import numpy as np
from jax import lax
from jax.experimental import pallas as pl
from jax.experimental.pallas import tpu as pltpu

F32 = jnp.float32
BF16 = jnp.bfloat16

GRID_W = 64
ROPE_THETA = 10000.0
NORM_EPS = 1e-6
HEAD_DIM = 64
MLA_HEADS, MLA_NOPE, MLA_ROPE, MLA_V = 8, 64, 32, 64
MLA_Q_RANK, MLA_KV_RANK = 384, 256
RWKV_HEADS, RWKV_N = 8, 64
RWKV_W = RWKV_HEADS * RWKV_N
DECAY_LORA, ICLR_LORA, GATE_LORA = 64, 64, 128
GN_EPS = 64e-5
SSM_HEADS, SSM_P, SSM_G, SSM_N = 16, 64, 2, 128
SSM_INNER = SSM_HEADS * SSM_P
SSM_CONV = 5
SWA_HEADS, SWA_KV_HEADS, WINDOW = 8, 2, 128
D_FF = 2816
FFN_CONV = 3

LANES = 128
HALO = 16
ROW_BLOCK = 256
RWKV_CHUNK = 64
SSD_CHUNK = 128
SWA_BLOCK = 128
FFN_CHUNK = 256
VMEM_LIMIT = 56 * 1024 * 1024
NEG_BIG = -1e30


def _cparams(sem):
    return pltpu.CompilerParams(dimension_semantics=sem, vmem_limit_bytes=VMEM_LIMIT)


def _sigmoid(x):
    return 1.0 / (1.0 + jnp.exp(-x))


def _silu(x):
    return x * _sigmoid(x)


def _rms(x, eps=NORM_EPS):
    return x * lax.rsqrt(jnp.mean(x * x, axis=-1, keepdims=True) + eps)


def _bdot(a, b):
    return jnp.dot(a.astype(BF16), b.astype(BF16), preferred_element_type=F32)


def _bdot_nt(a, b):
    return lax.dot_general(a.astype(BF16), b.astype(BF16), (((1,), (1,)), ((), ())),
                           preferred_element_type=F32)


def _bdot_tn(a, b):
    return lax.dot_general(a.astype(BF16), b.astype(BF16), (((0,), (0,)), ((), ())),
                           preferred_element_type=F32)


def _split3(x):
    hi = x.astype(BF16)
    r1 = x - hi.astype(F32)
    mid = r1.astype(BF16)
    lo = (r1 - mid.astype(F32)).astype(BF16)
    return hi, mid, lo


def _dot_exact_rhs(x, m):
    hi, mid, lo = _split3(x)
    return (jnp.dot(hi, m, preferred_element_type=F32) + jnp.dot(mid, m, preferred_element_type=F32)
            + jnp.dot(lo, m, preferred_element_type=F32))


def _dot_exact_lhs(m, x):
    hi, mid, lo = _split3(x)
    return (jnp.dot(m, hi, preferred_element_type=F32) + jnp.dot(m, mid, preferred_element_type=F32)
            + jnp.dot(m, lo, preferred_element_type=F32))


def _dot_exact_rhs_nt(x, m):
    hi, mid, lo = _split3(x)
    dn = (((1,), (1,)), ((), ()))
    return (lax.dot_general(hi, m, dn, preferred_element_type=F32)
            + lax.dot_general(mid, m, dn, preferred_element_type=F32)
            + lax.dot_general(lo, m, dn, preferred_element_type=F32))


def _rope(t, cos, sin_a, sin_b, half):
    w = t.shape[-1]
    return t * cos + pltpu.roll(t, w - half, 1) * sin_a + pltpu.roll(t, half, 1) * sin_b


def _modnorm(x, g, shift, scale):
    return (_rms(x) * g) * (1.0 + scale) + shift


def _seg_flags(i, starts, ends):
    prev_ok = functools.reduce(jnp.logical_and, [i != s for s in starts])
    next_ok = functools.reduce(jnp.logical_and, [i != (e - 1) for e in ends])
    return prev_ok, next_ok


def _h_with_halo(x_ref, xp_ref, xn_ref, g, shift, scale, prev_ok, next_ok):
    h = _modnorm(x_ref[0], g, shift, scale)
    hp = _modnorm(xp_ref[0], g, shift, scale) * prev_ok.astype(F32)
    hn = _modnorm(xn_ref[0], g, shift, scale) * next_ok.astype(F32)
    return jnp.concatenate([hp.astype(BF16), h.astype(BF16), hn.astype(BF16)], axis=0)


def _shift_rows(x, delta, t):
    m = x.shape[0]
    return pltpu.roll(x, (m - delta) % m, 0)[HALO:HALO + t]


def _ada_kernel(c_ref, w_ref, b_ref, o_ref):
    o_ref[0] = _bdot(_silu(c_ref[...]), w_ref[0]) + b_ref[0]


def _ada_mods(cond, ada_w, ada_b):
    depth, d, n = ada_w.shape
    rows = cond.shape[0]
    tn = 1024
    return pl.pallas_call(
        _ada_kernel,
        grid=(depth, n // tn),
        in_specs=[pl.BlockSpec((rows, d), lambda l, j: (0, 0)),
                  pl.BlockSpec((1, d, tn), lambda l, j: (l, 0, j)),
                  pl.BlockSpec((1, 1, tn), lambda l, j: (l, 0, j))],
        out_specs=pl.BlockSpec((1, rows, tn), lambda l, j: (l, 0, j)),
        out_shape=jax.ShapeDtypeStruct((depth, rows, n), F32),
        compiler_params=_cparams(("parallel", "parallel")),
        name="ada_mods",
    )(cond, ada_w, ada_b.reshape(depth, 1, n))


def _row_specs(t, w, nrows8, off=0):
    per = t // HALO
    nh = nrows8
    main = pl.BlockSpec((1, t, w), lambda b, i: (b, i + off, 0))
    prev = pl.BlockSpec((1, HALO, w), lambda b, i: (b, jnp.maximum((i + off) * per - 1, 0), 0))
    nxt = pl.BlockSpec((1, HALO, w), lambda b, i: (b, jnp.minimum((i + off + 1) * per, nh - 1), 0))
    return main, prev, nxt


def _const_spec(shape):
    nd = len(shape)
    return pl.BlockSpec(shape, lambda *_: (0,) * nd)


def _ab_in_kernel(x_ref, xp_ref, xn_ref, mods_ref, g_ref, cos_ref, sa_ref, sb_ref, win_ref,
                  gq_ref, wq_ref, gkv_ref, wkk_ref, wvv_ref, mup_ref, mun_ref, g2_ref,
                  w0_ref, w2_ref, a0_ref, a2_ref, kkk_ref, ka_ref, rk_ref, bd_ref,
                  q_out, k_out, v_out, r_out, vv_out, kk_out, g_out, bonus_out,
                  lw_out, ke_out, b_out, *, starts, ends):
    i = pl.program_id(1)
    t = x_ref.shape[1]
    prev_ok, next_ok = _seg_flags(i, starts, ends)
    m = mods_ref[0, 0]
    h_all = _h_with_halo(x_ref, xp_ref, xn_ref, g_ref[...], m[0:1], m[1:2], prev_ok, next_ok)
    pin = jnp.dot(h_all, win_ref[...], preferred_element_type=F32)

    cos, sa, sb = cos_ref[...], sa_ref[...], sb_ref[...]
    main = pin[HALO:HALO + t]
    cq, ckv, kr = main[:, 0:384], main[:, 384:640], main[:, 640:768]
    qf = _bdot(_rms(cq) * gq_ref[...], wq_ref[...])
    kvn = (_rms(ckv) * gkv_ref[...]).astype(BF16)
    kf = jnp.dot(kvn, wkk_ref[...], preferred_element_type=F32)
    v_out[0] = jnp.dot(kvn, wvv_ref[...], preferred_element_type=F32).astype(BF16)
    krr = _rope(kr, cos, sa, sb, MLA_ROPE // 2)
    for h in range(MLA_HEADS):
        sl = slice(h * LANES, (h + 1) * LANES)
        q_out[0, h] = _rope(qf[:, sl], cos, sa, sb, MLA_ROPE // 2).astype(BF16)
        k_out[0, h] = (kf[:, sl] + krr).astype(BF16)

    rw = pin[:, 768:]
    x = rw[HALO:HALO + t]
    xp = _shift_rows(rw, -1, t)
    xn = _shift_rows(rw, 1, t)
    xs = x + mup_ref[...] * (xp - x) + mun_ref[...] * (xn - x)
    w = RWKV_W
    r, k, v = xs[:, 0:w], xs[:, w:2 * w], xs[:, 2 * w:3 * w]
    xw, xa, xg = xs[:, 3 * w:3 * w + 128], xs[:, 3 * w + 128:3 * w + 256], xs[:, 3 * w + 256:3 * w + 384]
    bd = bd_ref[...]
    g_out[0] = _bdot(_sigmoid(xg), g2_ref[...])
    kkr = k * kkk_ref[...]
    kk = kkr * lax.rsqrt(_dot_exact_rhs(kkr * kkr, bd) + 1e-12)
    r_out[0] = r
    vv_out[0] = v
    kk_out[0] = kk
    bonus_out[0] = _dot_exact_rhs(r * k * rk_ref[...], bd) * v
    tw = jnp.tanh(xw).astype(BF16)
    xab = xa.astype(BF16)
    for d in range(2):
        wl = w0_ref[d] + jnp.dot(tw, w2_ref[d], preferred_element_type=F32)
        lw_out[d, 0] = -math.exp(-0.5) * _sigmoid(wl)
        a = _sigmoid(a0_ref[d] + jnp.dot(xab, a2_ref[d], preferred_element_type=F32))
        ke_out[d, 0] = k * (1.0 + (a - 1.0) * ka_ref[...])
        b_out[d, 0] = kk * a


def _ab_in(x, mods, p, rope, lc):
    b, ltot, d = x.shape
    t = ROW_BLOCK
    nb, nbc = ltot // t, lc // t
    kern = functools.partial(_ab_in_kernel, starts=(0, nbc), ends=(nbc, nb))
    main, prev, nxt = _row_specs(t, d, ltot // HALO)
    tab = pl.BlockSpec((t, LANES), lambda bb, i: (i, 0))
    consts = [p['win'], p['gq'], p['wq'], p['gkv'], p['wkk'], p['wvv'], p['mup'], p['mun'], p['g2'],
              p['w0'], p['w2'], p['a0'], p['a2'], p['kkk'], p['ka'], p['rk'], p['bd']]
    in_specs = ([main, prev, nxt,
                 pl.BlockSpec((1, 1, 6, d), lambda bb, i: (bb, jnp.where(i >= nbc, 1, 0), 0, 0)),
                 _const_spec((1, d)), tab, tab, tab] + [_const_spec(c.shape) for c in consts])
    hq = pl.BlockSpec((1, MLA_HEADS, t, LANES), lambda bb, i: (bb, 0, i, 0))
    row = pl.BlockSpec((1, t, RWKV_W), lambda bb, i: (bb, i, 0))
    drow = pl.BlockSpec((2, 1, t, RWKV_W), lambda bb, i: (0, bb, i, 0))
    f_rows = jax.ShapeDtypeStruct((b, ltot, RWKV_W), F32)
    f_drows = jax.ShapeDtypeStruct((2, b, ltot, RWKV_W), F32)
    return pl.pallas_call(
        kern, grid=(b, nb), in_specs=in_specs,
        out_specs=[hq, hq, row, row, row, row, row, row, drow, drow, drow],
        out_shape=[jax.ShapeDtypeStruct((b, MLA_HEADS, ltot, LANES), BF16),
                   jax.ShapeDtypeStruct((b, MLA_HEADS, ltot, LANES), BF16),
                   jax.ShapeDtypeStruct((b, ltot, MLA_HEADS * MLA_V), BF16),
                   f_rows, f_rows, f_rows, f_rows, f_rows, f_drows, f_drows, f_drows],
        compiler_params=_cparams(("parallel", "parallel")),
        name="ab_in",
    )(x, x, x, mods, p['norm_g'], rope[0], rope[1], rope[2], *consts)


def _mla_attn_kernel(q_ref, k_ref, v_ref, o_ref, *, nk_ctx, scale):
    i = pl.program_id(1)
    nk_all = k_ref.shape[2]

    def attend(nk):
        for hp in range(MLA_HEADS // 2):
            outs = []
            for h in (2 * hp, 2 * hp + 1):
                s = lax.dot_general(q_ref[0, h], k_ref[0, h, 0:nk], (((1,), (1,)), ((), ())),
                                    preferred_element_type=F32) * scale
                mx = jnp.max(s, axis=-1, keepdims=True)
                e = jnp.exp(s - mx)
                den = jnp.sum(e, axis=-1, keepdims=True)
                o2 = jnp.dot(e.astype(BF16), v_ref[0, 0:nk, hp * LANES:(hp + 1) * LANES],
                             preferred_element_type=F32)
                outs.append(o2 / den)
            lane = lax.broadcasted_iota(jnp.int32, outs[0].shape, 1)
            o_ref[0, :, hp * LANES:(hp + 1) * LANES] = jnp.where(lane < MLA_V, outs[0], outs[1])

    pl.when(i == 0)(lambda: attend(nk_ctx))
    pl.when(i > 0)(lambda: attend(nk_all))


def _mla_attn(q, k, v, lc):
    b, hh, ltot, _ = q.shape
    t = lc
    kern = functools.partial(_mla_attn_kernel, nk_ctx=lc, scale=(MLA_NOPE + MLA_ROPE) ** -0.5)
    return pl.pallas_call(
        kern, grid=(b, ltot // t),
        in_specs=[pl.BlockSpec((1, hh, t, LANES), lambda bb, i: (bb, 0, i, 0)),
                  pl.BlockSpec((1, hh, ltot, LANES), lambda bb, i: (bb, 0, 0, 0)),
                  pl.BlockSpec((1, ltot, hh * MLA_V), lambda bb, i: (bb, 0, 0))],
        out_specs=pl.BlockSpec((1, t, hh * MLA_V), lambda bb, i: (bb, i, 0)),
        out_shape=jax.ShapeDtypeStruct((b, ltot, hh * MLA_V), F32),
        compiler_params=_cparams(("parallel", "arbitrary")),
        name="mla_attn",
    )(q, k, v)


def _rwkv_scan_kernel(r_ref, v_ref, kk_ref, lw_ref, ke_ref, b_ref, y_ref, s_ref):
    d = pl.program_id(0)
    n = pl.program_id(2)
    c = RWKV_CHUNK

    @pl.when(n == 0)
    def _():
        s_ref[...] = jnp.zeros_like(s_ref)

    row = lax.broadcasted_iota(jnp.int32, (c, c), 0)
    col = lax.broadcasted_iota(jnp.int32, (c, c), 1)
    ahead = jnp.where(d == 0, row - col, col - row)
    incl = ahead >= 0
    strict = ahead > 0
    eye = (row == col).astype(F32)

    lw = lw_ref[0, 0]
    cum = _dot_exact_lhs(incl.astype(BF16), lw)
    tot = jnp.sum(lw, axis=0, keepdims=True)
    r, v, kk, ke, bb = r_ref[0], v_ref[0], kk_ref[0], ke_ref[0, 0], b_ref[0, 0]
    e_nc = jnp.exp(-cum)
    e_end = jnp.exp(tot - cum)
    rt = r * jnp.exp(cum)
    kt = kk * jnp.exp(cum - lw)
    bh, kh = bb * e_nc, ke * e_nc
    bc, kc = bb * e_end, ke * e_end
    gam = jnp.exp(tot)

    for h in range(RWKV_HEADS):
        sl = slice(h * RWKV_N, (h + 1) * RWKV_N)
        a = jnp.concatenate([kt[:, sl], rt[:, sl]], axis=0).astype(BF16)
        bm = jnp.concatenate([bh[:, sl], kh[:, sl]], axis=0).astype(BF16)
        g = lax.dot_general(a, bm, (((1,), (1,)), ((), ())), preferred_element_type=F32)
        m = jnp.where(strict, g[0:c, 0:c], 0.0)
        nn = jnp.where(strict, g[0:c, c:2 * c], 0.0)
        pb = jnp.where(incl, g[c:2 * c, 0:c], 0.0)
        pk = jnp.where(incl, g[c:2 * c, c:2 * c], 0.0)
        xinv = eye - m
        pw = _bdot(m, m)
        steps = int(math.log2(c)) - 1
        for it in range(steps):
            xinv = xinv + _bdot(xinv, pw)
            if it < steps - 1:
                pw = _bdot(pw, pw)
        s0 = s_ref[h]
        vh = v[:, sl]
        at = lax.dot_general(a, s0.astype(BF16), (((1,), (1,)), ((), ())), preferred_element_type=F32)
        nv = _bdot(jnp.concatenate([nn, pk], axis=0), vh)
        u = -_bdot(xinv, at[0:c] + nv[0:c])
        y_ref[0, 0, :, sl] = at[c:2 * c] + nv[c:2 * c] + _bdot(pb, u)
        uv = jnp.concatenate([u, vh], axis=0)
        bk = jnp.concatenate([bc[:, sl], kc[:, sl]], axis=0)
        s_ref[h] = s0 * gam[:, sl] + _bdot_tn(uv, bk)


def _scan_chunk_index(n, nc_ctx, nc, d):
    bwd = jnp.where(n < nc_ctx, nc_ctx - 1 - n, nc - 1 - (n - nc_ctx))
    return jnp.where(d == 0, n, bwd)


def _rwkv_scan(r, v, kk, lw, ke, bb, lc):
    b, ltot, w = r.shape
    c = RWKV_CHUNK
    nc, ncc = ltot // c, lc // c
    shared = pl.BlockSpec((1, c, w), lambda d, bi, n: (bi, _scan_chunk_index(n, ncc, nc, d), 0))
    perdir = pl.BlockSpec((1, 1, c, w), lambda d, bi, n: (d, bi, _scan_chunk_index(n, ncc, nc, d), 0))
    return pl.pallas_call(
        _rwkv_scan_kernel, grid=(2, b, nc),
        in_specs=[shared, shared, shared, perdir, perdir, perdir],
        out_specs=perdir,
        out_shape=jax.ShapeDtypeStruct((2, b, ltot, w), F32),
        scratch_shapes=[pltpu.VMEM((RWKV_HEADS, RWKV_N, RWKV_N), F32)],
        compiler_params=_cparams(("parallel", "parallel", "arbitrary")),
        name="rwkv_scan",
    )(r, v, kk, lw, ke, bb)


def _ab_out_kernel(x_ref, att_ref, y_ref, bonus_ref, g_ref, mods_ref, lng_ref, lnb_ref, bd_ref,
                   wout_ref, o_ref):
    y = y_ref[0, 0] + y_ref[1, 0]
    bd = bd_ref[...]
    mu = _dot_exact_rhs(y, bd) * (1.0 / RWKV_N)
    dl = y - mu
    var = _dot_exact_rhs(dl * dl, bd) * (1.0 / RWKV_N)
    yn = dl * lax.rsqrt(var + GN_EPS) * lng_ref[...] + lnb_ref[...]
    rwo = (yn + bonus_ref[0]) * g_ref[0]
    feat = jnp.concatenate([att_ref[0].astype(BF16), rwo.astype(BF16)], axis=1)
    o = jnp.dot(feat, wout_ref[...], preferred_element_type=F32)
    o_ref[0] = x_ref[0] + mods_ref[0, 0][2:3] * o


def _ab_out(x, att, y, bonus, g, mods, p, lc):
    b, ltot, d = x.shape
    t = ROW_BLOCK
    nbc = lc // t
    row = lambda w: pl.BlockSpec((1, t, w), lambda bb, i: (bb, i, 0))
    consts = [p['lng'], p['lnb'], p['bd'], p['wout']]
    return pl.pallas_call(
        _ab_out_kernel, grid=(b, ltot // t),
        in_specs=[row(d), row(RWKV_W), pl.BlockSpec((2, 1, t, RWKV_W), lambda bb, i: (0, bb, i, 0)),
                  row(RWKV_W), row(RWKV_W),
                  pl.BlockSpec((1, 1, 6, d), lambda bb, i: (bb, jnp.where(i >= nbc, 1, 0), 0, 0))]
        + [_const_spec(c.shape) for c in consts],
        out_specs=row(d),
        out_shape=jax.ShapeDtypeStruct((b, ltot, d), F32),
        compiler_params=_cparams(("parallel", "parallel")),
        name="ab_out",
    )(x, att, y, bonus, g, mods, *consts)


def _ffn_kernel(x_ref, xp_ref, xn_ref, mods_ref, g_ref, wg_ref, wv_ref, cwg_ref, cwv_ref,
                cbg_ref, cbv_ref, wd_ref, fg_ref, o_ref, acc_ref, *, starts, ends, final_norm):
    i = pl.program_id(1)
    t = x_ref.shape[1]
    prev_ok, next_ok = _seg_flags(i, starts, ends)
    m = mods_ref[0, 0]
    h_all = _h_with_halo(x_ref, xp_ref, xn_ref, g_ref[...], m[3:4], m[4:5], prev_ok, next_ok)
    acc_ref[...] = jnp.zeros_like(acc_ref)

    def conv(u, cw, cb):
        return (cw[0:1] * _shift_rows(u, -1, t) + cw[1:2] * u[HALO:HALO + t]
                + cw[2:3] * _shift_rows(u, 1, t) + cb)

    def step(c, carry):
        ug = jnp.dot(h_all, wg_ref[c], preferred_element_type=F32)
        uv = jnp.dot(h_all, wv_ref[c], preferred_element_type=F32)
        act = _silu(conv(ug, cwg_ref[c], cbg_ref[c])) * conv(uv, cwv_ref[c], cbv_ref[c])
        acc_ref[...] += jnp.dot(act.astype(BF16), wd_ref[c], preferred_element_type=F32)
        return carry

    lax.fori_loop(0, wg_ref.shape[0], step, 0)
    out = x_ref[0] + m[5:6] * acc_ref[...]
    if final_norm:
        out = _rms(out) * fg_ref[...]
    o_ref[0] = out


def _ffn(x, mods, p, seg_bounds, seg_index, final_g):
    b, rows, d = x.shape
    t = ROW_BLOCK
    starts, ends = tuple(seg_bounds[:-1]), tuple(seg_bounds[1:])
    kern = functools.partial(_ffn_kernel, starts=starts, ends=ends, final_norm=final_g is not None)
    main, prev, nxt = _row_specs(t, d, rows // HALO)
    fg = final_g if final_g is not None else jnp.ones((1, d), F32)
    consts = [p['wg'], p['wv'], p['cwg'], p['cwv'], p['cbg'], p['cbv'], p['wd'], fg]
    return pl.pallas_call(
        kern, grid=(b, rows // t),
        in_specs=[main, prev, nxt,
                  pl.BlockSpec((1, 1, 6, d), lambda bb, i: (bb, seg_index(i), 0, 0)),
                  _const_spec((1, d))] + [_const_spec(c.shape) for c in consts],
        out_specs=pl.BlockSpec((1, t, d), lambda bb, i: (bb, i, 0)),
        out_shape=jax.ShapeDtypeStruct((b, rows, d), F32),
        scratch_shapes=[pltpu.VMEM((t, d), F32)],
        compiler_params=_cparams(("parallel", "parallel")),
        name="conv_ffn",
    )(x, x, x, mods, p['norm_g'], *consts)


def _cd_in_kernel(x_ref, xp_ref, xn_ref, mods_ref, g_ref, cos_ref, sa_ref, sb_ref, win_ref,
                  cw_ref, cb_ref, dtb_ref,
                  z_out, xs_out, bc_out, dt_out, q_out, k_out, v_out, *, starts, ends):
    i = pl.program_id(1)
    t = x_ref.shape[1]
    prev_ok, next_ok = _seg_flags(i, starts, ends)
    m = mods_ref[0, 0]
    h_all = _h_with_halo(x_ref, xp_ref, xn_ref, g_ref[...], m[0:1], m[1:2], prev_ok, next_ok)
    pin = jnp.dot(h_all, win_ref[...], preferred_element_type=F32)
    main = pin[HALO:HALO + t]
    z_out[0] = main[:, 0:1024]
    xbc = pin[:, 1024:2560]
    cw = cw_ref[...]
    conv = cb_ref[...] + cw[2:3] * xbc[HALO:HALO + t]
    for j in (0, 1, 3, 4):
        conv = conv + cw[j:j + 1] * _shift_rows(xbc, j - SSM_CONV // 2, t)
    xc = _silu(conv)
    xs_out[0] = xc[:, 0:SSM_INNER]
    bc_out[0] = xc[:, SSM_INNER:]
    raw = main[:, 2560:2688] + dtb_ref[...]
    dt_out[0] = jnp.maximum(raw, 0.0) + jnp.log(1.0 + jnp.exp(-jnp.abs(raw)))
    cos, sa, sb = cos_ref[...], sa_ref[...], sb_ref[...]
    q = main[:, 2688:3200]
    tile = lambda a: jnp.concatenate([a] * (q.shape[1] // LANES), axis=1)
    q_out[0] = _rope(q, tile(cos), tile(sa), tile(sb), HEAD_DIM // 2).astype(BF16)
    k_out[0] = _rope(main[:, 3200:3328], cos, sa, sb, HEAD_DIM // 2).astype(BF16)
    v_out[0] = main[:, 3328:3456].astype(BF16)


def _cd_in(x, mods, p, rope, lc):
    b, ltot, d = x.shape
    t = ROW_BLOCK
    nb, nbc = ltot // t, lc // t
    kern = functools.partial(_cd_in_kernel, starts=(0, nbc), ends=(nbc, nb))
    main, prev, nxt = _row_specs(t, d, ltot // HALO)
    tab = pl.BlockSpec((t, LANES), lambda bb, i: (i, 0))
    consts = [p['win'], p['cw'], p['cb'], p['dtb']]
    row = lambda w: pl.BlockSpec((1, t, w), lambda bb, i: (bb, i, 0))
    sds = lambda w, dt: jax.ShapeDtypeStruct((b, ltot, w), dt)
    kvw = SWA_KV_HEADS * HEAD_DIM
    return pl.pallas_call(
        kern, grid=(b, nb),
        in_specs=[main, prev, nxt,
                  pl.BlockSpec((1, 1, 6, d), lambda bb, i: (bb, jnp.where(i >= nbc, 1, 0), 0, 0)),
                  _const_spec((1, d)), tab, tab, tab] + [_const_spec(c.shape) for c in consts],
        out_specs=[row(SSM_INNER), row(SSM_INNER), row(2 * SSM_G * SSM_N), row(LANES),
                   row(SWA_HEADS * HEAD_DIM), row(kvw), row(kvw)],
        out_shape=[sds(SSM_INNER, F32), sds(SSM_INNER, F32), sds(2 * SSM_G * SSM_N, F32), sds(LANES, F32),
                   sds(SWA_HEADS * HEAD_DIM, BF16), sds(kvw, BF16), sds(kvw, BF16)],
        compiler_params=_cparams(("parallel", "parallel")),
        name="cd_in",
    )(x, x, x, mods, p['norm_g'], rope[0], rope[1], rope[2], *consts)


def _ssd_kernel(xs_ref, bc_ref, dt_ref, dtt_ref, a_ref, at_ref, ex_ref, dsk_ref, y_ref, h_ref):
    d = pl.program_id(0)
    n = pl.program_id(2)
    q = SSD_CHUNK
    hg = SSM_HEADS // SSM_G
    gw = hg * SSM_P

    @pl.when(n == 0)
    def _():
        h_ref[...] = jnp.zeros_like(h_ref)

    row = lax.broadcasted_iota(jnp.int32, (q, q), 0)
    col = lax.broadcasted_iota(jnp.int32, (q, q), 1)
    fwd = d == 0
    incl = jnp.where(fwd, row - col, col - row) >= 0
    inclb = incl.astype(BF16)

    dt_c = dt_ref[0, 0]
    dt_r = dtt_ref[0, 0]
    acum_c = _dot_exact_lhs(inclb, dt_c * a_ref[0])
    acum_r = _dot_exact_rhs_nt(dt_r * at_ref[0], inclb)
    ex = ex_ref[...]
    acum_x = _dot_exact_rhs(acum_c, ex)
    dt_x = _dot_exact_rhs(dt_c, ex)
    end_x = jnp.where(fwd, acum_x[q - 1:q], acum_x[0:1])
    grow = jnp.exp(acum_x)
    tail = jnp.exp(end_x - acum_x) * dt_x
    chunk_decay = jnp.exp(end_x)
    skip = dsk_ref[...] * (d == 0).astype(F32)

    xs = xs_ref[0]
    bcm = bc_ref[0]
    for g in range(SSM_G):
        bg = bcm[:, g * SSM_N:(g + 1) * SSM_N].astype(BF16)
        cg = bcm[:, (SSM_G + g) * SSM_N:(SSM_G + g + 1) * SSM_N].astype(BF16)
        cb = lax.dot_general(cg, bg, (((1,), (1,)), ((), ())), preferred_element_type=F32)
        gs = slice(g * gw, (g + 1) * gw)
        xg = xs[:, gs]
        yd = []
        for hh in range(hg):
            hd = g * hg + hh
            seg = jnp.where(incl, acum_c[:, hd:hd + 1] - acum_r[hd:hd + 1, :], NEG_BIG)
            wgt = cb * jnp.exp(seg) * dt_r[hd:hd + 1, :]
            yd.append(_bdot(wgt, xg[:, hh * SSM_P:(hh + 1) * SSM_P]))
        h_in = h_ref[g]
        y_off = jnp.dot(cg, h_in.astype(BF16), preferred_element_type=F32) * grow[:, gs]
        y_ref[0, 0, :, gs] = jnp.concatenate(yd, axis=1) + y_off + skip[:, gs] * xg
        h_ref[g] = h_in * chunk_decay[:, gs] + _bdot_tn(bg, xg * tail[:, gs])


def _ssd(xs, bcm, dt2, dtt2, p, lc):
    b, ltot, inner = xs.shape
    q = SSD_CHUNK
    nc, ncc = ltot // q, lc // q
    cidx = lambda d, bi, n: _scan_chunk_index(n, ncc, nc, d)
    hg = SSM_HEADS // SSM_G
    return pl.pallas_call(
        _ssd_kernel, grid=(2, b, nc),
        in_specs=[pl.BlockSpec((1, q, inner), lambda d, bi, n: (bi, cidx(d, bi, n), 0)),
                  pl.BlockSpec((1, q, 2 * SSM_G * SSM_N), lambda d, bi, n: (bi, cidx(d, bi, n), 0)),
                  pl.BlockSpec((1, 1, q, LANES), lambda d, bi, n: (d, bi, cidx(d, bi, n), 0)),
                  pl.BlockSpec((1, 1, SSM_HEADS, q), lambda d, bi, n: (d, bi, 0, cidx(d, bi, n))),
                  pl.BlockSpec((1, 1, LANES), lambda d, bi, n: (d, 0, 0)),
                  pl.BlockSpec((1, SSM_HEADS, 1), lambda d, bi, n: (d, 0, 0)),
                  _const_spec(p['ex'].shape), _const_spec(p['dsk'].shape)],
        out_specs=pl.BlockSpec((1, 1, q, inner), lambda d, bi, n: (d, bi, cidx(d, bi, n), 0)),
        out_shape=jax.ShapeDtypeStruct((2, b, ltot, inner), F32),
        scratch_shapes=[pltpu.VMEM((SSM_G, SSM_N, hg * SSM_P), F32)],
        compiler_params=_cparams(("parallel", "parallel", "arbitrary")),
        name="ssd_scan",
    )(xs, bcm, dt2, dtt2, p['a_row'], p['a_col'], p['ex'], p['dsk'])


def _swa_kernel(sink_ref, q_ref, kc_ref, kp_ref, k0_ref, kn_ref, vc_ref, vp_ref, v0_ref, vn_ref, o_ref,
                *, nblocks, scale):
    i = pl.program_id(1)
    t = q_ref.shape[1]
    nctx = kc_ref.shape[1]
    hg = SWA_HEADS // SWA_KV_HEADS
    kall = jnp.concatenate([kc_ref[0], kp_ref[0], k0_ref[0], kn_ref[0]], axis=0)
    vall = jnp.concatenate([vc_ref[0], vp_ref[0], v0_ref[0], vn_ref[0]], axis=0)
    nk = nctx + 3 * t
    r = lax.broadcasted_iota(jnp.int32, (hg * t, nk), 0) % t
    c = lax.broadcasted_iota(jnp.int32, (hg * t, nk), 1) - nctx
    lo = jnp.where(i > 0, r, t)
    hi = jnp.where(i < nblocks - 1, r + 2 * t, 2 * t - 1)
    okg = jnp.logical_or(c < 0, jnp.logical_and(c >= lo, c <= hi))
    rowh = lax.broadcasted_iota(jnp.int32, (hg * t, 1), 0) // t
    qb = q_ref[0]
    for g in range(SWA_KV_HEADS):
        qg = jnp.concatenate([qb[:, (g * hg + j) * HEAD_DIM:(g * hg + j + 1) * HEAD_DIM] for j in range(hg)],
                             axis=0)
        kg = kall[:, g * HEAD_DIM:(g + 1) * HEAD_DIM]
        vg = vall[:, g * HEAD_DIM:(g + 1) * HEAD_DIM]
        s = lax.dot_general(qg, kg, (((1,), (1,)), ((), ())), preferred_element_type=F32) * scale
        s = jnp.where(okg, s, NEG_BIG)
        sink = jnp.zeros((hg * t, 1), F32)
        for j in range(hg):
            sink = jnp.where(rowh == j, sink_ref[g * hg + j], sink)
        mx = jnp.maximum(jnp.max(s, axis=-1, keepdims=True), sink)
        e = jnp.exp(s - mx)
        den = jnp.sum(e, axis=-1, keepdims=True) + jnp.exp(sink - mx)
        og = jnp.dot(e.astype(BF16), vg, preferred_element_type=F32) / den
        for j in range(hg):
            hd = g * hg + j
            o_ref[0, :, hd * HEAD_DIM:(hd + 1) * HEAD_DIM] = og[j * t:(j + 1) * t]


def _swa(q, k, v, sink, lc):
    b, ltot, qw = q.shape
    t = SWA_BLOCK
    nb = (ltot - lc) // t
    off = lc // t
    kvw = k.shape[-1]
    kern = functools.partial(_swa_kernel, nblocks=nb, scale=HEAD_DIM ** -0.5)
    ctx = pl.BlockSpec((1, lc, kvw), lambda bb, i: (bb, 0, 0))
    prev = pl.BlockSpec((1, t, kvw), lambda bb, i: (bb, off + jnp.maximum(i - 1, 0), 0))
    cur = pl.BlockSpec((1, t, kvw), lambda bb, i: (bb, off + i, 0))
    nxt = pl.BlockSpec((1, t, kvw), lambda bb, i: (bb, off + jnp.minimum(i + 1, nb - 1), 0))
    return pl.pallas_call(
        kern, grid=(b, nb),
        in_specs=[pl.BlockSpec(memory_space=pltpu.SMEM),
                  pl.BlockSpec((1, t, qw), lambda bb, i: (bb, off + i, 0)),
                  ctx, prev, cur, nxt, ctx, prev, cur, nxt],
        out_specs=pl.BlockSpec((1, t, qw), lambda bb, i: (bb, i, 0)),
        out_shape=jax.ShapeDtypeStruct((b, ltot - lc, qw), F32),
        compiler_params=_cparams(("parallel", "parallel")),
        name="swa_attn",
    )(sink, q, k, k, k, k, v, v, v, v)


def _cd_out_kernel(x_ref, y_ref, z_ref, att_ref, mods_ref, ng_ref, wout_ref, o_ref):
    u = (y_ref[0, 0] + y_ref[1, 0]) * _silu(z_ref[0])
    gw = SSM_INNER // SSM_G
    un = jnp.concatenate([_rms(u[:, g * gw:(g + 1) * gw]) for g in range(SSM_G)], axis=1) * ng_ref[...]
    feat = jnp.concatenate([un.astype(BF16), att_ref[0].astype(BF16)], axis=1)
    o = jnp.dot(feat, wout_ref[...], preferred_element_type=F32)
    o_ref[0] = x_ref[0] + mods_ref[0, 0][2:3] * o


def _cd_out(x, y, z, att, mods, p, lc):
    b, ltot, d = x.shape
    t = ROW_BLOCK
    off = lc // t
    nbl = (ltot - lc) // t
    full = lambda w: pl.BlockSpec((1, t, w), lambda bb, i: (bb, i + off, 0))
    return pl.pallas_call(
        _cd_out_kernel, grid=(b, nbl),
        in_specs=[full(d), pl.BlockSpec((2, 1, t, SSM_INNER), lambda bb, i: (0, bb, i + off, 0)),
                  full(SSM_INNER), pl.BlockSpec((1, t, att.shape[-1]), lambda bb, i: (bb, i, 0)),
                  pl.BlockSpec((1, 1, 6, d), lambda bb, i: (bb, 1, 0, 0)),
                  _const_spec(p['ng'].shape), _const_spec(p['wout'].shape)],
        out_specs=pl.BlockSpec((1, t, d), lambda bb, i: (bb, i, 0)),
        out_shape=jax.ShapeDtypeStruct((b, ltot - lc, d), F32),
        compiler_params=_cparams(("parallel", "parallel")),
        name="cd_out",
    )(x, y, z, att, mods, p['ng'], p['wout'])


def _pad_cols(w, width):
    return jnp.pad(w, ((0, 0), (0, width - w.shape[1])))


def _rope_tables(n_tok, lc, rot_dim, place):
    rows = n_tok // GRID_W
    rp, cp = jnp.meshgrid(jnp.arange(rows, dtype=F32), jnp.arange(GRID_W, dtype=F32), indexing='ij')
    n_freq = rot_dim // 4
    inv_freq = ROPE_THETA ** (-jnp.arange(n_freq, dtype=F32) / n_freq)
    ang = jnp.concatenate([rp.reshape(-1, 1) * inv_freq, cp.reshape(-1, 1) * inv_freq], axis=-1)
    cos = jnp.concatenate([jnp.ones((lc, rot_dim // 2), F32), jnp.cos(ang)], axis=0)
    sin = jnp.concatenate([jnp.zeros((lc, rot_dim // 2), F32), jnp.sin(ang)], axis=0)
    zero = jnp.zeros_like(sin)
    return place(cos, cos, 1.0), place(-sin, zero, 0.0), place(zero, sin, 0.0)


def _mla_place(first, second, fill):
    n = first.shape[0]
    return jnp.concatenate([jnp.full((n, MLA_NOPE), fill, F32), first, second,
                            jnp.zeros((n, LANES - MLA_NOPE - MLA_ROPE), F32)], axis=1)


def _swa_place(first, second, fill):
    return jnp.concatenate([first, second] * (LANES // HEAD_DIM), axis=1)


def _block_diag_ones(n, group):
    idx = np.arange(n) // group
    return jnp.asarray(idx[:, None] == idx[None, :], BF16)


def _ffn_params(i, norm_ffn_g, ffn_w_up, ffn_conv_w, ffn_conv_b, ffn_w_down):
    d = ffn_w_up.shape[1]
    nch = D_FF // FFN_CHUNK
    chunks = lambda w: w.reshape(w.shape[0], nch, FFN_CHUNK).transpose(1, 0, 2)
    wup = ffn_w_up[i].astype(BF16)
    return {
        'norm_g': norm_ffn_g[i].reshape(1, d),
        'wg': chunks(wup[:, :D_FF]), 'wv': chunks(wup[:, D_FF:]),
        'cwg': chunks(ffn_conv_w[i][:, :D_FF]), 'cwv': chunks(ffn_conv_w[i][:, D_FF:]),
        'cbg': chunks(ffn_conv_b[i][None, :D_FF]), 'cbv': chunks(ffn_conv_b[i][None, D_FF:]),
        'wd': ffn_w_down[i].astype(BF16).reshape(nch, FFN_CHUNK, d),
    }


def kernel(x, c, ctx, c_ctx, ada_w, ada_b, norm_mix_g, norm_ffn_g, ffn_w_up, ffn_conv_w, ffn_conv_b, ffn_w_down, final_norm_g, ab_w_in, ab_w_out, mla_q_norm_g, mla_w_q_up, mla_kv_norm_g, mla_w_kv_up, rwkv_mu_prev, rwkv_mu_next, rwkv_w0, rwkv_w2, rwkv_a0, rwkv_a2, rwkv_g2, rwkv_k_k, rwkv_k_a, rwkv_r_k, rwkv_ln_g, rwkv_ln_b, cd_w_in, cd_w_out, ssm_conv_w, ssm_conv_b, ssm_dt_bias, ssm_a_log, ssm_d, ssm_norm_g, swa_sink):
    b, l, d = x.shape
    lc = ctx.shape[1]
    assert ada_w.shape[0] == 2 and lc % ROW_BLOCK == 0 and l % ROW_BLOCK == 0
    ltot = lc + l
    xall = jnp.concatenate([ctx, x], axis=1)

    nrow = -(-(b + 1) // 8) * 8
    cond = jnp.zeros((nrow, d), F32).at[:b].set(c).at[b].set(c_ctx)
    ada = _ada_mods(cond, ada_w, ada_b)
    def mods_of(i):
        lat = ada[i, :b].reshape(b, 1, 6, d)
        cx = jnp.broadcast_to(ada[i, b].reshape(1, 1, 6, d), (b, 1, 6, d))
        return jnp.concatenate([cx, lat], axis=1)

    row = lambda v: v.reshape(1, -1)
    nbc, nb = lc // ROW_BLOCK, ltot // ROW_BLOCK
    seg_of = lambda i: jnp.where(i >= nbc, 1, 0)

    w_in = ab_w_in[0]
    padw = lambda w, n: jnp.pad(w, ((0, 0), (0, n - w.shape[1])))
    o1, o2, o3 = MLA_Q_RANK, MLA_Q_RANK + MLA_KV_RANK, MLA_Q_RANK + MLA_KV_RANK + MLA_ROPE
    w_kr = jnp.pad(w_in[:, o2:o3], ((0, 0), (MLA_NOPE, LANES - MLA_NOPE - MLA_ROPE)))
    rw = w_in[:, o3:]
    rw_sizes = (3 * RWKV_W, DECAY_LORA, ICLR_LORA, GATE_LORA)
    def pad_rw(v):
        a, bq, cq_, dq = jnp.split(v, np.cumsum(rw_sizes)[:-1].tolist(), axis=-1)
        z = jnp.zeros(v.shape[:-1] + (LANES - DECAY_LORA,), v.dtype)
        return jnp.concatenate([a, bq, z, cq_, z, dq], axis=-1)
    win_ab = jnp.concatenate([w_in[:, :o2], w_kr, pad_rw(rw)], axis=1).astype(BF16)
    qd = MLA_NOPE + MLA_ROPE
    wq = jnp.pad(mla_w_q_up[0].reshape(MLA_Q_RANK, MLA_HEADS, qd), ((0, 0), (0, 0), (0, LANES - qd)))
    wkv = mla_w_kv_up[0].reshape(MLA_KV_RANK, MLA_HEADS, MLA_NOPE + MLA_V)
    wkk = jnp.pad(wkv[:, :, :MLA_NOPE], ((0, 0), (0, 0), (0, LANES - MLA_NOPE)))
    padrows = lambda w: jnp.pad(w, ((0, 0), (0, LANES - w.shape[1]), (0, 0)))
    p_ab = {
        'norm_g': row(norm_mix_g[0]), 'win': win_ab,
        'gq': row(mla_q_norm_g[0]), 'wq': wq.reshape(MLA_Q_RANK, MLA_HEADS * LANES).astype(BF16),
        'gkv': row(mla_kv_norm_g[0]), 'wkk': wkk.reshape(MLA_KV_RANK, MLA_HEADS * LANES).astype(BF16),
        'wvv': wkv[:, :, MLA_NOPE:].reshape(MLA_KV_RANK, MLA_HEADS * MLA_V).astype(BF16),
        'mup': row(pad_rw(rwkv_mu_prev[0])), 'mun': row(pad_rw(rwkv_mu_next[0])),
        'g2': rwkv_g2[0].astype(BF16),
        'w0': rwkv_w0[0].reshape(2, 1, RWKV_W), 'w2': padrows(rwkv_w2[0]).astype(BF16),
        'a0': rwkv_a0[0].reshape(2, 1, RWKV_W), 'a2': padrows(rwkv_a2[0]).astype(BF16),
        'kkk': row(rwkv_k_k[0]), 'ka': row(rwkv_k_a[0]), 'rk': row(rwkv_r_k[0]),
        'bd': _block_diag_ones(RWKV_W, RWKV_N),
        'lng': row(rwkv_ln_g[0]), 'lnb': row(rwkv_ln_b[0]), 'wout': ab_w_out[0].astype(BF16),
    }
    mods0 = mods_of(0)
    rope_mla = _rope_tables(l, lc, MLA_ROPE, _mla_place)
    (q, k, v, r, vv, kk, g, bonus, lw, ke, bb) = _ab_in(xall, mods0, p_ab, rope_mla, lc)
    att = _mla_attn(q, k, v, lc)
    y = _rwkv_scan(r, vv, kk, lw, ke, bb, lc)
    x1 = _ab_out(xall, att, y, bonus, g, mods0, p_ab, lc)
    x2 = _ffn(x1, mods0, _ffn_params(0, norm_ffn_g, ffn_w_up, ffn_conv_w, ffn_conv_b, ffn_w_down),
              (0, nbc, nb), seg_of, None)

    cw_in = cd_w_in[0]
    s1 = SSM_INNER
    s2 = s1 + SSM_INNER + 2 * SSM_G * SSM_N
    s3 = s2 + 2 * SSM_HEADS
    win_cd = jnp.concatenate([cw_in[:, :s2], padw(cw_in[:, s2:s3], LANES), cw_in[:, s3:]], axis=1).astype(BF16)
    a_neg = -jnp.exp(ssm_a_log[0])
    heads_to_lanes = np.zeros((LANES, SSM_INNER), np.float32)
    for hd in range(SSM_HEADS):
        heads_to_lanes[hd, hd * SSM_P:(hd + 1) * SSM_P] = 1.0
    p_cd = {
        'norm_g': row(norm_mix_g[1]), 'win': win_cd,
        'cw': ssm_conv_w[0], 'cb': row(ssm_conv_b[0]),
        'dtb': row(jnp.pad(ssm_dt_bias[0].reshape(-1), (0, LANES - 2 * SSM_HEADS))),
        'a_row': jnp.pad(a_neg, ((0, 0), (0, LANES - SSM_HEADS))).reshape(2, 1, LANES),
        'a_col': a_neg.reshape(2, SSM_HEADS, 1),
        'ex': jnp.asarray(heads_to_lanes, BF16),
        'dsk': row(jnp.repeat(ssm_d[0], SSM_P)),
        'ng': row(ssm_norm_g[0]), 'wout': cd_w_out[0].astype(BF16),
    }
    mods1 = mods_of(1)
    rope_swa = _rope_tables(l, lc, HEAD_DIM, _swa_place)
    z, xs, bcm, dt, sq, sk, sv = _cd_in(x2, mods1, p_cd, rope_swa, lc)
    dt2 = jnp.stack([jnp.pad(dt[:, :, dd * SSM_HEADS:(dd + 1) * SSM_HEADS],
                             ((0, 0), (0, 0), (0, LANES - SSM_HEADS))) for dd in range(2)])
    dtt2 = jnp.stack([jnp.swapaxes(dt[:, :, dd * SSM_HEADS:(dd + 1) * SSM_HEADS], 1, 2) for dd in range(2)])
    ys = _ssd(xs, bcm, dt2, dtt2, p_cd, lc)
    satt = _swa(sq, sk, sv, swa_sink[0], lc)
    x3 = _cd_out(x2, ys, z, satt, mods1, p_cd, lc)
    nbl = l // ROW_BLOCK
    return _ffn(x3, mods1, _ffn_params(1, norm_ffn_g, ffn_w_up, ffn_conv_w, ffn_conv_b, ffn_w_down),
                (0, nbl), lambda i: 1, row(final_norm_g))
```

```python
import functools
import math

import jax
import jax.numpy as jnp
import numpy as np
from jax import lax
from jax.experimental import pallas as pl
from jax.experimental.pallas import tpu as pltpu

F32 = jnp.float32
BF16 = jnp.bfloat16

GRID_W = 64
ROPE_THETA = 10000.0
NORM_EPS = 1e-6
HEAD_DIM = 64
MLA_HEADS, MLA_NOPE, MLA_ROPE, MLA_V = 8, 64, 32, 64
MLA_Q_RANK, MLA_KV_RANK = 384, 256
RWKV_HEADS, RWKV_N = 8, 64
RWKV_W = RWKV_HEADS * RWKV_N
DECAY_LORA, ICLR_LORA, GATE_LORA = 64, 64, 128
GN_EPS = 64e-5
SSM_HEADS, SSM_P, SSM_G, SSM_N = 16, 64, 2, 128
SSM_INNER = SSM_HEADS * SSM_P
SSM_CONV = 5
SWA_HEADS, SWA_KV_HEADS, WINDOW = 8, 2, 128
D_FF = 2816
FFN_CONV = 3

LANES = 128
HALO = 16
ROW_BLOCK = 256
RWKV_CHUNK = 64
RWKV_GROUP = 2
SSD_CHUNK = 128
SWA_BLOCK = 128
FFN_CHUNK = 256
VMEM_LIMIT = 56 * 1024 * 1024
NEG_BIG = -1e30


def _cparams(sem):
    return pltpu.CompilerParams(dimension_semantics=sem, vmem_limit_bytes=VMEM_LIMIT)


def _sigmoid(x):
    return 1.0 / (1.0 + jnp.exp(-x))


def _silu(x):
    return x * _sigmoid(x)


def _rms(x, eps=NORM_EPS):
    return x * lax.rsqrt(jnp.mean(x * x, axis=-1, keepdims=True) + eps)


def _bdot(a, b):
    return jnp.dot(a.astype(BF16), b.astype(BF16), preferred_element_type=F32)


def _bdot_nt(a, b):
    return lax.dot_general(a.astype(BF16), b.astype(BF16), (((1,), (1,)), ((), ())),
                           preferred_element_type=F32)


def _bdot_tn(a, b):
    return lax.dot_general(a.astype(BF16), b.astype(BF16), (((0,), (0,)), ((), ())),
                           preferred_element_type=F32)


def _split3(x):
    hi = x.astype(BF16)
    r1 = x - hi.astype(F32)
    mid = r1.astype(BF16)
    lo = (r1 - mid.astype(F32)).astype(BF16)
    return hi, mid, lo


def _dot_exact_rhs(x, m):
    hi, mid, lo = _split3(x)
    return (jnp.dot(hi, m, preferred_element_type=F32) + jnp.dot(mid, m, preferred_element_type=F32)
            + jnp.dot(lo, m, preferred_element_type=F32))


def _dot_exact_lhs(m, x):
    hi, mid, lo = _split3(x)
    return (jnp.dot(m, hi, preferred_element_type=F32) + jnp.dot(m, mid, preferred_element_type=F32)
            + jnp.dot(m, lo, preferred_element_type=F32))


def _dot_exact_rhs_nt(x, m):
    hi, mid, lo = _split3(x)
    dn = (((1,), (1,)), ((), ()))
    return (lax.dot_general(hi, m, dn, preferred_element_type=F32)
            + lax.dot_general(mid, m, dn, preferred_element_type=F32)
            + lax.dot_general(lo, m, dn, preferred_element_type=F32))


def _rope(t, cos, sin_a, sin_b, half):
    w = t.shape[-1]
    return t * cos + pltpu.roll(t, w - half, 1) * sin_a + pltpu.roll(t, half, 1) * sin_b


def _modnorm(x, g, shift, scale):
    return (_rms(x) * g) * (1.0 + scale) + shift


def _seg_flags(i, starts, ends):
    prev_ok = functools.reduce(jnp.logical_and, [i != s for s in starts])
    next_ok = functools.reduce(jnp.logical_and, [i != (e - 1) for e in ends])
    return prev_ok, next_ok


def _h_with_halo(x_ref, xp_ref, xn_ref, g, shift, scale, prev_ok, next_ok):
    h = _modnorm(x_ref[0], g, shift, scale)
    hp = _modnorm(xp_ref[0], g, shift, scale) * prev_ok.astype(F32)
    hn = _modnorm(xn_ref[0], g, shift, scale) * next_ok.astype(F32)
    return jnp.concatenate([hp.astype(BF16), h.astype(BF16), hn.astype(BF16)], axis=0)


def _shift_rows(x, delta, t):
    m = x.shape[0]
    return pltpu.roll(x, (m - delta) % m, 0)[HALO:HALO + t]


def _ada_kernel(c_ref, w_ref, b_ref, o_ref):
    o_ref[0] = _bdot(_silu(c_ref[...]), w_ref[0]) + b_ref[0]


def _ada_mods(cond, ada_w, ada_b):
    depth, d, n = ada_w.shape
    rows = cond.shape[0]
    tn = 1024
    return pl.pallas_call(
        _ada_kernel,
        grid=(depth, n // tn),
        in_specs=[pl.BlockSpec((rows, d), lambda l, j: (0, 0)),
                  pl.BlockSpec((1, d, tn), lambda l, j: (l, 0, j)),
                  pl.BlockSpec((1, 1, tn), lambda l, j: (l, 0, j))],
        out_specs=pl.BlockSpec((1, rows, tn), lambda l, j: (l, 0, j)),
        out_shape=jax.ShapeDtypeStruct((depth, rows, n), F32),
        compiler_params=_cparams(("parallel", "parallel")),
        name="ada_mods",
    )(cond, ada_w, ada_b.reshape(depth, 1, n))


def _row_specs(t, w, nrows8, off=0):
    per = t // HALO
    nh = nrows8
    main = pl.BlockSpec((1, t, w), lambda b, i: (b, i + off, 0))
    prev = pl.BlockSpec((1, HALO, w), lambda b, i: (b, jnp.maximum((i + off) * per - 1, 0), 0))
    nxt = pl.BlockSpec((1, HALO, w), lambda b, i: (b, jnp.minimum((i + off + 1) * per, nh - 1), 0))
    return main, prev, nxt


def _const_spec(shape):
    nd = len(shape)
    return pl.BlockSpec(shape, lambda *_: (0,) * nd)


def _ab_in_kernel(x_ref, xp_ref, xn_ref, mods_ref, g_ref, cos_ref, sa_ref, sb_ref, win_ref,
                  gq_ref, wq_ref, gkv_ref, wkk_ref, wvv_ref, mup_ref, mun_ref, g2_ref,
                  w0_ref, w2_ref, a0_ref, a2_ref, kkk_ref, ka_ref, rk_ref, bd_ref,
                  q_out, k_out, v_out, r_out, vv_out, kk_out, g_out, bonus_out,
                  lw_out, ke_out, b_out, *, starts, ends):
    i = pl.program_id(1)
    t = x_ref.shape[1]
    prev_ok, next_ok = _seg_flags(i, starts, ends)
    m = mods_ref[0, 0]
    h_all = _h_with_halo(x_ref, xp_ref, xn_ref, g_ref[...], m[0:1], m[1:2], prev_ok, next_ok)
    pin = jnp.dot(h_all, win_ref[...], preferred_element_type=F32)

    cos, sa, sb = cos_ref[...], sa_ref[...], sb_ref[...]
    main = pin[HALO:HALO + t]
    cq, ckv, kr = main[:, 0:384], main[:, 384:640], main[:, 640:768]
    qf = _bdot(_rms(cq) * gq_ref[...], wq_ref[...])
    kvn = (_rms(ckv) * gkv_ref[...]).astype(BF16)
    kf = jnp.dot(kvn, wkk_ref[...], preferred_element_type=F32)
    v_out[0] = jnp.dot(kvn, wvv_ref[...], preferred_element_type=F32).astype(BF16)
    krr = _rope(kr, cos, sa, sb, MLA_ROPE // 2)
    for h in range(MLA_HEADS):
        sl = slice(h * LANES, (h + 1) * LANES)
        q_out[0, h] = _rope(qf[:, sl], cos, sa, sb, MLA_ROPE // 2).astype(BF16)
        k_out[0, h] = (kf[:, sl] + krr).astype(BF16)

    rw = pin[:, 768:]
    x = rw[HALO:HALO + t]
    xp = _shift_rows(rw, -1, t)
    xn = _shift_rows(rw, 1, t)
    xs = x + mup_ref[...] * (xp - x) + mun_ref[...] * (xn - x)
    w = RWKV_W
    r, k, v = xs[:, 0:w], xs[:, w:2 * w], xs[:, 2 * w:3 * w]
    xw, xa, xg = xs[:, 3 * w:3 * w + 128], xs[:, 3 * w + 128:3 * w + 256], xs[:, 3 * w + 256:3 * w + 384]
    bd = bd_ref[...]
    g_out[0] = _bdot(_sigmoid(xg), g2_ref[...])
    kkr = k * kkk_ref[...]
    kk = kkr * lax.rsqrt(_dot_exact_rhs(kkr * kkr, bd) + 1e-12)
    r_out[0] = r
    vv_out[0] = v
    kk_out[0] = kk
    bonus_out[0] = _dot_exact_rhs(r * k * rk_ref[...], bd) * v
    tw = jnp.tanh(xw).astype(BF16)
    xab = xa.astype(BF16)
    for d in range(2):
        wl = w0_ref[d] + jnp.dot(tw, w2_ref[d], preferred_element_type=F32)
        lw_out[d, 0] = -math.exp(-0.5) * _sigmoid(wl)
        a = _sigmoid(a0_ref[d] + jnp.dot(xab, a2_ref[d], preferred_element_type=F32))
        ke_out[d, 0] = k * (1.0 + (a - 1.0) * ka_ref[...])
        b_out[d, 0] = kk * a


def _ab_in(x, mods, p, rope, lc):
    b, ltot, d = x.shape
    t = ROW_BLOCK
    nb, nbc = ltot // t, lc // t
    kern = functools.partial(_ab_in_kernel, starts=(0, nbc), ends=(nbc, nb))
    main, prev, nxt = _row_specs(t, d, ltot // HALO)
    tab = pl.BlockSpec((t, LANES), lambda bb, i: (i, 0))
    consts = [p['win'], p['gq'], p['wq'], p['gkv'], p['wkk'], p['wvv'], p['mup'], p['mun'], p['g2'],
              p['w0'], p['w2'], p['a0'], p['a2'], p['kkk'], p['ka'], p['rk'], p['bd']]
    in_specs = ([main, prev, nxt,
                 pl.BlockSpec((1, 1, 6, d), lambda bb, i: (bb, jnp.where(i >= nbc, 1, 0), 0, 0)),
                 _const_spec((1, d)), tab, tab, tab] + [_const_spec(c.shape) for c in consts])
    hq = pl.BlockSpec((1, MLA_HEADS, t, LANES), lambda bb, i: (bb, 0, i, 0))
    row = pl.BlockSpec((1, t, RWKV_W), lambda bb, i: (bb, i, 0))
    drow = pl.BlockSpec((2, 1, t, RWKV_W), lambda bb, i: (0, bb, i, 0))
    f_rows = jax.ShapeDtypeStruct((b, ltot, RWKV_W), F32)
    f_drows = jax.ShapeDtypeStruct((2, b, ltot, RWKV_W), F32)
    return pl.pallas_call(
        kern, grid=(b, nb), in_specs=in_specs,
        out_specs=[hq, hq, row, row, row, row, row, row, drow, drow, drow],
        out_shape=[jax.ShapeDtypeStruct((b, MLA_HEADS, ltot, LANES), BF16),
                   jax.ShapeDtypeStruct((b, MLA_HEADS, ltot, LANES), BF16),
                   jax.ShapeDtypeStruct((b, ltot, MLA_HEADS * MLA_V), BF16),
                   f_rows, f_rows, f_rows, f_rows, f_rows, f_drows, f_drows, f_drows],
        compiler_params=_cparams(("parallel", "parallel")),
        name="ab_in",
    )(x, x, x, mods, p['norm_g'], rope[0], rope[1], rope[2], *consts)


def _mla_attn_kernel(q_ref, k_ref, v_ref, o_ref, *, nk_ctx, scale):
    i = pl.program_id(1)
    nk_all = k_ref.shape[2]

    def attend(nk):
        for hp in range(MLA_HEADS // 2):
            outs = []
            for h in (2 * hp, 2 * hp + 1):
                s = lax.dot_general(q_ref[0, h], k_ref[0, h, 0:nk], (((1,), (1,)), ((), ())),
                                    preferred_element_type=F32) * scale
                mx = jnp.max(s, axis=-1, keepdims=True)
                e = jnp.exp(s - mx)
                den = jnp.sum(e, axis=-1, keepdims=True)
                o2 = jnp.dot(e.astype(BF16), v_ref[0, 0:nk, hp * LANES:(hp + 1) * LANES],
                             preferred_element_type=F32)
                outs.append(o2 / den)
            lane = lax.broadcasted_iota(jnp.int32, outs[0].shape, 1)
            o_ref[0, :, hp * LANES:(hp + 1) * LANES] = jnp.where(lane < MLA_V, outs[0], outs[1])

    pl.when(i == 0)(lambda: attend(nk_ctx))
    pl.when(i > 0)(lambda: attend(nk_all))


def _mla_attn(q, k, v, lc):
    b, hh, ltot, _ = q.shape
    t = lc
    kern = functools.partial(_mla_attn_kernel, nk_ctx=lc, scale=(MLA_NOPE + MLA_ROPE) ** -0.5)
    return pl.pallas_call(
        kern, grid=(b, ltot // t),
        in_specs=[pl.BlockSpec((1, hh, t, LANES), lambda bb, i: (bb, 0, i, 0)),
                  pl.BlockSpec((1, hh, ltot, LANES), lambda bb, i: (bb, 0, 0, 0)),
                  pl.BlockSpec((1, ltot, hh * MLA_V), lambda bb, i: (bb, 0, 0))],
        out_specs=pl.BlockSpec((1, t, hh * MLA_V), lambda bb, i: (bb, i, 0)),
        out_shape=jax.ShapeDtypeStruct((b, ltot, hh * MLA_V), F32),
        compiler_params=_cparams(("parallel", "arbitrary")),
        name="mla_attn",
    )(q, k, v)


def _rwkv_scan_kernel(r_ref, v_ref, kk_ref, lw_ref, ke_ref, b_ref, y_ref, s_ref, *, reverse, nchunks):
    n = pl.program_id(1)
    c = RWKV_CHUNK
    rows = nchunks * c

    @pl.when(n == 0)
    def _():
        s_ref[...] = jnp.zeros_like(s_ref)

    sgn = -1 if reverse else 1
    rowb = lax.broadcasted_iota(jnp.int32, (rows, rows), 0)
    colb = lax.broadcasted_iota(jnp.int32, (rows, rows), 1)
    same = (rowb // c) == (colb // c)
    inclb = jnp.logical_and(same, sgn * (rowb - colb) >= 0)
    row = lax.broadcasted_iota(jnp.int32, (c, c), 0)
    col = lax.broadcasted_iota(jnp.int32, (c, c), 1)
    incl = sgn * (row - col) >= 0
    strict = sgn * (row - col) > 0
    eye = (row == col).astype(F32)

    lw = lw_ref[0, 0]
    cum = _dot_exact_lhs(inclb.astype(BF16), lw)
    tot = _dot_exact_lhs(same.astype(BF16), lw)
    r, v, kk, ke, bb = r_ref[0], v_ref[0], kk_ref[0], ke_ref[0, 0], b_ref[0, 0]
    e_nc = jnp.exp(-cum)
    e_end = jnp.exp(tot - cum)
    rt = r * jnp.exp(cum)
    kt = kk * jnp.exp(cum - lw)
    bh, kh = bb * e_nc, ke * e_nc
    bc, kc = bb * e_end, ke * e_end
    gam = jnp.exp(tot)

    chains = [(g, h) for g in range(nchunks) for h in range(RWKV_HEADS)]
    blk = lambda x, g, h: x[g * c:(g + 1) * c, h * RWKV_N:(h + 1) * RWKV_N]
    ms, ns, pbs, pks = [], [], [], []
    for g, h in chains:
        a = jnp.concatenate([blk(kt, g, h), blk(rt, g, h)], axis=0)
        bm = jnp.concatenate([blk(bh, g, h), blk(kh, g, h)], axis=0)
        gm = _bdot_nt(a, bm)
        ms.append(jnp.where(strict, gm[0:c, 0:c], 0.0))
        ns.append(jnp.where(strict, gm[0:c, c:2 * c], 0.0))
        pbs.append(jnp.where(incl, gm[c:2 * c, 0:c], 0.0))
        pks.append(jnp.where(incl, gm[c:2 * c, c:2 * c], 0.0))
    xinv = [eye - m for m in ms]
    pw = [_bdot(m, m) for m in ms]
    steps = int(math.log2(c)) - 1
    for it in range(steps):
        xinv = [x + _bdot(x, p) for x, p in zip(xinv, pw)]
        if it < steps - 1:
            pw = [_bdot(p, p) for p in pw]
    npk = [_bdot(jnp.concatenate([nn, pk], axis=0), blk(v, g, h)) for (g, h), nn, pk in zip(chains, ns, pks)]
    kuu0 = [-_bdot(x, jnp.concatenate([blk(kt, g, h), q[0:c]], axis=1))
            for (g, h), x, q in zip(chains, xinv, npk)]
    pbk = [_bdot(pb, ku) for pb, ku in zip(pbs, kuu0)]
    ry = [(blk(rt, g, h) + q[:, 0:RWKV_N]).astype(BF16) for (g, h), q in zip(chains, pbk)]
    y0 = [q[:, RWKV_N:] + w[c:2 * c] for q, w in zip(pbk, npk)]
    tms = []
    for (g, h), ku in zip(chains, kuu0):
        lhs = jnp.concatenate([ku, jnp.concatenate([jnp.zeros((c, RWKV_N), F32), blk(v, g, h)], axis=1)],
                              axis=0)
        rhs = jnp.concatenate([blk(bc, g, h), blk(kc, g, h)], axis=0)
        tms.append(_bdot_tn(lhs, rhs))

    state = [s_ref[h] for h in range(RWKV_HEADS)]
    youts = [None] * nchunks
    for g in (range(nchunks - 1, -1, -1) if reverse else range(nchunks)):
        ys = []
        for h in range(RWKV_HEADS):
            k = g * RWKV_HEADS + h
            s0 = state[h]
            sb = s0.astype(BF16)
            ys.append(y0[k] + lax.dot_general(ry[k], sb, (((1,), (1,)), ((), ())), preferred_element_type=F32))
            state[h] = (s0 * gam[g * c:g * c + 1, h * RWKV_N:(h + 1) * RWKV_N]
                        + jnp.dot(sb, tms[k][0:RWKV_N].astype(BF16), preferred_element_type=F32)
                        + tms[k][RWKV_N:])
        youts[g] = jnp.concatenate(ys, axis=1)
    y_ref[0] = jnp.concatenate(youts, axis=0)
    for h in range(RWKV_HEADS):
        s_ref[h] = state[h]


def _scan_chunk_index(n, nc_ctx, nc, d):
    bwd = jnp.where(n < nc_ctx, nc_ctx - 1 - n, nc - 1 - (n - nc_ctx))
    return jnp.where(d == 0, n, bwd)


def _rwkv_scan(r, v, kk, lw, ke, bb, lc, reverse):
    b, ltot, w = r.shape
    nch = RWKV_GROUP
    rows = nch * RWKV_CHUNK
    nc, ncc = ltot // rows, lc // rows
    d = 1 if reverse else 0
    cidx = lambda n: _scan_chunk_index(n, ncc, nc, d)
    shared = pl.BlockSpec((1, rows, w), lambda bi, n: (bi, cidx(n), 0))
    perdir = pl.BlockSpec((1, 1, rows, w), lambda bi, n: (d, bi, cidx(n), 0))
    kern = functools.partial(_rwkv_scan_kernel, reverse=reverse, nchunks=nch)
    return pl.pallas_call(
        kern, grid=(b, nc),
        in_specs=[shared, shared, shared, perdir, perdir, perdir],
        out_specs=shared,
        out_shape=jax.ShapeDtypeStruct((b, ltot, w), F32),
        scratch_shapes=[pltpu.VMEM((RWKV_HEADS, RWKV_N, RWKV_N), F32)],
        compiler_params=_cparams(("parallel", "arbitrary")),
        name="rwkv_scan_bwd" if reverse else "rwkv_scan_fwd",
    )(r, v, kk, lw, ke, bb)


def _ab_out_kernel(x_ref, att_ref, yf_ref, yb_ref, bonus_ref, g_ref, mods_ref, lng_ref, lnb_ref, bd_ref,
                   wout_ref, o_ref):
    y = yf_ref[0] + yb_ref[0]
    bd = bd_ref[...]
    mu = _dot_exact_rhs(y, bd) * (1.0 / RWKV_N)
    dl = y - mu
    var = _dot_exact_rhs(dl * dl, bd) * (1.0 / RWKV_N)
    yn = dl * lax.rsqrt(var + GN_EPS) * lng_ref[...] + lnb_ref[...]
    rwo = (yn + bonus_ref[0]) * g_ref[0]
    feat = jnp.concatenate([att_ref[0].astype(BF16), rwo.astype(BF16)], axis=1)
    o = jnp.dot(feat, wout_ref[...], preferred_element_type=F32)
    o_ref[0] = x_ref[0] + mods_ref[0, 0][2:3] * o


def _ab_out(x, att, yf, yb, bonus, g, mods, p, lc):
    b, ltot, d = x.shape
    t = ROW_BLOCK
    nbc = lc // t
    row = lambda w: pl.BlockSpec((1, t, w), lambda bb, i: (bb, i, 0))
    consts = [p['lng'], p['lnb'], p['bd'], p['wout']]
    return pl.pallas_call(
        _ab_out_kernel, grid=(b, ltot // t),
        in_specs=[row(d), row(RWKV_W), row(RWKV_W), row(RWKV_W), row(RWKV_W), row(RWKV_W),
                  pl.BlockSpec((1, 1, 6, d), lambda bb, i: (bb, jnp.where(i >= nbc, 1, 0), 0, 0))]
        + [_const_spec(c.shape) for c in consts],
        out_specs=row(d),
        out_shape=jax.ShapeDtypeStruct((b, ltot, d), F32),
        compiler_params=_cparams(("parallel", "parallel")),
        name="ab_out",
    )(x, att, yf, yb, bonus, g, mods, *consts)


def _ffn_kernel(x_ref, xp_ref, xn_ref, mods_ref, g_ref, wg_ref, wv_ref, cwg_ref, cwv_ref,
                cbg_ref, cbv_ref, wd_ref, fg_ref, o_ref, acc_ref, *, starts, ends, final_norm):
    i = pl.program_id(1)
    t = x_ref.shape[1]
    prev_ok, next_ok = _seg_flags(i, starts, ends)
    m = mods_ref[0, 0]
    h_all = _h_with_halo(x_ref, xp_ref, xn_ref, g_ref[...], m[3:4], m[4:5], prev_ok, next_ok)
    acc_ref[...] = jnp.zeros_like(acc_ref)

    def conv(u, cw, cb):
        return (cw[0:1] * _shift_rows(u, -1, t) + cw[1:2] * u[HALO:HALO + t]
                + cw[2:3] * _shift_rows(u, 1, t) + cb)

    def step(c, carry):
        ug = jnp.dot(h_all, wg_ref[c], preferred_element_type=F32)
        uv = jnp.dot(h_all, wv_ref[c], preferred_element_type=F32)
        act = _silu(conv(ug, cwg_ref[c], cbg_ref[c])) * conv(uv, cwv_ref[c], cbv_ref[c])
        acc_ref[...] += jnp.dot(act.astype(BF16), wd_ref[c], preferred_element_type=F32)
        return carry

    lax.fori_loop(0, wg_ref.shape[0], step, 0)
    out = x_ref[0] + m[5:6] * acc_ref[...]
    if final_norm:
        out = _rms(out) * fg_ref[...]
    o_ref[0] = out


def _ffn(x, mods, p, seg_bounds, seg_index, final_g):
    b, rows, d = x.shape
    t = ROW_BLOCK
    starts, ends = tuple(seg_bounds[:-1]), tuple(seg_bounds[1:])
    kern = functools.partial(_ffn_kernel, starts=starts, ends=ends, final_norm=final_g is not None)
    main, prev, nxt = _row_specs(t, d, rows // HALO)
    fg = final_g if final_g is not None else jnp.ones((1, d), F32)
    consts = [p['wg'], p['wv'], p['cwg'], p['cwv'], p['cbg'], p['cbv'], p['wd'], fg]
    return pl.pallas_call(
        kern, grid=(b, rows // t),
        in_specs=[main, prev, nxt,
                  pl.BlockSpec((1, 1, 6, d), lambda bb, i: (bb, seg_index(i), 0, 0)),
                  _const_spec((1, d))] + [_const_spec(c.shape) for c in consts],
        out_specs=pl.BlockSpec((1, t, d), lambda bb, i: (bb, i, 0)),
        out_shape=jax.ShapeDtypeStruct((b, rows, d), F32),
        scratch_shapes=[pltpu.VMEM((t, d), F32)],
        compiler_params=_cparams(("parallel", "parallel")),
        name="conv_ffn",
    )(x, x, x, mods, p['norm_g'], *consts)


def _cd_in_kernel(x_ref, xp_ref, xn_ref, mods_ref, g_ref, cos_ref, sa_ref, sb_ref, win_ref,
                  cw_ref, cb_ref, dtb_ref,
                  z_out, xs_out, bc_out, dt_out, q_out, k_out, v_out, *, starts, ends):
    i = pl.program_id(1)
    t = x_ref.shape[1]
    prev_ok, next_ok = _seg_flags(i, starts, ends)
    m = mods_ref[0, 0]
    h_all = _h_with_halo(x_ref, xp_ref, xn_ref, g_ref[...], m[0:1], m[1:2], prev_ok, next_ok)
    pin = jnp.dot(h_all, win_ref[...], preferred_element_type=F32)
    main = pin[HALO:HALO + t]
    z_out[0] = main[:, 0:1024]
    xbc = pin[:, 1024:2560]
    cw = cw_ref[...]
    conv = cb_ref[...] + cw[2:3] * xbc[HALO:HALO + t]
    for j in (0, 1, 3, 4):
        conv = conv + cw[j:j + 1] * _shift_rows(xbc, j - SSM_CONV // 2, t)
    xc = _silu(conv)
    xs_out[0] = xc[:, 0:SSM_INNER]
    bc_out[0] = xc[:, SSM_INNER:]
    raw = main[:, 2560:2688] + dtb_ref[...]
    dt_out[0] = jnp.maximum(raw, 0.0) + jnp.log(1.0 + jnp.exp(-jnp.abs(raw)))
    cos, sa, sb = cos_ref[...], sa_ref[...], sb_ref[...]
    q = main[:, 2688:3200]
    tile = lambda a: jnp.concatenate([a] * (q.shape[1] // LANES), axis=1)
    q_out[0] = _rope(q, tile(cos), tile(sa), tile(sb), HEAD_DIM // 2).astype(BF16)
    k_out[0] = _rope(main[:, 3200:3328], cos, sa, sb, HEAD_DIM // 2).astype(BF16)
    v_out[0] = main[:, 3328:3456].astype(BF16)


def _cd_in(x, mods, p, rope, lc):
    b, ltot, d = x.shape
    t = ROW_BLOCK
    nb, nbc = ltot // t, lc // t
    kern = functools.partial(_cd_in_kernel, starts=(0, nbc), ends=(nbc, nb))
    main, prev, nxt = _row_specs(t, d, ltot // HALO)
    tab = pl.BlockSpec((t, LANES), lambda bb, i: (i, 0))
    consts = [p['win'], p['cw'], p['cb'], p['dtb']]
    row = lambda w: pl.BlockSpec((1, t, w), lambda bb, i: (bb, i, 0))
    sds = lambda w, dt: jax.ShapeDtypeStruct((b, ltot, w), dt)
    kvw = SWA_KV_HEADS * HEAD_DIM
    return pl.pallas_call(
        kern, grid=(b, nb),
        in_specs=[main, prev, nxt,
                  pl.BlockSpec((1, 1, 6, d), lambda bb, i: (bb, jnp.where(i >= nbc, 1, 0), 0, 0)),
                  _const_spec((1, d)), tab, tab, tab] + [_const_spec(c.shape) for c in consts],
        out_specs=[row(SSM_INNER), row(SSM_INNER), row(2 * SSM_G * SSM_N), row(LANES),
                   row(SWA_HEADS * HEAD_DIM), row(kvw), row(kvw)],
        out_shape=[sds(SSM_INNER, F32), sds(SSM_INNER, F32), sds(2 * SSM_G * SSM_N, F32), sds(LANES, F32),
                   sds(SWA_HEADS * HEAD_DIM, BF16), sds(kvw, BF16), sds(kvw, BF16)],
        compiler_params=_cparams(("parallel", "parallel")),
        name="cd_in",
    )(x, x, x, mods, p['norm_g'], rope[0], rope[1], rope[2], *consts)


def _ssd_kernel(xs_ref, bc_ref, dt_ref, dtt_ref, a_ref, at_ref, ex_ref, dsk_ref, y_ref, h_ref):
    d = pl.program_id(0)
    n = pl.program_id(2)
    q = SSD_CHUNK
    hg = SSM_HEADS // SSM_G
    gw = hg * SSM_P

    @pl.when(n == 0)
    def _():
        h_ref[...] = jnp.zeros_like(h_ref)

    row = lax.broadcasted_iota(jnp.int32, (q, q), 0)
    col = lax.broadcasted_iota(jnp.int32, (q, q), 1)
    fwd = d == 0
    incl = jnp.where(fwd, row - col, col - row) >= 0
    inclb = incl.astype(BF16)

    dt_c = dt_ref[0, 0]
    dt_r = dtt_ref[0, 0]
    acum_c = _dot_exact_lhs(inclb, dt_c * a_ref[0])
    acum_r = _dot_exact_rhs_nt(dt_r * at_ref[0], inclb)
    ex = ex_ref[...]
    acum_x = _dot_exact_rhs(acum_c, ex)
    dt_x = _dot_exact_rhs(dt_c, ex)
    end_x = jnp.where(fwd, acum_x[q - 1:q], acum_x[0:1])
    grow = jnp.exp(acum_x)
    tail = jnp.exp(end_x - acum_x) * dt_x
    chunk_decay = jnp.exp(end_x)
    skip = dsk_ref[...] * (d == 0).astype(F32)

    xs = xs_ref[0]
    bcm = bc_ref[0]
    for g in range(SSM_G):
        bg = bcm[:, g * SSM_N:(g + 1) * SSM_N].astype(BF16)
        cg = bcm[:, (SSM_G + g) * SSM_N:(SSM_G + g + 1) * SSM_N].astype(BF16)
        cb = lax.dot_general(cg, bg, (((1,), (1,)), ((), ())), preferred_element_type=F32)
        gs = slice(g * gw, (g + 1) * gw)
        xg = xs[:, gs]
        yd = []
        for hh in range(hg):
            hd = g * hg + hh
            seg = jnp.where(incl, acum_c[:, hd:hd + 1] - acum_r[hd:hd + 1, :], NEG_BIG)
            wgt = cb * jnp.exp(seg) * dt_r[hd:hd + 1, :]
            yd.append(_bdot(wgt, xg[:, hh * SSM_P:(hh + 1) * SSM_P]))
        h_in = h_ref[g]
        y_off = jnp.dot(cg, h_in.astype(BF16), preferred_element_type=F32) * grow[:, gs]
        y_ref[0, 0, :, gs] = jnp.concatenate(yd, axis=1) + y_off + skip[:, gs] * xg
        h_ref[g] = h_in * chunk_decay[:, gs] + _bdot_tn(bg, xg * tail[:, gs])


def _ssd(xs, bcm, dt2, dtt2, p, lc):
    b, ltot, inner = xs.shape
    q = SSD_CHUNK
    nc, ncc = ltot // q, lc // q
    cidx = lambda d, bi, n: _scan_chunk_index(n, ncc, nc, d)
    hg = SSM_HEADS // SSM_G
    return pl.pallas_call(
        _ssd_kernel, grid=(2, b, nc),
        in_specs=[pl.BlockSpec((1, q, inner), lambda d, bi, n: (bi, cidx(d, bi, n), 0)),
                  pl.BlockSpec((1, q, 2 * SSM_G * SSM_N), lambda d, bi, n: (bi, cidx(d, bi, n), 0)),
                  pl.BlockSpec((1, 1, q, LANES), lambda d, bi, n: (d, bi, cidx(d, bi, n), 0)),
                  pl.BlockSpec((1, 1, SSM_HEADS, q), lambda d, bi, n: (d, bi, 0, cidx(d, bi, n))),
                  pl.BlockSpec((1, 1, LANES), lambda d, bi, n: (d, 0, 0)),
                  pl.BlockSpec((1, SSM_HEADS, 1), lambda d, bi, n: (d, 0, 0)),
                  _const_spec(p['ex'].shape), _const_spec(p['dsk'].shape)],
        out_specs=pl.BlockSpec((1, 1, q, inner), lambda d, bi, n: (d, bi, cidx(d, bi, n), 0)),
        out_shape=jax.ShapeDtypeStruct((2, b, ltot, inner), F32),
        scratch_shapes=[pltpu.VMEM((SSM_G, SSM_N, hg * SSM_P), F32)],
        compiler_params=_cparams(("parallel", "parallel", "arbitrary")),
        name="ssd_scan",
    )(xs, bcm, dt2, dtt2, p['a_row'], p['a_col'], p['ex'], p['dsk'])


def _swa_kernel(sink_ref, q_ref, kc_ref, kp_ref, k0_ref, kn_ref, vc_ref, vp_ref, v0_ref, vn_ref, o_ref,
                *, nblocks, scale):
    i = pl.program_id(1)
    t = q_ref.shape[1]
    nctx = kc_ref.shape[1]
    hg = SWA_HEADS // SWA_KV_HEADS
    kall = jnp.concatenate([kc_ref[0], kp_ref[0], k0_ref[0], kn_ref[0]], axis=0)
    vall = jnp.concatenate([vc_ref[0], vp_ref[0], v0_ref[0], vn_ref[0]], axis=0)
    nk = nctx + 3 * t
    r = lax.broadcasted_iota(jnp.int32, (hg * t, nk), 0) % t
    c = lax.broadcasted_iota(jnp.int32, (hg * t, nk), 1) - nctx
    lo = jnp.where(i > 0, r, t)
    hi = jnp.where(i < nblocks - 1, r + 2 * t, 2 * t - 1)
    okg = jnp.logical_or(c < 0, jnp.logical_and(c >= lo, c <= hi))
    rowh = lax.broadcasted_iota(jnp.int32, (hg * t, 1), 0) // t
    qb = q_ref[0]
    for g in range(SWA_KV_HEADS):
        qg = jnp.concatenate([qb[:, (g * hg + j) * HEAD_DIM:(g * hg + j + 1) * HEAD_DIM] for j in range(hg)],
                             axis=0)
        kg = kall[:, g * HEAD_DIM:(g + 1) * HEAD_DIM]
        vg = vall[:, g * HEAD_DIM:(g + 1) * HEAD_DIM]
        s = lax.dot_general(qg, kg, (((1,), (1,)), ((), ())), preferred_element_type=F32) * scale
        s = jnp.where(okg, s, NEG_BIG)
        sink = jnp.zeros((hg * t, 1), F32)
        for j in range(hg):
            sink = jnp.where(rowh == j, sink_ref[g * hg + j], sink)
        mx = jnp.maximum(jnp.max(s, axis=-1, keepdims=True), sink)
        e = jnp.exp(s - mx)
        den = jnp.sum(e, axis=-1, keepdims=True) + jnp.exp(sink - mx)
        og = jnp.dot(e.astype(BF16), vg, preferred_element_type=F32) / den
        for j in range(hg):
            hd = g * hg + j
            o_ref[0, :, hd * HEAD_DIM:(hd + 1) * HEAD_DIM] = og[j * t:(j + 1) * t]


def _swa(q, k, v, sink, lc):
    b, ltot, qw = q.shape
    t = SWA_BLOCK
    nb = (ltot - lc) // t
    off = lc // t
    kvw = k.shape[-1]
    kern = functools.partial(_swa_kernel, nblocks=nb, scale=HEAD_DIM ** -0.5)
    ctx = pl.BlockSpec((1, lc, kvw), lambda bb, i: (bb, 0, 0))
    prev = pl.BlockSpec((1, t, kvw), lambda bb, i: (bb, off + jnp.maximum(i - 1, 0), 0))
    cur = pl.BlockSpec((1, t, kvw), lambda bb, i: (bb, off + i, 0))
    nxt = pl.BlockSpec((1, t, kvw), lambda bb, i: (bb, off + jnp.minimum(i + 1, nb - 1), 0))
    return pl.pallas_call(
        kern, grid=(b, nb),
        in_specs=[pl.BlockSpec(memory_space=pltpu.SMEM),
                  pl.BlockSpec((1, t, qw), lambda bb, i: (bb, off + i, 0)),
                  ctx, prev, cur, nxt, ctx, prev, cur, nxt],
        out_specs=pl.BlockSpec((1, t, qw), lambda bb, i: (bb, i, 0)),
        out_shape=jax.ShapeDtypeStruct((b, ltot - lc, qw), F32),
        compiler_params=_cparams(("parallel", "parallel")),
        name="swa_attn",
    )(sink, q, k, k, k, k, v, v, v, v)


def _cd_out_kernel(x_ref, y_ref, z_ref, att_ref, mods_ref, ng_ref, wout_ref, o_ref):
    u = (y_ref[0, 0] + y_ref[1, 0]) * _silu(z_ref[0])
    gw = SSM_INNER // SSM_G
    un = jnp.concatenate([_rms(u[:, g * gw:(g + 1) * gw]) for g in range(SSM_G)], axis=1) * ng_ref[...]
    feat = jnp.concatenate([un.astype(BF16), att_ref[0].astype(BF16)], axis=1)
    o = jnp.dot(feat, wout_ref[...], preferred_element_type=F32)
    o_ref[0] = x_ref[0] + mods_ref[0, 0][2:3] * o


def _cd_out(x, y, z, att, mods, p, lc):
    b, ltot, d = x.shape
    t = ROW_BLOCK
    off = lc // t
    nbl = (ltot - lc) // t
    full = lambda w: pl.BlockSpec((1, t, w), lambda bb, i: (bb, i + off, 0))
    return pl.pallas_call(
        _cd_out_kernel, grid=(b, nbl),
        in_specs=[full(d), pl.BlockSpec((2, 1, t, SSM_INNER), lambda bb, i: (0, bb, i + off, 0)),
                  full(SSM_INNER), pl.BlockSpec((1, t, att.shape[-1]), lambda bb, i: (bb, i, 0)),
                  pl.BlockSpec((1, 1, 6, d), lambda bb, i: (bb, 1, 0, 0)),
                  _const_spec(p['ng'].shape), _const_spec(p['wout'].shape)],
        out_specs=pl.BlockSpec((1, t, d), lambda bb, i: (bb, i, 0)),
        out_shape=jax.ShapeDtypeStruct((b, ltot - lc, d), F32),
        compiler_params=_cparams(("parallel", "parallel")),
        name="cd_out",
    )(x, y, z, att, mods, p['ng'], p['wout'])


def _pad_cols(w, width):
    return jnp.pad(w, ((0, 0), (0, width - w.shape[1])))


def _rope_tables(n_tok, lc, rot_dim, place):
    rows = n_tok // GRID_W
    rp, cp = jnp.meshgrid(jnp.arange(rows, dtype=F32), jnp.arange(GRID_W, dtype=F32), indexing='ij')
    n_freq = rot_dim // 4
    inv_freq = ROPE_THETA ** (-jnp.arange(n_freq, dtype=F32) / n_freq)
    ang = jnp.concatenate([rp.reshape(-1, 1) * inv_freq, cp.reshape(-1, 1) * inv_freq], axis=-1)
    cos = jnp.concatenate([jnp.ones((lc, rot_dim // 2), F32), jnp.cos(ang)], axis=0)
    sin = jnp.concatenate([jnp.zeros((lc, rot_dim // 2), F32), jnp.sin(ang)], axis=0)
    zero = jnp.zeros_like(sin)
    return place(cos, cos, 1.0), place(-sin, zero, 0.0), place(zero, sin, 0.0)


def _mla_place(first, second, fill):
    n = first.shape[0]
    return jnp.concatenate([jnp.full((n, MLA_NOPE), fill, F32), first, second,
                            jnp.zeros((n, LANES - MLA_NOPE - MLA_ROPE), F32)], axis=1)


def _swa_place(first, second, fill):
    return jnp.concatenate([first, second] * (LANES // HEAD_DIM), axis=1)


def _block_diag_ones(n, group):
    idx = np.arange(n) // group
    return jnp.asarray(idx[:, None] == idx[None, :], BF16)


def _ffn_params(i, norm_ffn_g, ffn_w_up, ffn_conv_w, ffn_conv_b, ffn_w_down):
    d = ffn_w_up.shape[1]
    nch = D_FF // FFN_CHUNK
    chunks = lambda w: w.reshape(w.shape[0], nch, FFN_CHUNK).transpose(1, 0, 2)
    wup = ffn_w_up[i].astype(BF16)
    return {
        'norm_g': norm_ffn_g[i].reshape(1, d),
        'wg': chunks(wup[:, :D_FF]), 'wv': chunks(wup[:, D_FF:]),
        'cwg': chunks(ffn_conv_w[i][:, :D_FF]), 'cwv': chunks(ffn_conv_w[i][:, D_FF:]),
        'cbg': chunks(ffn_conv_b[i][None, :D_FF]), 'cbv': chunks(ffn_conv_b[i][None, D_FF:]),
        'wd': ffn_w_down[i].astype(BF16).reshape(nch, FFN_CHUNK, d),
    }


def kernel(x, c, ctx, c_ctx, ada_w, ada_b, norm_mix_g, norm_ffn_g, ffn_w_up, ffn_conv_w, ffn_conv_b, ffn_w_down, final_norm_g, ab_w_in, ab_w_out, mla_q_norm_g, mla_w_q_up, mla_kv_norm_g, mla_w_kv_up, rwkv_mu_prev, rwkv_mu_next, rwkv_w0, rwkv_w2, rwkv_a0, rwkv_a2, rwkv_g2, rwkv_k_k, rwkv_k_a, rwkv_r_k, rwkv_ln_g, rwkv_ln_b, cd_w_in, cd_w_out, ssm_conv_w, ssm_conv_b, ssm_dt_bias, ssm_a_log, ssm_d, ssm_norm_g, swa_sink):
    b, l, d = x.shape
    lc = ctx.shape[1]
    assert ada_w.shape[0] == 2 and lc % ROW_BLOCK == 0 and l % ROW_BLOCK == 0
    ltot = lc + l
    xall = jnp.concatenate([ctx, x], axis=1)

    nrow = -(-(b + 1) // 8) * 8
    cond = jnp.zeros((nrow, d), F32).at[:b].set(c).at[b].set(c_ctx)
    ada = _ada_mods(cond, ada_w, ada_b)
    def mods_of(i):
        lat = ada[i, :b].reshape(b, 1, 6, d)
        cx = jnp.broadcast_to(ada[i, b].reshape(1, 1, 6, d), (b, 1, 6, d))
        return jnp.concatenate([cx, lat], axis=1)

    row = lambda v: v.reshape(1, -1)
    nbc, nb = lc // ROW_BLOCK, ltot // ROW_BLOCK
    seg_of = lambda i: jnp.where(i >= nbc, 1, 0)

    w_in = ab_w_in[0]
    padw = lambda w, n: jnp.pad(w, ((0, 0), (0, n - w.shape[1])))
    o1, o2, o3 = MLA_Q_RANK, MLA_Q_RANK + MLA_KV_RANK, MLA_Q_RANK + MLA_KV_RANK + MLA_ROPE
    w_kr = jnp.pad(w_in[:, o2:o3], ((0, 0), (MLA_NOPE, LANES - MLA_NOPE - MLA_ROPE)))
    rw = w_in[:, o3:]
    rw_sizes = (3 * RWKV_W, DECAY_LORA, ICLR_LORA, GATE_LORA)
    def pad_rw(v):
        a, bq, cq_, dq = jnp.split(v, np.cumsum(rw_sizes)[:-1].tolist(), axis=-1)
        z = jnp.zeros(v.shape[:-1] + (LANES - DECAY_LORA,), v.dtype)
        return jnp.concatenate([a, bq, z, cq_, z, dq], axis=-1)
    win_ab = jnp.concatenate([w_in[:, :o2], w_kr, pad_rw(rw)], axis=1).astype(BF16)
    qd = MLA_NOPE + MLA_ROPE
    wq = jnp.pad(mla_w_q_up[0].reshape(MLA_Q_RANK, MLA_HEADS, qd), ((0, 0), (0, 0), (0, LANES - qd)))
    wkv = mla_w_kv_up[0].reshape(MLA_KV_RANK, MLA_HEADS, MLA_NOPE + MLA_V)
    wkk = jnp.pad(wkv[:, :, :MLA_NOPE], ((0, 0), (0, 0), (0, LANES - MLA_NOPE)))
    padrows = lambda w: jnp.pad(w, ((0, 0), (0, LANES - w.shape[1]), (0, 0)))
    p_ab = {
        'norm_g': row(norm_mix_g[0]), 'win': win_ab,
        'gq': row(mla_q_norm_g[0]), 'wq': wq.reshape(MLA_Q_RANK, MLA_HEADS * LANES).astype(BF16),
        'gkv': row(mla_kv_norm_g[0]), 'wkk': wkk.reshape(MLA_KV_RANK, MLA_HEADS * LANES).astype(BF16),
        'wvv': wkv[:, :, MLA_NOPE:].reshape(MLA_KV_RANK, MLA_HEADS * MLA_V).astype(BF16),
        'mup': row(pad_rw(rwkv_mu_prev[0])), 'mun': row(pad_rw(rwkv_mu_next[0])),
        'g2': rwkv_g2[0].astype(BF16),
        'w0': rwkv_w0[0].reshape(2, 1, RWKV_W), 'w2': padrows(rwkv_w2[0]).astype(BF16),
        'a0': rwkv_a0[0].reshape(2, 1, RWKV_W), 'a2': padrows(rwkv_a2[0]).astype(BF16),
        'kkk': row(rwkv_k_k[0]), 'ka': row(rwkv_k_a[0]), 'rk': row(rwkv_r_k[0]),
        'bd': _block_diag_ones(RWKV_W, RWKV_N),
        'lng': row(rwkv_ln_g[0]), 'lnb': row(rwkv_ln_b[0]), 'wout': ab_w_out[0].astype(BF16),
    }
    mods0 = mods_of(0)
    rope_mla = _rope_tables(l, lc, MLA_ROPE, _mla_place)
    (q, k, v, r, vv, kk, g, bonus, lw, ke, bb) = _ab_in(xall, mods0, p_ab, rope_mla, lc)
    att = _mla_attn(q, k, v, lc)
    yf = _rwkv_scan(r, vv, kk, lw, ke, bb, lc, False)
    yb = _rwkv_scan(r, vv, kk, lw, ke, bb, lc, True)
    x1 = _ab_out(xall, att, yf, yb, bonus, g, mods0, p_ab, lc)
    x2 = _ffn(x1, mods0, _ffn_params(0, norm_ffn_g, ffn_w_up, ffn_conv_w, ffn_conv_b, ffn_w_down),
              (0, nbc, nb), seg_of, None)

    cw_in = cd_w_in[0]
    s1 = SSM_INNER
    s2 = s1 + SSM_INNER + 2 * SSM_G * SSM_N
    s3 = s2 + 2 * SSM_HEADS
    win_cd = jnp.concatenate([cw_in[:, :s2], padw(cw_in[:, s2:s3], LANES), cw_in[:, s3:]], axis=1).astype(BF16)
    a_neg = -jnp.exp(ssm_a_log[0])
    heads_to_lanes = np.zeros((LANES, SSM_INNER), np.float32)
    for hd in range(SSM_HEADS):
        heads_to_lanes[hd, hd * SSM_P:(hd + 1) * SSM_P] = 1.0
    p_cd = {
        'norm_g': row(norm_mix_g[1]), 'win': win_cd,
        'cw': ssm_conv_w[0], 'cb': row(ssm_conv_b[0]),
        'dtb': row(jnp.pad(ssm_dt_bias[0].reshape(-1), (0, LANES - 2 * SSM_HEADS))),
        'a_row': jnp.pad(a_neg, ((0, 0), (0, LANES - SSM_HEADS))).reshape(2, 1, LANES),
        'a_col': a_neg.reshape(2, SSM_HEADS, 1),
        'ex': jnp.asarray(heads_to_lanes, BF16),
        'dsk': row(jnp.repeat(ssm_d[0], SSM_P)),
        'ng': row(ssm_norm_g[0]), 'wout': cd_w_out[0].astype(BF16),
    }
    mods1 = mods_of(1)
    rope_swa = _rope_tables(l, lc, HEAD_DIM, _swa_place)
    z, xs, bcm, dt, sq, sk, sv = _cd_in(x2, mods1, p_cd, rope_swa, lc)
    dt2 = jnp.stack([jnp.pad(dt[:, :, dd * SSM_HEADS:(dd + 1) * SSM_HEADS],
                             ((0, 0), (0, 0), (0, LANES - SSM_HEADS))) for dd in range(2)])
    dtt2 = jnp.stack([jnp.swapaxes(dt[:, :, dd * SSM_HEADS:(dd + 1) * SSM_HEADS], 1, 2) for dd in range(2)])
    ys = _ssd(xs, bcm, dt2, dtt2, p_cd, lc)
    satt = _swa(sq, sk, sv, swa_sink[0], lc)
    x3 = _cd_out(x2, ys, z, satt, mods1, p_cd, lc)
    nbl = l // ROW_BLOCK
    return _ffn(x3, mods1, _ffn_params(1, norm_ffn_g, ffn_w_up, ffn_conv_w, ffn_conv_b, ffn_w_down),
                (0, nbl), lambda i: 1, row(final_norm_g))
```

```python
import functools
import math

import jax
import jax.numpy as jnp
import numpy as np
from jax import lax
from jax.experimental import pallas as pl
from jax.experimental.pallas import tpu as pltpu

F32 = jnp.float32
BF16 = jnp.bfloat16

GRID_W = 64
ROPE_THETA = 10000.0
NORM_EPS = 1e-6
HEAD_DIM = 64
MLA_HEADS, MLA_NOPE, MLA_ROPE, MLA_V = 8, 64, 32, 64
MLA_Q_RANK, MLA_KV_RANK = 384, 256
RWKV_HEADS, RWKV_N = 8, 64
RWKV_W = RWKV_HEADS * RWKV_N
DECAY_LORA, ICLR_LORA, GATE_LORA = 64, 64, 128
GN_EPS = 64e-5
SSM_HEADS, SSM_P, SSM_G, SSM_N = 16, 64, 2, 128
SSM_INNER = SSM_HEADS * SSM_P
SSM_CONV = 5
SWA_HEADS, SWA_KV_HEADS, WINDOW = 8, 2, 128
D_FF = 2816
FFN_CONV = 3

LANES = 128
HALO = 16
ROW_BLOCK = 256
RWKV_CHUNK = 64
RWKV_GROUP = 2
SSD_CHUNK = 128
SWA_BLOCK = 128
FFN_CHUNK = 256
FFN_DOWN_GROUP = 4
FFN_ROW_BLOCK_LATENT = 512
VMEM_LIMIT = 56 * 1024 * 1024
NEG_BIG = -1e30


def _cparams(sem):
    return pltpu.CompilerParams(dimension_semantics=sem, vmem_limit_bytes=VMEM_LIMIT)


def _sigmoid(x):
    return 1.0 / (1.0 + jnp.exp(-x))


def _silu(x):
    return x * _sigmoid(x)


def _rms(x, eps=NORM_EPS):
    return x * lax.rsqrt(jnp.mean(x * x, axis=-1, keepdims=True) + eps)


def _bdot(a, b):
    return jnp.dot(a.astype(BF16), b.astype(BF16), preferred_element_type=F32)


def _bdot_nt(a, b):
    return lax.dot_general(a.astype(BF16), b.astype(BF16), (((1,), (1,)), ((), ())),
                           preferred_element_type=F32)


def _bdot_tn(a, b):
    return lax.dot_general(a.astype(BF16), b.astype(BF16), (((0,), (0,)), ((), ())),
                           preferred_element_type=F32)


def _split3(x):
    hi = x.astype(BF16)
    r1 = x - hi.astype(F32)
    mid = r1.astype(BF16)
    lo = (r1 - mid.astype(F32)).astype(BF16)
    return hi, mid, lo


def _dot_exact_rhs(x, m):
    hi, mid, lo = _split3(x)
    return (jnp.dot(hi, m, preferred_element_type=F32) + jnp.dot(mid, m, preferred_element_type=F32)
            + jnp.dot(lo, m, preferred_element_type=F32))


def _group_sum(x, m):
    hi = x.astype(BF16)
    lo = (x - hi.astype(F32)).astype(BF16)
    return jnp.dot(hi, m, preferred_element_type=F32) + jnp.dot(lo, m, preferred_element_type=F32)


def _dot_exact_lhs(m, x):
    hi, mid, lo = _split3(x)
    return (jnp.dot(m, hi, preferred_element_type=F32) + jnp.dot(m, mid, preferred_element_type=F32)
            + jnp.dot(m, lo, preferred_element_type=F32))


def _dot_exact_rhs_nt(x, m):
    hi, mid, lo = _split3(x)
    dn = (((1,), (1,)), ((), ()))
    return (lax.dot_general(hi, m, dn, preferred_element_type=F32)
            + lax.dot_general(mid, m, dn, preferred_element_type=F32)
            + lax.dot_general(lo, m, dn, preferred_element_type=F32))


def _rope(t, cos, sin_a, sin_b, half):
    w = t.shape[-1]
    return t * cos + pltpu.roll(t, w - half, 1) * sin_a + pltpu.roll(t, half, 1) * sin_b


def _modnorm(x, g, shift, scale):
    return (_rms(x) * g) * (1.0 + scale) + shift


def _seg_flags(i, starts, ends):
    prev_ok = functools.reduce(jnp.logical_and, [i != s for s in starts])
    next_ok = functools.reduce(jnp.logical_and, [i != (e - 1) for e in ends])
    return prev_ok, next_ok


def _h_with_halo(x_ref, xp_ref, xn_ref, g, shift, scale, prev_ok, next_ok):
    h = _modnorm(x_ref[0], g, shift, scale)
    hp = _modnorm(xp_ref[0], g, shift, scale) * prev_ok.astype(F32)
    hn = _modnorm(xn_ref[0], g, shift, scale) * next_ok.astype(F32)
    return jnp.concatenate([hp.astype(BF16), h.astype(BF16), hn.astype(BF16)], axis=0)


def _shift_rows(x, delta, t):
    m = x.shape[0]
    return pltpu.roll(x, (m - delta) % m, 0)[HALO:HALO + t]


def _ada_kernel(c_ref, w_ref, b_ref, o_ref):
    o_ref[0] = _bdot(_silu(c_ref[...]), w_ref[0]) + b_ref[0]


def _ada_mods(cond, ada_w, ada_b):
    depth, d, n = ada_w.shape
    rows = cond.shape[0]
    tn = 1024
    return pl.pallas_call(
        _ada_kernel,
        grid=(depth, n // tn),
        in_specs=[pl.BlockSpec((rows, d), lambda l, j: (0, 0)),
                  pl.BlockSpec((1, d, tn), lambda l, j: (l, 0, j)),
                  pl.BlockSpec((1, 1, tn), lambda l, j: (l, 0, j))],
        out_specs=pl.BlockSpec((1, rows, tn), lambda l, j: (l, 0, j)),
        out_shape=jax.ShapeDtypeStruct((depth, rows, n), F32),
        compiler_params=_cparams(("parallel", "parallel")),
        name="ada_mods",
    )(cond, ada_w, ada_b.reshape(depth, 1, n))


def _row_specs(t, w, nrows8, off=0):
    per = t // HALO
    nh = nrows8
    main = pl.BlockSpec((1, t, w), lambda b, i: (b, i + off, 0))
    prev = pl.BlockSpec((1, HALO, w), lambda b, i: (b, jnp.maximum((i + off) * per - 1, 0), 0))
    nxt = pl.BlockSpec((1, HALO, w), lambda b, i: (b, jnp.minimum((i + off + 1) * per, nh - 1), 0))
    return main, prev, nxt


def _const_spec(shape):
    nd = len(shape)
    return pl.BlockSpec(shape, lambda *_: (0,) * nd)


def _resident_spec(shape):
    nd = len(shape)
    return pl.BlockSpec(shape, lambda *_: (0,) * nd, pipeline_mode=pl.Buffered(1))


def _ab_in_kernel(x_ref, xp_ref, xn_ref, mods_ref, g_ref, cos_ref, sa_ref, sb_ref, win_ref,
                  gq_ref, wq_ref, gkv_ref, wkk_ref, wvv_ref, mup_ref, mun_ref, g2_ref,
                  w0_ref, w2_ref, a0_ref, a2_ref, kkk_ref, ka_ref, rk_ref, bd_ref,
                  q_out, k_out, v_out, r_out, vv_out, kk_out, g_out, bonus_out,
                  lw_out, ke_out, b_out, *, starts, ends):
    i = pl.program_id(1)
    t = x_ref.shape[1]
    prev_ok, next_ok = _seg_flags(i, starts, ends)
    m = mods_ref[0, 0]
    h_all = _h_with_halo(x_ref, xp_ref, xn_ref, g_ref[...], m[0:1], m[1:2], prev_ok, next_ok)
    pin = jnp.dot(h_all, win_ref[...], preferred_element_type=F32)

    cos, sa, sb = cos_ref[...], sa_ref[...], sb_ref[...]
    main = pin[HALO:HALO + t]
    cq, ckv, kr = main[:, 0:384], main[:, 384:640], main[:, 640:768]
    qf = _bdot(_rms(cq) * gq_ref[...], wq_ref[...])
    kvn = (_rms(ckv) * gkv_ref[...]).astype(BF16)
    kf = jnp.dot(kvn, wkk_ref[...], preferred_element_type=F32)
    v_out[0] = jnp.dot(kvn, wvv_ref[...], preferred_element_type=F32).astype(BF16)
    krr = _rope(kr, cos, sa, sb, MLA_ROPE // 2)
    for h in range(MLA_HEADS):
        sl = slice(h * LANES, (h + 1) * LANES)
        q_out[0, h] = _rope(qf[:, sl], cos, sa, sb, MLA_ROPE // 2).astype(BF16)
        k_out[0, h] = (kf[:, sl] + krr).astype(BF16)

    rw = pin[:, 768:]
    x = rw[HALO:HALO + t]
    xp = _shift_rows(rw, -1, t)
    xn = _shift_rows(rw, 1, t)
    xs = x + mup_ref[...] * (xp - x) + mun_ref[...] * (xn - x)
    w = RWKV_W
    r, k, v = xs[:, 0:w], xs[:, w:2 * w], xs[:, 2 * w:3 * w]
    xw, xa, xg = xs[:, 3 * w:3 * w + 128], xs[:, 3 * w + 128:3 * w + 256], xs[:, 3 * w + 256:3 * w + 384]
    bd = bd_ref[...]
    g_out[0] = _bdot(_sigmoid(xg), g2_ref[...])
    kkr = k * kkk_ref[...]
    kk = kkr * lax.rsqrt(_group_sum(kkr * kkr, bd) + 1e-12)
    r_out[0] = r
    vv_out[0] = v
    kk_out[0] = kk
    bonus_out[0] = _group_sum(r * k * rk_ref[...], bd) * v
    tw = jnp.tanh(xw).astype(BF16)
    xab = xa.astype(BF16)
    for d in range(2):
        wl = w0_ref[d] + jnp.dot(tw, w2_ref[d], preferred_element_type=F32)
        lw_out[d, 0] = -math.exp(-0.5) * _sigmoid(wl)
        a = _sigmoid(a0_ref[d] + jnp.dot(xab, a2_ref[d], preferred_element_type=F32))
        ke_out[d, 0] = k * (1.0 + (a - 1.0) * ka_ref[...])
        b_out[d, 0] = kk * a


def _ab_in(x, mods, p, rope, lc):
    b, ltot, d = x.shape
    t = ROW_BLOCK
    nb, nbc = ltot // t, lc // t
    kern = functools.partial(_ab_in_kernel, starts=(0, nbc), ends=(nbc, nb))
    main, prev, nxt = _row_specs(t, d, ltot // HALO)
    tab = pl.BlockSpec((t, LANES), lambda bb, i: (i, 0))
    consts = [p['win'], p['gq'], p['wq'], p['gkv'], p['wkk'], p['wvv'], p['mup'], p['mun'], p['g2'],
              p['w0'], p['w2'], p['a0'], p['a2'], p['kkk'], p['ka'], p['rk'], p['bd']]
    in_specs = ([main, prev, nxt,
                 pl.BlockSpec((1, 1, 6, d), lambda bb, i: (bb, jnp.where(i >= nbc, 1, 0), 0, 0)),
                 _const_spec((1, d)), tab, tab, tab] + [_const_spec(c.shape) for c in consts])
    hq = pl.BlockSpec((1, MLA_HEADS, t, LANES), lambda bb, i: (bb, 0, i, 0))
    row = pl.BlockSpec((1, t, RWKV_W), lambda bb, i: (bb, i, 0))
    drow = pl.BlockSpec((2, 1, t, RWKV_W), lambda bb, i: (0, bb, i, 0))
    f_rows = jax.ShapeDtypeStruct((b, ltot, RWKV_W), F32)
    f_drows = jax.ShapeDtypeStruct((2, b, ltot, RWKV_W), F32)
    return pl.pallas_call(
        kern, grid=(b, nb), in_specs=in_specs,
        out_specs=[hq, hq, row, row, row, row, row, row, drow, drow, drow],
        out_shape=[jax.ShapeDtypeStruct((b, MLA_HEADS, ltot, LANES), BF16),
                   jax.ShapeDtypeStruct((b, MLA_HEADS, ltot, LANES), BF16),
                   jax.ShapeDtypeStruct((b, ltot, MLA_HEADS * MLA_V), BF16),
                   f_rows, f_rows, f_rows, f_rows, f_rows, f_drows, f_drows, f_drows],
        compiler_params=_cparams(("parallel", "parallel")),
        name="ab_in",
    )(x, x, x, mods, p['norm_g'], rope[0], rope[1], rope[2], *consts)


def _mla_attn_kernel(q_ref, k_ref, v_ref, o_ref, *, nk_ctx, scale):
    i = pl.program_id(1)
    nk_all = k_ref.shape[2]

    def attend(nk):
        for hp in range(MLA_HEADS // 2):
            outs = []
            for h in (2 * hp, 2 * hp + 1):
                s = lax.dot_general(q_ref[0, h], k_ref[0, h, 0:nk], (((1,), (1,)), ((), ())),
                                    preferred_element_type=F32) * scale
                mx = jnp.max(s, axis=-1, keepdims=True)
                e = jnp.exp(s - mx)
                den = jnp.sum(e, axis=-1, keepdims=True)
                o2 = jnp.dot(e.astype(BF16), v_ref[0, 0:nk, hp * LANES:(hp + 1) * LANES],
                             preferred_element_type=F32)
                outs.append(o2 / den)
            lane = lax.broadcasted_iota(jnp.int32, outs[0].shape, 1)
            o_ref[0, :, hp * LANES:(hp + 1) * LANES] = jnp.where(lane < MLA_V, outs[0], outs[1])

    pl.when(i == 0)(lambda: attend(nk_ctx))
    pl.when(i > 0)(lambda: attend(nk_all))


def _mla_attn(q, k, v, lc):
    b, hh, ltot, _ = q.shape
    t = lc
    kern = functools.partial(_mla_attn_kernel, nk_ctx=lc, scale=(MLA_NOPE + MLA_ROPE) ** -0.5)
    return pl.pallas_call(
        kern, grid=(b, ltot // t),
        in_specs=[pl.BlockSpec((1, hh, t, LANES), lambda bb, i: (bb, 0, i, 0)),
                  pl.BlockSpec((1, hh, ltot, LANES), lambda bb, i: (bb, 0, 0, 0)),
                  pl.BlockSpec((1, ltot, hh * MLA_V), lambda bb, i: (bb, 0, 0))],
        out_specs=pl.BlockSpec((1, t, hh * MLA_V), lambda bb, i: (bb, i, 0)),
        out_shape=jax.ShapeDtypeStruct((b, ltot, hh * MLA_V), F32),
        compiler_params=_cparams(("parallel", "arbitrary")),
        name="mla_attn",
    )(q, k, v)


def _rwkv_scan_kernel(r_ref, v_ref, kk_ref, lw_ref, ke_ref, b_ref, y_ref, s_ref, *, reverse, nchunks):
    n = pl.program_id(1)
    c = RWKV_CHUNK
    rows = nchunks * c

    @pl.when(n == 0)
    def _():
        s_ref[...] = jnp.zeros_like(s_ref)

    sgn = -1 if reverse else 1
    rowb = lax.broadcasted_iota(jnp.int32, (rows, rows), 0)
    colb = lax.broadcasted_iota(jnp.int32, (rows, rows), 1)
    same = (rowb // c) == (colb // c)
    inclb = jnp.logical_and(same, sgn * (rowb - colb) >= 0)
    row = lax.broadcasted_iota(jnp.int32, (c, c), 0)
    col = lax.broadcasted_iota(jnp.int32, (c, c), 1)
    incl = sgn * (row - col) >= 0
    strict = sgn * (row - col) > 0
    eye = (row == col).astype(F32)

    lw = lw_ref[0, 0]
    cum = _dot_exact_lhs(inclb.astype(BF16), lw)
    tot = _dot_exact_lhs(same.astype(BF16), lw)
    r, v, kk, ke, bb = r_ref[0], v_ref[0], kk_ref[0], ke_ref[0, 0], b_ref[0, 0]
    e_nc = jnp.exp(-cum)
    e_end = jnp.exp(tot - cum)
    rt = r * jnp.exp(cum)
    kt = kk * jnp.exp(cum - lw)
    bh, kh = bb * e_nc, ke * e_nc
    bc, kc = bb * e_end, ke * e_end
    gam = jnp.exp(tot)

    chains = [(g, h) for g in range(nchunks) for h in range(RWKV_HEADS)]
    blk = lambda x, g, h: x[g * c:(g + 1) * c, h * RWKV_N:(h + 1) * RWKV_N]
    ms, ns, pbs, pks = [], [], [], []
    for g, h in chains:
        a = jnp.concatenate([blk(kt, g, h), blk(rt, g, h)], axis=0)
        bm = jnp.concatenate([blk(bh, g, h), blk(kh, g, h)], axis=0)
        gm = _bdot_nt(a, bm)
        ms.append(jnp.where(strict, gm[0:c, 0:c], 0.0))
        ns.append(jnp.where(strict, gm[0:c, c:2 * c], 0.0))
        pbs.append(jnp.where(incl, gm[c:2 * c, 0:c], 0.0))
        pks.append(jnp.where(incl, gm[c:2 * c, c:2 * c], 0.0))
    xinv = [eye - m for m in ms]
    pw = [_bdot(m, m) for m in ms]
    steps = int(math.log2(c)) - 1
    for it in range(steps):
        if it < steps - 1:
            xp = [_bdot(jnp.concatenate([x, p], axis=0), p) for x, p in zip(xinv, pw)]
            xinv = [x + q[0:c] for x, q in zip(xinv, xp)]
            pw = [q[c:2 * c] for q in xp]
        else:
            xinv = [x + _bdot(x, p) for x, p in zip(xinv, pw)]
    npk = [_bdot(jnp.concatenate([nn, pk], axis=0), blk(v, g, h)) for (g, h), nn, pk in zip(chains, ns, pks)]
    kuu0 = [-_bdot(x, jnp.concatenate([blk(kt, g, h), q[0:c]], axis=1))
            for (g, h), x, q in zip(chains, xinv, npk)]
    pbk = [_bdot(pb, ku) for pb, ku in zip(pbs, kuu0)]
    ry = [(blk(rt, g, h) + q[:, 0:RWKV_N]).astype(BF16) for (g, h), q in zip(chains, pbk)]
    y0 = [q[:, RWKV_N:] + w[c:2 * c] for q, w in zip(pbk, npk)]
    tms = []
    for (g, h), ku in zip(chains, kuu0):
        lhs = jnp.concatenate([ku, jnp.concatenate([jnp.zeros((c, RWKV_N), F32), blk(v, g, h)], axis=1)],
                              axis=0)
        rhs = jnp.concatenate([blk(bc, g, h), blk(kc, g, h)], axis=0)
        tms.append(_bdot_tn(lhs, rhs))

    state = [s_ref[h] for h in range(RWKV_HEADS)]
    youts = [None] * nchunks
    for g in (range(nchunks - 1, -1, -1) if reverse else range(nchunks)):
        ys = []
        for h in range(RWKV_HEADS):
            k = g * RWKV_HEADS + h
            s0 = state[h]
            sb = s0.astype(BF16)
            ys.append(y0[k] + lax.dot_general(ry[k], sb, (((1,), (1,)), ((), ())), preferred_element_type=F32))
            state[h] = (s0 * gam[g * c:g * c + 1, h * RWKV_N:(h + 1) * RWKV_N]
                        + jnp.dot(sb, tms[k][0:RWKV_N].astype(BF16), preferred_element_type=F32)
                        + tms[k][RWKV_N:])
        youts[g] = jnp.concatenate(ys, axis=1)
    y_ref[0] = jnp.concatenate(youts, axis=0)
    for h in range(RWKV_HEADS):
        s_ref[h] = state[h]


def _scan_chunk_index(n, nc_ctx, nc, d):
    bwd = jnp.where(n < nc_ctx, nc_ctx - 1 - n, nc - 1 - (n - nc_ctx))
    return jnp.where(d == 0, n, bwd)


def _rwkv_scan(r, v, kk, lw, ke, bb, lc, reverse):
    b, ltot, w = r.shape
    nch = RWKV_GROUP
    rows = nch * RWKV_CHUNK
    nc, ncc = ltot // rows, lc // rows
    d = 1 if reverse else 0
    cidx = lambda n: _scan_chunk_index(n, ncc, nc, d)
    shared = pl.BlockSpec((1, rows, w), lambda bi, n: (bi, cidx(n), 0))
    perdir = pl.BlockSpec((1, 1, rows, w), lambda bi, n: (d, bi, cidx(n), 0))
    kern = functools.partial(_rwkv_scan_kernel, reverse=reverse, nchunks=nch)
    return pl.pallas_call(
        kern, grid=(b, nc),
        in_specs=[shared, shared, shared, perdir, perdir, perdir],
        out_specs=shared,
        out_shape=jax.ShapeDtypeStruct((b, ltot, w), F32),
        scratch_shapes=[pltpu.VMEM((RWKV_HEADS, RWKV_N, RWKV_N), F32)],
        compiler_params=_cparams(("parallel", "arbitrary")),
        name="rwkv_scan_bwd" if reverse else "rwkv_scan_fwd",
    )(r, v, kk, lw, ke, bb)


def _ab_out_kernel(x_ref, att_ref, yf_ref, yb_ref, bonus_ref, g_ref, mods_ref, lng_ref, lnb_ref, bd_ref,
                   wout_ref, o_ref):
    y = yf_ref[0] + yb_ref[0]
    bd = bd_ref[...]
    mu = _group_sum(y, bd) * (1.0 / RWKV_N)
    dl = y - mu
    var = _group_sum(dl * dl, bd) * (1.0 / RWKV_N)
    yn = dl * lax.rsqrt(var + GN_EPS) * lng_ref[...] + lnb_ref[...]
    rwo = (yn + bonus_ref[0]) * g_ref[0]
    feat = jnp.concatenate([att_ref[0].astype(BF16), rwo.astype(BF16)], axis=1)
    o = jnp.dot(feat, wout_ref[...], preferred_element_type=F32)
    o_ref[0] = x_ref[0] + mods_ref[0, 0][2:3] * o


def _ab_out(x, att, yf, yb, bonus, g, mods, p, lc):
    b, ltot, d = x.shape
    t = ROW_BLOCK
    nbc = lc // t
    row = lambda w: pl.BlockSpec((1, t, w), lambda bb, i: (bb, i, 0))
    consts = [p['lng'], p['lnb'], p['bd'], p['wout']]
    return pl.pallas_call(
        _ab_out_kernel, grid=(b, ltot // t),
        in_specs=[row(d), row(RWKV_W), row(RWKV_W), row(RWKV_W), row(RWKV_W), row(RWKV_W),
                  pl.BlockSpec((1, 1, 6, d), lambda bb, i: (bb, jnp.where(i >= nbc, 1, 0), 0, 0))]
        + [_const_spec(c.shape) for c in consts],
        out_specs=row(d),
        out_shape=jax.ShapeDtypeStruct((b, ltot, d), F32),
        compiler_params=_cparams(("parallel", "parallel")),
        name="ab_out",
    )(x, att, yf, yb, bonus, g, mods, *consts)


def _ffn_kernel(x_ref, xp_ref, xn_ref, mods_ref, g_ref, wg_ref, wv_ref, cwg_ref, cwv_ref,
                cbg_ref, cbv_ref, wd_ref, fg_ref, o_ref, act_ref, *, starts, ends, final_norm):
    i = pl.program_id(1)
    t = x_ref.shape[1]
    prev_ok, next_ok = _seg_flags(i, starts, ends)
    m = mods_ref[0, 0]
    h_all = _h_with_halo(x_ref, xp_ref, xn_ref, g_ref[...], m[3:4], m[4:5], prev_ok, next_ok)
    nch = wg_ref.shape[0]
    fc = wg_ref.shape[2]

    def conv(u, cw, cb):
        return (cw[0:1] * _shift_rows(u, -1, t) + cw[1:2] * u[HALO:HALO + t]
                + cw[2:3] * _shift_rows(u, 1, t) + cb)

    def up(c):
        return (jnp.dot(h_all, wg_ref[c], preferred_element_type=F32),
                jnp.dot(h_all, wv_ref[c], preferred_element_type=F32))

    nxt = up(0)
    acc = None
    lo = 0
    for c in range(nch):
        ug, uv = nxt
        if c + 1 < nch:
            nxt = up(c + 1)
        act = _silu(conv(ug, cwg_ref[c], cbg_ref[c])) * conv(uv, cwv_ref[c], cbv_ref[c])
        act_ref[:, c * fc:(c + 1) * fc] = act.astype(BF16)
        if (c + 1 - lo) == FFN_DOWN_GROUP or c + 1 == nch:
            part = jnp.dot(act_ref[:, lo * fc:(c + 1) * fc], wd_ref[lo * fc:(c + 1) * fc, :],
                           preferred_element_type=F32)
            acc = part if acc is None else acc + part
            lo = c + 1
    out = x_ref[0] + m[5:6] * acc
    if final_norm:
        out = _rms(out) * fg_ref[...]
    o_ref[0] = out


def _ffn(x, mods, p, t, seg_rows, seg_index, final_g):
    b, rows, d = x.shape
    bounds = [r // t for r in seg_rows]
    starts, ends = tuple(bounds[:-1]), tuple(bounds[1:])
    kern = functools.partial(_ffn_kernel, starts=starts, ends=ends, final_norm=final_g is not None)
    main, prev, nxt = _row_specs(t, d, rows // HALO)
    fg = final_g if final_g is not None else jnp.ones((1, d), F32)
    consts = [p['wg'], p['wv'], p['cwg'], p['cwv'], p['cbg'], p['cbv'], p['wd'], fg]
    return pl.pallas_call(
        kern, grid=(b, rows // t),
        in_specs=[main, prev, nxt,
                  pl.BlockSpec((1, 1, 6, d), lambda bb, i: (bb, seg_index(i), 0, 0)),
                  _const_spec((1, d))] + [_resident_spec(c.shape) for c in consts],
        out_specs=pl.BlockSpec((1, t, d), lambda bb, i: (bb, i, 0)),
        out_shape=jax.ShapeDtypeStruct((b, rows, d), F32),
        scratch_shapes=[pltpu.VMEM((t, D_FF), BF16)],
        compiler_params=_cparams(("parallel", "parallel")),
        name="conv_ffn",
    )(x, x, x, mods, p['norm_g'], *consts)


def _cd_in_kernel(x_ref, xp_ref, xn_ref, mods_ref, g_ref, cos_ref, sa_ref, sb_ref, win_ref,
                  cw_ref, cb_ref, dtb_ref,
                  z_out, xs_out, bc_out, dt_out, q_out, k_out, v_out, *, starts, ends):
    i = pl.program_id(1)
    t = x_ref.shape[1]
    prev_ok, next_ok = _seg_flags(i, starts, ends)
    m = mods_ref[0, 0]
    h_all = _h_with_halo(x_ref, xp_ref, xn_ref, g_ref[...], m[0:1], m[1:2], prev_ok, next_ok)
    pin = jnp.dot(h_all, win_ref[...], preferred_element_type=F32)
    main = pin[HALO:HALO + t]
    z_out[0] = main[:, 0:1024]
    xbc = pin[:, 1024:2560]
    cw = cw_ref[...]
    conv = cb_ref[...] + cw[2:3] * xbc[HALO:HALO + t]
    for j in (0, 1, 3, 4):
        conv = conv + cw[j:j + 1] * _shift_rows(xbc, j - SSM_CONV // 2, t)
    xc = _silu(conv)
    xs_out[0] = xc[:, 0:SSM_INNER]
    bc_out[0] = xc[:, SSM_INNER:]
    raw = main[:, 2560:2688] + dtb_ref[...]
    dt_out[0] = jnp.maximum(raw, 0.0) + jnp.log(1.0 + jnp.exp(-jnp.abs(raw)))
    cos, sa, sb = cos_ref[...], sa_ref[...], sb_ref[...]
    q = main[:, 2688:3200]
    tile = lambda a: jnp.concatenate([a] * (q.shape[1] // LANES), axis=1)
    q_out[0] = _rope(q, tile(cos), tile(sa), tile(sb), HEAD_DIM // 2).astype(BF16)
    k_out[0] = _rope(main[:, 3200:3328], cos, sa, sb, HEAD_DIM // 2).astype(BF16)
    v_out[0] = main[:, 3328:3456].astype(BF16)


def _cd_in(x, mods, p, rope, lc):
    b, ltot, d = x.shape
    t = ROW_BLOCK
    nb, nbc = ltot // t, lc // t
    kern = functools.partial(_cd_in_kernel, starts=(0, nbc), ends=(nbc, nb))
    main, prev, nxt = _row_specs(t, d, ltot // HALO)
    tab = pl.BlockSpec((t, LANES), lambda bb, i: (i, 0))
    consts = [p['win'], p['cw'], p['cb'], p['dtb']]
    row = lambda w: pl.BlockSpec((1, t, w), lambda bb, i: (bb, i, 0))
    sds = lambda w, dt: jax.ShapeDtypeStruct((b, ltot, w), dt)
    kvw = SWA_KV_HEADS * HEAD_DIM
    return pl.pallas_call(
        kern, grid=(b, nb),
        in_specs=[main, prev, nxt,
                  pl.BlockSpec((1, 1, 6, d), lambda bb, i: (bb, jnp.where(i >= nbc, 1, 0), 0, 0)),
                  _const_spec((1, d)), tab, tab, tab] + [_const_spec(c.shape) for c in consts],
        out_specs=[row(SSM_INNER), row(SSM_INNER), row(2 * SSM_G * SSM_N), row(LANES),
                   row(SWA_HEADS * HEAD_DIM), row(kvw), row(kvw)],
        out_shape=[sds(SSM_INNER, F32), sds(SSM_INNER, F32), sds(2 * SSM_G * SSM_N, F32), sds(LANES, F32),
                   sds(SWA_HEADS * HEAD_DIM, BF16), sds(kvw, BF16), sds(kvw, BF16)],
        compiler_params=_cparams(("parallel", "parallel")),
        name="cd_in",
    )(x, x, x, mods, p['norm_g'], rope[0], rope[1], rope[2], *consts)


def _ssd_kernel(xs_ref, bc_ref, dt_ref, dtt_ref, a_ref, at_ref, ex_ref, dsk_ref, y_ref, h_ref):
    d = pl.program_id(0)
    n = pl.program_id(2)
    q = SSD_CHUNK
    hg = SSM_HEADS // SSM_G
    gw = hg * SSM_P

    @pl.when(n == 0)
    def _():
        h_ref[...] = jnp.zeros_like(h_ref)

    row = lax.broadcasted_iota(jnp.int32, (q, q), 0)
    col = lax.broadcasted_iota(jnp.int32, (q, q), 1)
    fwd = d == 0
    incl = jnp.where(fwd, row - col, col - row) >= 0
    inclb = incl.astype(BF16)

    dt_c = dt_ref[0, 0]
    dt_r = dtt_ref[0, 0]
    acum_c = _dot_exact_lhs(inclb, dt_c * a_ref[0])
    acum_r = _dot_exact_rhs_nt(dt_r * at_ref[0], inclb)
    ex = ex_ref[...]
    acum_x = _dot_exact_rhs(acum_c, ex)
    dt_x = _dot_exact_rhs(dt_c, ex)
    end_x = jnp.where(fwd, acum_x[q - 1:q], acum_x[0:1])
    grow = jnp.exp(acum_x)
    tail = jnp.exp(end_x - acum_x) * dt_x
    chunk_decay = jnp.exp(end_x)
    skip = dsk_ref[...] * (d == 0).astype(F32)

    xs = xs_ref[0]
    bcm = bc_ref[0]
    for g in range(SSM_G):
        bg = bcm[:, g * SSM_N:(g + 1) * SSM_N].astype(BF16)
        cg = bcm[:, (SSM_G + g) * SSM_N:(SSM_G + g + 1) * SSM_N].astype(BF16)
        cb = lax.dot_general(cg, bg, (((1,), (1,)), ((), ())), preferred_element_type=F32)
        gs = slice(g * gw, (g + 1) * gw)
        xg = xs[:, gs]
        yd = []
        for hh in range(hg):
            hd = g * hg + hh
            seg = jnp.where(incl, acum_c[:, hd:hd + 1] - acum_r[hd:hd + 1, :], NEG_BIG)
            wgt = cb * jnp.exp(seg) * dt_r[hd:hd + 1, :]
            yd.append(_bdot(wgt, xg[:, hh * SSM_P:(hh + 1) * SSM_P]))
        h_in = h_ref[g]
        y_off = jnp.dot(cg, h_in.astype(BF16), preferred_element_type=F32) * grow[:, gs]
        y_ref[0, 0, :, gs] = jnp.concatenate(yd, axis=1) + y_off + skip[:, gs] * xg
        h_ref[g] = h_in * chunk_decay[:, gs] + _bdot_tn(bg, xg * tail[:, gs])


def _ssd(xs, bcm, dt2, dtt2, p, lc):
    b, ltot, inner = xs.shape
    q = SSD_CHUNK
    nc, ncc = ltot // q, lc // q
    cidx = lambda d, bi, n: _scan_chunk_index(n, ncc, nc, d)
    hg = SSM_HEADS // SSM_G
    return pl.pallas_call(
        _ssd_kernel, grid=(2, b, nc),
        in_specs=[pl.BlockSpec((1, q, inner), lambda d, bi, n: (bi, cidx(d, bi, n), 0)),
                  pl.BlockSpec((1, q, 2 * SSM_G * SSM_N), lambda d, bi, n: (bi, cidx(d, bi, n), 0)),
                  pl.BlockSpec((1, 1, q, LANES), lambda d, bi, n: (d, bi, cidx(d, bi, n), 0)),
                  pl.BlockSpec((1, 1, SSM_HEADS, q), lambda d, bi, n: (d, bi, 0, cidx(d, bi, n))),
                  pl.BlockSpec((1, 1, LANES), lambda d, bi, n: (d, 0, 0)),
                  pl.BlockSpec((1, SSM_HEADS, 1), lambda d, bi, n: (d, 0, 0)),
                  _const_spec(p['ex'].shape), _const_spec(p['dsk'].shape)],
        out_specs=pl.BlockSpec((1, 1, q, inner), lambda d, bi, n: (d, bi, cidx(d, bi, n), 0)),
        out_shape=jax.ShapeDtypeStruct((2, b, ltot, inner), F32),
        scratch_shapes=[pltpu.VMEM((SSM_G, SSM_N, hg * SSM_P), F32)],
        compiler_params=_cparams(("parallel", "parallel", "arbitrary")),
        name="ssd_scan",
    )(xs, bcm, dt2, dtt2, p['a_row'], p['a_col'], p['ex'], p['dsk'])


def _swa_kernel(sink_ref, q_ref, kc_ref, kp_ref, k0_ref, kn_ref, vc_ref, vp_ref, v0_ref, vn_ref, o_ref,
                *, nblocks, scale):
    i = pl.program_id(1)
    t = q_ref.shape[1]
    nctx = kc_ref.shape[1]
    hg = SWA_HEADS // SWA_KV_HEADS
    kall = jnp.concatenate([kc_ref[0], kp_ref[0], k0_ref[0], kn_ref[0]], axis=0)
    vall = jnp.concatenate([vc_ref[0], vp_ref[0], v0_ref[0], vn_ref[0]], axis=0)
    nk = nctx + 3 * t
    r = lax.broadcasted_iota(jnp.int32, (hg * t, nk), 0) % t
    c = lax.broadcasted_iota(jnp.int32, (hg * t, nk), 1) - nctx
    lo = jnp.where(i > 0, r, t)
    hi = jnp.where(i < nblocks - 1, r + 2 * t, 2 * t - 1)
    okg = jnp.logical_or(c < 0, jnp.logical_and(c >= lo, c <= hi))
    rowh = lax.broadcasted_iota(jnp.int32, (hg * t, 1), 0) // t
    qb = q_ref[0]
    for g in range(SWA_KV_HEADS):
        qg = jnp.concatenate([qb[:, (g * hg + j) * HEAD_DIM:(g * hg + j + 1) * HEAD_DIM] for j in range(hg)],
                             axis=0)
        kg = kall[:, g * HEAD_DIM:(g + 1) * HEAD_DIM]
        vg = vall[:, g * HEAD_DIM:(g + 1) * HEAD_DIM]
        s = lax.dot_general(qg, kg, (((1,), (1,)), ((), ())), preferred_element_type=F32) * scale
        s = jnp.where(okg, s, NEG_BIG)
        sink = jnp.zeros((hg * t, 1), F32)
        for j in range(hg):
            sink = jnp.where(rowh == j, sink_ref[g * hg + j], sink)
        mx = jnp.maximum(jnp.max(s, axis=-1, keepdims=True), sink)
        e = jnp.exp(s - mx)
        den = jnp.sum(e, axis=-1, keepdims=True) + jnp.exp(sink - mx)
        og = jnp.dot(e.astype(BF16), vg, preferred_element_type=F32) / den
        for j in range(hg):
            hd = g * hg + j
            o_ref[0, :, hd * HEAD_DIM:(hd + 1) * HEAD_DIM] = og[j * t:(j + 1) * t]


def _swa(q, k, v, sink, lc):
    b, ltot, qw = q.shape
    t = SWA_BLOCK
    nb = (ltot - lc) // t
    off = lc // t
    kvw = k.shape[-1]
    kern = functools.partial(_swa_kernel, nblocks=nb, scale=HEAD_DIM ** -0.5)
    ctx = pl.BlockSpec((1, lc, kvw), lambda bb, i: (bb, 0, 0))
    prev = pl.BlockSpec((1, t, kvw), lambda bb, i: (bb, off + jnp.maximum(i - 1, 0), 0))
    cur = pl.BlockSpec((1, t, kvw), lambda bb, i: (bb, off + i, 0))
    nxt = pl.BlockSpec((1, t, kvw), lambda bb, i: (bb, off + jnp.minimum(i + 1, nb - 1), 0))
    return pl.pallas_call(
        kern, grid=(b, nb),
        in_specs=[pl.BlockSpec(memory_space=pltpu.SMEM),
                  pl.BlockSpec((1, t, qw), lambda bb, i: (bb, off + i, 0)),
                  ctx, prev, cur, nxt, ctx, prev, cur, nxt],
        out_specs=pl.BlockSpec((1, t, qw), lambda bb, i: (bb, i, 0)),
        out_shape=jax.ShapeDtypeStruct((b, ltot - lc, qw), F32),
        compiler_params=_cparams(("parallel", "parallel")),
        name="swa_attn",
    )(sink, q, k, k, k, k, v, v, v, v)


def _cd_out_kernel(x_ref, y_ref, z_ref, att_ref, mods_ref, ng_ref, wout_ref, o_ref):
    u = (y_ref[0, 0] + y_ref[1, 0]) * _silu(z_ref[0])
    gw = SSM_INNER // SSM_G
    un = jnp.concatenate([_rms(u[:, g * gw:(g + 1) * gw]) for g in range(SSM_G)], axis=1) * ng_ref[...]
    feat = jnp.concatenate([un.astype(BF16), att_ref[0].astype(BF16)], axis=1)
    o = jnp.dot(feat, wout_ref[...], preferred_element_type=F32)
    o_ref[0] = x_ref[0] + mods_ref[0, 0][2:3] * o


def _cd_out(x, y, z, att, mods, p, lc):
    b, ltot, d = x.shape
    t = ROW_BLOCK
    off = lc // t
    nbl = (ltot - lc) // t
    full = lambda w: pl.BlockSpec((1, t, w), lambda bb, i: (bb, i + off, 0))
    return pl.pallas_call(
        _cd_out_kernel, grid=(b, nbl),
        in_specs=[full(d), pl.BlockSpec((2, 1, t, SSM_INNER), lambda bb, i: (0, bb, i + off, 0)),
                  full(SSM_INNER), pl.BlockSpec((1, t, att.shape[-1]), lambda bb, i: (bb, i, 0)),
                  pl.BlockSpec((1, 1, 6, d), lambda bb, i: (bb, 1, 0, 0)),
                  _const_spec(p['ng'].shape), _const_spec(p['wout'].shape)],
        out_specs=pl.BlockSpec((1, t, d), lambda bb, i: (bb, i, 0)),
        out_shape=jax.ShapeDtypeStruct((b, ltot - lc, d), F32),
        compiler_params=_cparams(("parallel", "parallel")),
        name="cd_out",
    )(x, y, z, att, mods, p['ng'], p['wout'])


def _pad_cols(w, width):
    return jnp.pad(w, ((0, 0), (0, width - w.shape[1])))


def _rope_tables(n_tok, lc, rot_dim, place):
    rows = n_tok // GRID_W
    rp, cp = jnp.meshgrid(jnp.arange(rows, dtype=F32), jnp.arange(GRID_W, dtype=F32), indexing='ij')
    n_freq = rot_dim // 4
    inv_freq = ROPE_THETA ** (-jnp.arange(n_freq, dtype=F32) / n_freq)
    ang = jnp.concatenate([rp.reshape(-1, 1) * inv_freq, cp.reshape(-1, 1) * inv_freq], axis=-1)
    cos = jnp.concatenate([jnp.ones((lc, rot_dim // 2), F32), jnp.cos(ang)], axis=0)
    sin = jnp.concatenate([jnp.zeros((lc, rot_dim // 2), F32), jnp.sin(ang)], axis=0)
    zero = jnp.zeros_like(sin)
    return place(cos, cos, 1.0), place(-sin, zero, 0.0), place(zero, sin, 0.0)


def _mla_place(first, second, fill):
    n = first.shape[0]
    return jnp.concatenate([jnp.full((n, MLA_NOPE), fill, F32), first, second,
                            jnp.zeros((n, LANES - MLA_NOPE - MLA_ROPE), F32)], axis=1)


def _swa_place(first, second, fill):
    return jnp.concatenate([first, second] * (LANES // HEAD_DIM), axis=1)


def _block_diag_ones(n, group):
    idx = np.arange(n) // group
    return jnp.asarray(idx[:, None] == idx[None, :], BF16)


def _ffn_params(i, norm_ffn_g, ffn_w_up, ffn_conv_w, ffn_conv_b, ffn_w_down):
    d = ffn_w_up.shape[1]
    nch = D_FF // FFN_CHUNK
    chunks = lambda w: w.reshape(w.shape[0], nch, FFN_CHUNK).transpose(1, 0, 2)
    wup = ffn_w_up[i].astype(BF16)
    return {
        'norm_g': norm_ffn_g[i].reshape(1, d),
        'wg': chunks(wup[:, :D_FF]), 'wv': chunks(wup[:, D_FF:]),
        'cwg': chunks(ffn_conv_w[i][:, :D_FF]), 'cwv': chunks(ffn_conv_w[i][:, D_FF:]),
        'cbg': chunks(ffn_conv_b[i][None, :D_FF]), 'cbv': chunks(ffn_conv_b[i][None, D_FF:]),
        'wd': ffn_w_down[i].astype(BF16),
    }


def kernel(x, c, ctx, c_ctx, ada_w, ada_b, norm_mix_g, norm_ffn_g, ffn_w_up, ffn_conv_w, ffn_conv_b, ffn_w_down, final_norm_g, ab_w_in, ab_w_out, mla_q_norm_g, mla_w_q_up, mla_kv_norm_g, mla_w_kv_up, rwkv_mu_prev, rwkv_mu_next, rwkv_w0, rwkv_w2, rwkv_a0, rwkv_a2, rwkv_g2, rwkv_k_k, rwkv_k_a, rwkv_r_k, rwkv_ln_g, rwkv_ln_b, cd_w_in, cd_w_out, ssm_conv_w, ssm_conv_b, ssm_dt_bias, ssm_a_log, ssm_d, ssm_norm_g, swa_sink):
    b, l, d = x.shape
    lc = ctx.shape[1]
    assert ada_w.shape[0] == 2 and lc % ROW_BLOCK == 0 and l % ROW_BLOCK == 0
    ltot = lc + l
    xall = jnp.concatenate([ctx, x], axis=1)

    nrow = -(-(b + 1) // 8) * 8
    cond = jnp.zeros((nrow, d), F32).at[:b].set(c).at[b].set(c_ctx)
    ada = _ada_mods(cond, ada_w, ada_b)
    def mods_of(i):
        lat = ada[i, :b].reshape(b, 1, 6, d)
        cx = jnp.broadcast_to(ada[i, b].reshape(1, 1, 6, d), (b, 1, 6, d))
        return jnp.concatenate([cx, lat], axis=1)

    row = lambda v: v.reshape(1, -1)
    nbc, nb = lc // ROW_BLOCK, ltot // ROW_BLOCK
    seg_of = lambda i: jnp.where(i >= nbc, 1, 0)

    w_in = ab_w_in[0]
    padw = lambda w, n: jnp.pad(w, ((0, 0), (0, n - w.shape[1])))
    o1, o2, o3 = MLA_Q_RANK, MLA_Q_RANK + MLA_KV_RANK, MLA_Q_RANK + MLA_KV_RANK + MLA_ROPE
    w_kr = jnp.pad(w_in[:, o2:o3], ((0, 0), (MLA_NOPE, LANES - MLA_NOPE - MLA_ROPE)))
    rw = w_in[:, o3:]
    rw_sizes = (3 * RWKV_W, DECAY_LORA, ICLR_LORA, GATE_LORA)
    def pad_rw(v):
        a, bq, cq_, dq = jnp.split(v, np.cumsum(rw_sizes)[:-1].tolist(), axis=-1)
        z = jnp.zeros(v.shape[:-1] + (LANES - DECAY_LORA,), v.dtype)
        return jnp.concatenate([a, bq, z, cq_, z, dq], axis=-1)
    win_ab = jnp.concatenate([w_in[:, :o2], w_kr, pad_rw(rw)], axis=1).astype(BF16)
    qd = MLA_NOPE + MLA_ROPE
    wq = jnp.pad(mla_w_q_up[0].reshape(MLA_Q_RANK, MLA_HEADS, qd), ((0, 0), (0, 0), (0, LANES - qd)))
    wkv = mla_w_kv_up[0].reshape(MLA_KV_RANK, MLA_HEADS, MLA_NOPE + MLA_V)
    wkk = jnp.pad(wkv[:, :, :MLA_NOPE], ((0, 0), (0, 0), (0, LANES - MLA_NOPE)))
    padrows = lambda w: jnp.pad(w, ((0, 0), (0, LANES - w.shape[1]), (0, 0)))
    p_ab = {
        'norm_g': row(norm_mix_g[0]), 'win': win_ab,
        'gq': row(mla_q_norm_g[0]), 'wq': wq.reshape(MLA_Q_RANK, MLA_HEADS * LANES).astype(BF16),
        'gkv': row(mla_kv_norm_g[0]), 'wkk': wkk.reshape(MLA_KV_RANK, MLA_HEADS * LANES).astype(BF16),
        'wvv': wkv[:, :, MLA_NOPE:].reshape(MLA_KV_RANK, MLA_HEADS * MLA_V).astype(BF16),
        'mup': row(pad_rw(rwkv_mu_prev[0])), 'mun': row(pad_rw(rwkv_mu_next[0])),
        'g2': rwkv_g2[0].astype(BF16),
        'w0': rwkv_w0[0].reshape(2, 1, RWKV_W), 'w2': padrows(rwkv_w2[0]).astype(BF16),
        'a0': rwkv_a0[0].reshape(2, 1, RWKV_W), 'a2': padrows(rwkv_a2[0]).astype(BF16),
        'kkk': row(rwkv_k_k[0]), 'ka': row(rwkv_k_a[0]), 'rk': row(rwkv_r_k[0]),
        'bd': _block_diag_ones(RWKV_W, RWKV_N),
        'lng': row(rwkv_ln_g[0]), 'lnb': row(rwkv_ln_b[0]), 'wout': ab_w_out[0].astype(BF16),
    }
    mods0 = mods_of(0)
    rope_mla = _rope_tables(l, lc, MLA_ROPE, _mla_place)
    (q, k, v, r, vv, kk, g, bonus, lw, ke, bb) = _ab_in(xall, mods0, p_ab, rope_mla, lc)
    att = _mla_attn(q, k, v, lc)
    yf = _rwkv_scan(r, vv, kk, lw, ke, bb, lc, False)
    yb = _rwkv_scan(r, vv, kk, lw, ke, bb, lc, True)
    x1 = _ab_out(xall, att, yf, yb, bonus, g, mods0, p_ab, lc)
    x2 = _ffn(x1, mods0, _ffn_params(0, norm_ffn_g, ffn_w_up, ffn_conv_w, ffn_conv_b, ffn_w_down),
              ROW_BLOCK, (0, lc, ltot), seg_of, None)

    cw_in = cd_w_in[0]
    s1 = SSM_INNER
    s2 = s1 + SSM_INNER + 2 * SSM_G * SSM_N
    s3 = s2 + 2 * SSM_HEADS
    win_cd = jnp.concatenate([cw_in[:, :s2], padw(cw_in[:, s2:s3], LANES), cw_in[:, s3:]], axis=1).astype(BF16)
    a_neg = -jnp.exp(ssm_a_log[0])
    heads_to_lanes = np.zeros((LANES, SSM_INNER), np.float32)
    for hd in range(SSM_HEADS):
        heads_to_lanes[hd, hd * SSM_P:(hd + 1) * SSM_P] = 1.0
    p_cd = {
        'norm_g': row(norm_mix_g[1]), 'win': win_cd,
        'cw': ssm_conv_w[0], 'cb': row(ssm_conv_b[0]),
        'dtb': row(jnp.pad(ssm_dt_bias[0].reshape(-1), (0, LANES - 2 * SSM_HEADS))),
        'a_row': jnp.pad(a_neg, ((0, 0), (0, LANES - SSM_HEADS))).reshape(2, 1, LANES),
        'a_col': a_neg.reshape(2, SSM_HEADS, 1),
        'ex': jnp.asarray(heads_to_lanes, BF16),
        'dsk': row(jnp.repeat(ssm_d[0], SSM_P)),
        'ng': row(ssm_norm_g[0]), 'wout': cd_w_out[0].astype(BF16),
    }
    mods1 = mods_of(1)
    rope_swa = _rope_tables(l, lc, HEAD_DIM, _swa_place)
    z, xs, bcm, dt, sq, sk, sv = _cd_in(x2, mods1, p_cd, rope_swa, lc)
    dt2 = jnp.stack([jnp.pad(dt[:, :, dd * SSM_HEADS:(dd + 1) * SSM_HEADS],
                             ((0, 0), (0, 0), (0, LANES - SSM_HEADS))) for dd in range(2)])
    dtt2 = jnp.stack([jnp.swapaxes(dt[:, :, dd * SSM_HEADS:(dd + 1) * SSM_HEADS], 1, 2) for dd in range(2)])
    ys = _ssd(xs, bcm, dt2, dtt2, p_cd, lc)
    satt = _swa(sq, sk, sv, swa_sink[0], lc)
    x3 = _cd_out(x2, ys, z, satt, mods1, p_cd, lc)
    t1 = FFN_ROW_BLOCK_LATENT if l % FFN_ROW_BLOCK_LATENT == 0 else ROW_BLOCK
    return _ffn(x3, mods1, _ffn_params(1, norm_ffn_g, ffn_w_up, ffn_conv_w, ffn_conv_b, ffn_w_down),
                t1, (0, l), lambda i: 1, row(final_norm_g))
```

```python
import functools
import math

import jax
import jax.numpy as jnp
import numpy as np
from jax import lax
from jax.experimental import pallas as pl
from jax.experimental.pallas import tpu as pltpu

F32 = jnp.float32
BF16 = jnp.bfloat16

GRID_W = 64
ROPE_THETA = 10000.0
NORM_EPS = 1e-6
HEAD_DIM = 64
MLA_HEADS, MLA_NOPE, MLA_ROPE, MLA_V = 8, 64, 32, 64
MLA_Q_RANK, MLA_KV_RANK = 384, 256
RWKV_HEADS, RWKV_N = 8, 64
RWKV_W = RWKV_HEADS * RWKV_N
DECAY_LORA, ICLR_LORA, GATE_LORA = 64, 64, 128
GN_EPS = 64e-5
SSM_HEADS, SSM_P, SSM_G, SSM_N = 16, 64, 2, 128
SSM_INNER = SSM_HEADS * SSM_P
SSM_CONV = 5
SWA_HEADS, SWA_KV_HEADS, WINDOW = 8, 2, 128
D_FF = 2816
FFN_CONV = 3

LANES = 128
HALO = 16
ROW_BLOCK = 256
MLA_Q_BLOCKS = 2
RWKV_CHUNK = 64
RWKV_GROUP = 4
RWKV_PACK = 4
SSD_CHUNK = 128
SWA_BLOCK = 128
FFN_CHUNK = 256
FFN_DOWN_GROUP = 4
FFN_ROW_BLOCK_LATENT = 512
VMEM_LIMIT = 56 * 1024 * 1024
NEG_BIG = -1e30


def _cparams(sem):
    return pltpu.CompilerParams(dimension_semantics=sem, vmem_limit_bytes=VMEM_LIMIT)


def _sigmoid(x):
    return 1.0 / (1.0 + jnp.exp(-x))


def _silu(x):
    return x * _sigmoid(x)


def _rms(x, eps=NORM_EPS):
    return x * lax.rsqrt(jnp.mean(x * x, axis=-1, keepdims=True) + eps)


def _bdot(a, b):
    return jnp.dot(a.astype(BF16), b.astype(BF16), preferred_element_type=F32)


def _bdot_nt(a, b):
    return lax.dot_general(a.astype(BF16), b.astype(BF16), (((1,), (1,)), ((), ())),
                           preferred_element_type=F32)


def _bdot_tn(a, b):
    return lax.dot_general(a.astype(BF16), b.astype(BF16), (((0,), (0,)), ((), ())),
                           preferred_element_type=F32)


def _split3(x):
    hi = x.astype(BF16)
    r1 = x - hi.astype(F32)
    mid = r1.astype(BF16)
    lo = (r1 - mid.astype(F32)).astype(BF16)
    return hi, mid, lo


def _dot_exact_rhs(x, m):
    hi, mid, lo = _split3(x)
    return (jnp.dot(hi, m, preferred_element_type=F32) + jnp.dot(mid, m, preferred_element_type=F32)
            + jnp.dot(lo, m, preferred_element_type=F32))


def _group_sum(x, m):
    hi = x.astype(BF16)
    lo = (x - hi.astype(F32)).astype(BF16)
    return jnp.dot(hi, m, preferred_element_type=F32) + jnp.dot(lo, m, preferred_element_type=F32)


def _dot_exact_lhs(m, x):
    hi, mid, lo = _split3(x)
    return (jnp.dot(m, hi, preferred_element_type=F32) + jnp.dot(m, mid, preferred_element_type=F32)
            + jnp.dot(m, lo, preferred_element_type=F32))


def _dot_exact_rhs_nt(x, m):
    hi, mid, lo = _split3(x)
    dn = (((1,), (1,)), ((), ()))
    return (lax.dot_general(hi, m, dn, preferred_element_type=F32)
            + lax.dot_general(mid, m, dn, preferred_element_type=F32)
            + lax.dot_general(lo, m, dn, preferred_element_type=F32))


def _rope(t, cos, sin_a, sin_b, half):
    w = t.shape[-1]
    return t * cos + pltpu.roll(t, w - half, 1) * sin_a + pltpu.roll(t, half, 1) * sin_b


def _modnorm(x, g, shift, scale):
    return (_rms(x) * g) * (1.0 + scale) + shift


def _seg_flags(i, starts, ends):
    prev_ok = functools.reduce(jnp.logical_and, [i != s for s in starts])
    next_ok = functools.reduce(jnp.logical_and, [i != (e - 1) for e in ends])
    return prev_ok, next_ok


def _h_with_halo(x_ref, xp_ref, xn_ref, g, shift, scale, prev_ok, next_ok):
    h = _modnorm(x_ref[0], g, shift, scale)
    hp = _modnorm(xp_ref[0], g, shift, scale) * prev_ok.astype(F32)
    hn = _modnorm(xn_ref[0], g, shift, scale) * next_ok.astype(F32)
    return jnp.concatenate([hp.astype(BF16), h.astype(BF16), hn.astype(BF16)], axis=0)


def _shift_rows(x, delta, t):
    m = x.shape[0]
    return pltpu.roll(x, (m - delta) % m, 0)[HALO:HALO + t]


def _ada_kernel(c_ref, w_ref, b_ref, o_ref):
    o_ref[0] = _bdot(_silu(c_ref[...]), w_ref[0]) + b_ref[0]


def _ada_mods(cond, ada_w, ada_b):
    depth, d, n = ada_w.shape
    rows = cond.shape[0]
    tn = 1024
    return pl.pallas_call(
        _ada_kernel,
        grid=(depth, n // tn),
        in_specs=[pl.BlockSpec((rows, d), lambda l, j: (0, 0)),
                  pl.BlockSpec((1, d, tn), lambda l, j: (l, 0, j)),
                  pl.BlockSpec((1, 1, tn), lambda l, j: (l, 0, j))],
        out_specs=pl.BlockSpec((1, rows, tn), lambda l, j: (l, 0, j)),
        out_shape=jax.ShapeDtypeStruct((depth, rows, n), F32),
        compiler_params=_cparams(("parallel", "parallel")),
        name="ada_mods",
    )(cond, ada_w, ada_b.reshape(depth, 1, n))


def _row_specs(t, w, nrows8, off=0):
    per = t // HALO
    nh = nrows8
    main = pl.BlockSpec((1, t, w), lambda b, i: (b, i + off, 0))
    prev = pl.BlockSpec((1, HALO, w), lambda b, i: (b, jnp.maximum((i + off) * per - 1, 0), 0))
    nxt = pl.BlockSpec((1, HALO, w), lambda b, i: (b, jnp.minimum((i + off + 1) * per, nh - 1), 0))
    return main, prev, nxt


def _const_spec(shape):
    nd = len(shape)
    return pl.BlockSpec(shape, lambda *_: (0,) * nd)


def _resident_spec(shape):
    nd = len(shape)
    return pl.BlockSpec(shape, lambda *_: (0,) * nd, pipeline_mode=pl.Buffered(1))


def _ab_in_kernel(x_ref, xp_ref, xn_ref, mods_ref, g_ref, cos_ref, sa_ref, sb_ref, win_ref,
                  gq_ref, wq_ref, gkv_ref, wkk_ref, wvv_ref, mup_ref, mun_ref, g2_ref,
                  w0_ref, w2_ref, a0_ref, a2_ref, kkk_ref, ka_ref, rk_ref, bd_ref,
                  q_out, k_out, v_out, r_out, vv_out, kk_out, g_out, bonus_out,
                  lw_out, ke_out, b_out, *, starts, ends):
    i = pl.program_id(1)
    t = x_ref.shape[1]
    prev_ok, next_ok = _seg_flags(i, starts, ends)
    m = mods_ref[0, 0]
    h_all = _h_with_halo(x_ref, xp_ref, xn_ref, g_ref[...], m[0:1], m[1:2], prev_ok, next_ok)
    pin = jnp.dot(h_all, win_ref[...], preferred_element_type=F32)

    cos, sa, sb = cos_ref[...], sa_ref[...], sb_ref[...]
    main = pin[HALO:HALO + t]
    cq, ckv, kr = main[:, 0:384], main[:, 384:640], main[:, 640:768]
    qf = _bdot(_rms(cq) * gq_ref[...], wq_ref[...])
    kvn = (_rms(ckv) * gkv_ref[...]).astype(BF16)
    kf = jnp.dot(kvn, wkk_ref[...], preferred_element_type=F32)
    v_out[0] = jnp.dot(kvn, wvv_ref[...], preferred_element_type=F32).astype(BF16)
    krr = _rope(kr, cos, sa, sb, MLA_ROPE // 2)
    for h in range(MLA_HEADS):
        sl = slice(h * LANES, (h + 1) * LANES)
        q_out[0, h] = _rope(qf[:, sl], cos, sa, sb, MLA_ROPE // 2).astype(BF16)
        k_out[0, h] = (kf[:, sl] + krr).astype(BF16)

    rw = pin[:, 768:]
    x = rw[HALO:HALO + t]
    xp = _shift_rows(rw, -1, t)
    xn = _shift_rows(rw, 1, t)
    xs = x + mup_ref[...] * (xp - x) + mun_ref[...] * (xn - x)
    w = RWKV_W
    r, k, v = xs[:, 0:w], xs[:, w:2 * w], xs[:, 2 * w:3 * w]
    xw, xa, xg = xs[:, 3 * w:3 * w + 128], xs[:, 3 * w + 128:3 * w + 256], xs[:, 3 * w + 256:3 * w + 384]
    bd = bd_ref[...]
    g_out[0] = _bdot(_sigmoid(xg), g2_ref[...])
    kkr = k * kkk_ref[...]
    kk = kkr * lax.rsqrt(_group_sum(kkr * kkr, bd) + 1e-12)
    r_out[0] = r
    vv_out[0] = v
    kk_out[0] = kk
    bonus_out[0] = _group_sum(r * k * rk_ref[...], bd) * v
    tw = jnp.tanh(xw).astype(BF16)
    xab = xa.astype(BF16)
    for d in range(2):
        wl = w0_ref[d] + jnp.dot(tw, w2_ref[d], preferred_element_type=F32)
        lw_out[d, 0] = -math.exp(-0.5) * _sigmoid(wl)
        a = _sigmoid(a0_ref[d] + jnp.dot(xab, a2_ref[d], preferred_element_type=F32))
        ke_out[d, 0] = k * (1.0 + (a - 1.0) * ka_ref[...])
        b_out[d, 0] = kk * a


def _ab_in(x, mods, p, rope, lc):
    b, ltot, d = x.shape
    t = ROW_BLOCK
    nb, nbc = ltot // t, lc // t
    kern = functools.partial(_ab_in_kernel, starts=(0, nbc), ends=(nbc, nb))
    main, prev, nxt = _row_specs(t, d, ltot // HALO)
    tab = pl.BlockSpec((t, LANES), lambda bb, i: (i, 0))
    consts = [p['win'], p['gq'], p['wq'], p['gkv'], p['wkk'], p['wvv'], p['mup'], p['mun'], p['g2'],
              p['w0'], p['w2'], p['a0'], p['a2'], p['kkk'], p['ka'], p['rk'], p['bd']]
    in_specs = ([main, prev, nxt,
                 pl.BlockSpec((1, 1, 6, d), lambda bb, i: (bb, jnp.where(i >= nbc, 1, 0), 0, 0)),
                 _const_spec((1, d)), tab, tab, tab] + [_const_spec(c.shape) for c in consts])
    hq = pl.BlockSpec((1, MLA_HEADS, t, LANES), lambda bb, i: (bb, 0, i, 0))
    row = pl.BlockSpec((1, t, RWKV_W), lambda bb, i: (bb, i, 0))
    drow = pl.BlockSpec((2, 1, t, RWKV_W), lambda bb, i: (0, bb, i, 0))
    f_rows = jax.ShapeDtypeStruct((b, ltot, RWKV_W), F32)
    f_drows = jax.ShapeDtypeStruct((2, b, ltot, RWKV_W), F32)
    return pl.pallas_call(
        kern, grid=(b, nb), in_specs=in_specs,
        out_specs=[hq, hq, row, row, row, row, row, row, drow, drow, drow],
        out_shape=[jax.ShapeDtypeStruct((b, MLA_HEADS, ltot, LANES), BF16),
                   jax.ShapeDtypeStruct((b, MLA_HEADS, ltot, LANES), BF16),
                   jax.ShapeDtypeStruct((b, ltot, MLA_HEADS * MLA_V), BF16),
                   f_rows, f_rows, f_rows, f_rows, f_rows, f_drows, f_drows, f_drows],
        compiler_params=_cparams(("parallel", "parallel")),
        name="ab_in",
    )(x, x, x, mods, p['norm_g'], rope[0], rope[1], rope[2], *consts)


def _mla_attn_kernel(*refs, nq, exp2_scale):
    q_refs, (k_ref, v_ref, o_ref) = refs[:nq], refs[nq:]
    for hp in range(MLA_HEADS // 2):
        outs = []
        for h in (2 * hp, 2 * hp + 1):
            q = q_refs[0][0, h] if nq == 1 else jnp.concatenate([r[0, h] for r in q_refs], axis=0)
            s = lax.dot_general(q, k_ref[0, h], (((1,), (1,)), ((), ())), preferred_element_type=F32)
            mx = jnp.max(s, axis=-1, keepdims=True)
            e = jnp.exp2((s - mx) * exp2_scale)
            den = jnp.sum(e, axis=-1, keepdims=True)
            o2 = jnp.dot(e.astype(BF16), v_ref[0, :, hp * LANES:(hp + 1) * LANES],
                         preferred_element_type=F32)
            outs.append(o2 / den)
        lane = lax.broadcasted_iota(jnp.int32, outs[0].shape, 1)
        o_ref[0, :, hp * LANES:(hp + 1) * LANES] = jnp.where(lane < MLA_V, outs[0], outs[1])


def _mla_attn(q, k, v, lc):
    b, hh, ltot, _ = q.shape
    l = ltot - lc
    vw = hh * MLA_V
    exp2_scale = (MLA_NOPE + MLA_ROPE) ** -0.5 * math.log2(math.e)
    att_c = pl.pallas_call(
        functools.partial(_mla_attn_kernel, nq=1, exp2_scale=exp2_scale), grid=(b,),
        in_specs=[pl.BlockSpec((1, hh, lc, LANES), lambda bb: (bb, 0, 0, 0)),
                  pl.BlockSpec((1, hh, lc, LANES), lambda bb: (bb, 0, 0, 0)),
                  pl.BlockSpec((1, lc, vw), lambda bb: (bb, 0, 0))],
        out_specs=pl.BlockSpec((1, lc, vw), lambda bb: (bb, 0, 0)),
        out_shape=jax.ShapeDtypeStruct((b, lc, vw), F32),
        compiler_params=_cparams(("parallel",)),
        name="mla_attn_ctx",
    )(q, k, v)
    nq = MLA_Q_BLOCKS if l % (MLA_Q_BLOCKS * lc) == 0 else 1
    off = 1
    qspec = lambda j: pl.BlockSpec((1, hh, lc, LANES), lambda bb, i, j=j: (bb, 0, off + nq * i + j, 0))
    att_l = pl.pallas_call(
        functools.partial(_mla_attn_kernel, nq=nq, exp2_scale=exp2_scale), grid=(b, l // (nq * lc)),
        in_specs=[qspec(j) for j in range(nq)]
        + [pl.BlockSpec((1, hh, ltot, LANES), lambda bb, i: (bb, 0, 0, 0)),
           pl.BlockSpec((1, ltot, vw), lambda bb, i: (bb, 0, 0))],
        out_specs=pl.BlockSpec((1, nq * lc, vw), lambda bb, i: (bb, i, 0)),
        out_shape=jax.ShapeDtypeStruct((b, l, vw), F32),
        compiler_params=_cparams(("parallel", "arbitrary")),
        name="mla_attn_lat",
    )(*([q] * nq), k, v)
    return att_c, att_l


def _rwkv_scan_kernel(rf_ref, vf_ref, kkf_ref, lwf_ref, kef_ref, bf_ref,
                      rb_ref, vb_ref, kkb_ref, lwb_ref, keb_ref, bb_ref,
                      yf_ref, yb_ref, s_ref, *, nchunks):
    n = pl.program_id(1)
    c = RWKV_CHUNK
    rows = nchunks * c

    @pl.when(n == 0)
    def _():
        s_ref[...] = jnp.zeros_like(s_ref)

    rowb = lax.broadcasted_iota(jnp.int32, (rows, rows), 0)
    colb = lax.broadcasted_iota(jnp.int32, (rows, rows), 1)
    same = (rowb // c) == (colb // c)
    hpg = RWKV_PACK
    gw = hpg * RWKV_N
    ngrp = RWKV_HEADS // hpg
    row = lax.broadcasted_iota(jnp.int32, (c, gw), 0)
    col = lax.broadcasted_iota(jnp.int32, (c, gw), 1) % RWKV_N
    eye = (row == col).astype(F32)
    lane_head = lax.broadcasted_iota(jnp.int32, (c, gw), 1) // RWKV_N
    head_masks = [lane_head == h for h in range(hpg)]
    diag_blocks = (lax.broadcasted_iota(jnp.int32, (gw, gw), 0) // RWKV_N
                   == lax.broadcasted_iota(jnp.int32, (gw, gw), 1) // RWKV_N)

    def bdiag(x):
        return jnp.concatenate([jnp.where(mk, x, 0.0) for mk in head_masks], axis=0).astype(BF16)

    def bmm(x, w):
        return jnp.dot(x.astype(BF16), bdiag(w), preferred_element_type=F32)

    def bmm_nt(x, w):
        return lax.dot_general(x.astype(BF16), bdiag(w), (((1,), (1,)), ((), ())), preferred_element_type=F32)

    dirs = []
    for sgn, (r_ref, v_ref, kk_ref, lw_ref, ke_ref, b_ref) in (
            (1, (rf_ref, vf_ref, kkf_ref, lwf_ref, kef_ref, bf_ref)),
            (-1, (rb_ref, vb_ref, kkb_ref, lwb_ref, keb_ref, bb_ref))):
        inclb = jnp.logical_and(same, sgn * (rowb - colb) >= 0)
        lw = lw_ref[0, 0]
        cum = _dot_exact_lhs(inclb.astype(BF16), lw)
        tot = _dot_exact_lhs(same.astype(BF16), lw)
        r, v, kk, ke, bb = r_ref[0], v_ref[0], kk_ref[0], ke_ref[0, 0], b_ref[0, 0]
        e_nc = jnp.exp(-cum)
        e_end = jnp.exp(tot - cum)
        dirs.append(dict(
            incl=sgn * (row - col) >= 0,
            strict=sgn * (row - col) > 0,
            v=v, rt=r * jnp.exp(cum), kt=kk * jnp.exp(cum - lw), bh=bb * e_nc, kh=ke * e_nc,
            bc=bb * e_end, kc=ke * e_end, gam=jnp.exp(tot)))

    chains = [(d, g, q) for g in range(nchunks) for q in range(ngrp) for d in range(2)]
    blk = lambda name, d, g, q: dirs[d][name][g * c:(g + 1) * c, q * gw:(q + 1) * gw]
    ms, ns, pbs, pks = [], [], [], []
    for d, g, q in chains:
        a = jnp.concatenate([blk('kt', d, g, q), blk('rt', d, g, q)], axis=0)
        mpb = bmm_nt(a, blk('bh', d, g, q))
        nk = bmm_nt(a, blk('kh', d, g, q))
        ms.append(jnp.where(dirs[d]['strict'], mpb[0:c], 0.0))
        pbs.append(jnp.where(dirs[d]['incl'], mpb[c:2 * c], 0.0))
        ns.append(jnp.where(dirs[d]['strict'], nk[0:c], 0.0))
        pks.append(jnp.where(dirs[d]['incl'], nk[c:2 * c], 0.0))
    xinv = [eye - m for m in ms]
    pw = [bmm(m, m) for m in ms]
    steps = int(math.log2(c)) - 1
    for it in range(steps):
        if it < steps - 1:
            xp = [bmm(jnp.concatenate([x, p], axis=0), p) for x, p in zip(xinv, pw)]
            xinv = [x + z[0:c] for x, z in zip(xinv, xp)]
            pw = [z[c:2 * c] for z in xp]
        else:
            xinv = [x + bmm(x, p) for x, p in zip(xinv, pw)]
    npk = [bmm(jnp.concatenate([nn, pk], axis=0), blk('v', *ch)) for ch, nn, pk in zip(chains, ns, pks)]
    ku = [-bmm(x, blk('kt', *ch)) for ch, x in zip(chains, xinv)]
    u0 = [-bmm(x, z[0:c]) for x, z in zip(xinv, npk)]
    ry = [(blk('rt', *ch) + bmm(pb, k_)).astype(BF16) for ch, pb, k_ in zip(chains, pbs, ku)]
    y0 = [bmm(pb, u_) + z[c:2 * c] for pb, u_, z in zip(pbs, u0, npk)]
    tmw, sadd = {}, {}
    for ch, k_, u_ in zip(chains, ku, u0):
        lhs = jnp.concatenate([jnp.concatenate([k_, u_], axis=1),
                               jnp.concatenate([jnp.zeros((c, gw), F32), blk('v', *ch)], axis=1)],
                              axis=0)
        rhs = jnp.concatenate([blk('bc', *ch), blk('kc', *ch)], axis=0)
        full = _bdot_tn(lhs, rhs)
        tmw[ch] = jnp.where(diag_blocks, full[0:gw], 0.0).astype(BF16)
        low = full[gw:2 * gw]
        sadd[ch] = functools.reduce(
            jnp.add, [jnp.where(head_masks[h], low[h * RWKV_N:(h + 1) * RWKV_N], 0.0) for h in range(hpg)])
    ry = dict(zip(chains, ry))
    y0 = dict(zip(chains, y0))

    state = {(d, q): s_ref[d, :, q * gw:(q + 1) * gw] for d in range(2) for q in range(ngrp)}
    youts = [[None] * nchunks for _ in range(2)]
    for step in range(nchunks):
        for d in range(2):
            g = step if d == 0 else nchunks - 1 - step
            ys = []
            for q in range(ngrp):
                ch = (d, g, q)
                s0 = state[(d, q)]
                ys.append(y0[ch] + lax.dot_general(ry[ch], bdiag(s0), (((1,), (1,)), ((), ())),
                                                   preferred_element_type=F32))
                state[(d, q)] = (s0 * dirs[d]['gam'][g * c:g * c + 1, q * gw:(q + 1) * gw]
                                 + jnp.dot(s0.astype(BF16), tmw[ch], preferred_element_type=F32) + sadd[ch])
            youts[d][g] = jnp.concatenate(ys, axis=1)
    yf_ref[0] = jnp.concatenate(youts[0], axis=0)
    yb_ref[0] = jnp.concatenate(youts[1], axis=0)
    for d in range(2):
        for q in range(ngrp):
            s_ref[d, :, q * gw:(q + 1) * gw] = state[(d, q)]


def _scan_chunk_index(n, nc_ctx, nc, d):
    bwd = jnp.where(n < nc_ctx, nc_ctx - 1 - n, nc - 1 - (n - nc_ctx))
    return jnp.where(d == 0, n, bwd)


def _rwkv_scan(r, v, kk, lw, ke, bb, lc):
    b, ltot, w = r.shape
    nch = RWKV_GROUP
    rows = nch * RWKV_CHUNK
    nc, ncc = ltot // rows, lc // rows
    specs = []
    for d in range(2):
        cidx = functools.partial(_scan_chunk_index, nc_ctx=ncc, nc=nc, d=d)
        shared = pl.BlockSpec((1, rows, w), lambda bi, n, cidx=cidx: (bi, cidx(n), 0))
        perdir = pl.BlockSpec((1, 1, rows, w), lambda bi, n, cidx=cidx, d=d: (d, bi, cidx(n), 0))
        specs.append((shared, perdir))
    kern = functools.partial(_rwkv_scan_kernel, nchunks=nch)
    out = jax.ShapeDtypeStruct((b, ltot, w), F32)
    return pl.pallas_call(
        kern, grid=(b, nc),
        in_specs=[specs[0][0]] * 3 + [specs[0][1]] * 3 + [specs[1][0]] * 3 + [specs[1][1]] * 3,
        out_specs=[specs[0][0], specs[1][0]],
        out_shape=[out, out],
        scratch_shapes=[pltpu.VMEM((2, RWKV_N, w), F32)],
        compiler_params=_cparams(("parallel", "arbitrary")),
        name="rwkv_scan",
    )(r, v, kk, lw, ke, bb, r, v, kk, lw, ke, bb)


def _ab_out_kernel(x_ref, attc_ref, attl_ref, yf_ref, yb_ref, bonus_ref, g_ref, mods_ref, lng_ref, lnb_ref,
                   bd_ref, wout_ref, o_ref, *, nbc):
    att = jnp.where(pl.program_id(1) < nbc, attc_ref[0], attl_ref[0])
    y = yf_ref[0] + yb_ref[0]
    bd = bd_ref[...]
    mu = _group_sum(y, bd) * (1.0 / RWKV_N)
    dl = y - mu
    var = _group_sum(dl * dl, bd) * (1.0 / RWKV_N)
    yn = dl * lax.rsqrt(var + GN_EPS) * lng_ref[...] + lnb_ref[...]
    rwo = (yn + bonus_ref[0]) * g_ref[0]
    feat = jnp.concatenate([att.astype(BF16), rwo.astype(BF16)], axis=1)
    o = jnp.dot(feat, wout_ref[...], preferred_element_type=F32)
    o_ref[0] = x_ref[0] + mods_ref[0, 0][2:3] * o


def _ab_out(x, att_c, att_l, yf, yb, bonus, g, mods, p, lc):
    b, ltot, d = x.shape
    t = ROW_BLOCK
    nbc = lc // t
    row = lambda w: pl.BlockSpec((1, t, w), lambda bb, i: (bb, i, 0))
    aw = att_c.shape[-1]
    consts = [p['lng'], p['lnb'], p['bd'], p['wout']]
    return pl.pallas_call(
        functools.partial(_ab_out_kernel, nbc=nbc), grid=(b, ltot // t),
        in_specs=[row(d),
                  pl.BlockSpec((1, t, aw), lambda bb, i: (bb, jnp.minimum(i, nbc - 1), 0)),
                  pl.BlockSpec((1, t, aw), lambda bb, i: (bb, jnp.maximum(i - nbc, 0), 0)),
                  row(RWKV_W), row(RWKV_W), row(RWKV_W), row(RWKV_W),
                  pl.BlockSpec((1, 1, 6, d), lambda bb, i: (bb, jnp.where(i >= nbc, 1, 0), 0, 0))]
        + [_const_spec(c.shape) for c in consts],
        out_specs=row(d),
        out_shape=jax.ShapeDtypeStruct((b, ltot, d), F32),
        compiler_params=_cparams(("parallel", "parallel")),
        name="ab_out",
    )(x, att_c, att_l, yf, yb, bonus, g, mods, *consts)


def _ffn_kernel(x_ref, xp_ref, xn_ref, mods_ref, g_ref, wg_ref, wv_ref, cwg_ref, cwv_ref,
                cbg_ref, cbv_ref, wd_ref, fg_ref, o_ref, act_ref, *, starts, ends, final_norm):
    i = pl.program_id(1)
    t = x_ref.shape[1]
    prev_ok, next_ok = _seg_flags(i, starts, ends)
    m = mods_ref[0, 0]
    h_all = _h_with_halo(x_ref, xp_ref, xn_ref, g_ref[...], m[3:4], m[4:5], prev_ok, next_ok)
    nch = wg_ref.shape[0]
    fc = wg_ref.shape[2]

    def conv(u, cw, cb):
        return (cw[0:1] * _shift_rows(u, -1, t) + cw[1:2] * u[HALO:HALO + t]
                + cw[2:3] * _shift_rows(u, 1, t) + cb)

    def up(c):
        return (jnp.dot(h_all, wg_ref[c], preferred_element_type=F32),
                jnp.dot(h_all, wv_ref[c], preferred_element_type=F32))

    nxt = up(0)
    acc = None
    lo = 0
    for c in range(nch):
        ug, uv = nxt
        if c + 1 < nch:
            nxt = up(c + 1)
        act = _silu(conv(ug, cwg_ref[c], cbg_ref[c])) * conv(uv, cwv_ref[c], cbv_ref[c])
        act_ref[:, c * fc:(c + 1) * fc] = act.astype(BF16)
        if (c + 1 - lo) == FFN_DOWN_GROUP or c + 1 == nch:
            part = jnp.dot(act_ref[:, lo * fc:(c + 1) * fc], wd_ref[lo * fc:(c + 1) * fc, :],
                           preferred_element_type=F32)
            acc = part if acc is None else acc + part
            lo = c + 1
    out = x_ref[0] + m[5:6] * acc
    if final_norm:
        out = _rms(out) * fg_ref[...]
    o_ref[0] = out


def _ffn(x, mods, p, t, seg_rows, seg_index, final_g):
    b, rows, d = x.shape
    bounds = [r // t for r in seg_rows]
    starts, ends = tuple(bounds[:-1]), tuple(bounds[1:])
    kern = functools.partial(_ffn_kernel, starts=starts, ends=ends, final_norm=final_g is not None)
    main, prev, nxt = _row_specs(t, d, rows // HALO)
    fg = final_g if final_g is not None else jnp.ones((1, d), F32)
    consts = [p['wg'], p['wv'], p['cwg'], p['cwv'], p['cbg'], p['cbv'], p['wd'], fg]
    return pl.pallas_call(
        kern, grid=(b, rows // t),
        in_specs=[main, prev, nxt,
                  pl.BlockSpec((1, 1, 6, d), lambda bb, i: (bb, seg_index(i), 0, 0)),
                  _const_spec((1, d))] + [_resident_spec(c.shape) for c in consts],
        out_specs=pl.BlockSpec((1, t, d), lambda bb, i: (bb, i, 0)),
        out_shape=jax.ShapeDtypeStruct((b, rows, d), F32),
        scratch_shapes=[pltpu.VMEM((t, D_FF), BF16)],
        compiler_params=_cparams(("parallel", "parallel")),
        name="conv_ffn",
    )(x, x, x, mods, p['norm_g'], *consts)


def _cd_in_kernel(x_ref, xp_ref, xn_ref, mods_ref, g_ref, cos_ref, sa_ref, sb_ref, win_ref,
                  cw_ref, cb_ref, dtb_ref,
                  z_out, xs_out, bc_out, dt_out, q_out, k_out, v_out, *, starts, ends):
    i = pl.program_id(1)
    t = x_ref.shape[1]
    prev_ok, next_ok = _seg_flags(i, starts, ends)
    m = mods_ref[0, 0]
    h_all = _h_with_halo(x_ref, xp_ref, xn_ref, g_ref[...], m[0:1], m[1:2], prev_ok, next_ok)
    pin = jnp.dot(h_all, win_ref[...], preferred_element_type=F32)
    main = pin[HALO:HALO + t]
    z_out[0] = main[:, 0:1024]
    xbc = pin[:, 1024:2560]
    cw = cw_ref[...]
    conv = cb_ref[...] + cw[2:3] * xbc[HALO:HALO + t]
    for j in (0, 1, 3, 4):
        conv = conv + cw[j:j + 1] * _shift_rows(xbc, j - SSM_CONV // 2, t)
    xc = _silu(conv)
    xs_out[0] = xc[:, 0:SSM_INNER]
    bc_out[0] = xc[:, SSM_INNER:]
    raw = main[:, 2560:2688] + dtb_ref[...]
    dt_out[0] = jnp.maximum(raw, 0.0) + jnp.log(1.0 + jnp.exp(-jnp.abs(raw)))
    cos, sa, sb = cos_ref[...], sa_ref[...], sb_ref[...]
    q = main[:, 2688:3200]
    tile = lambda a: jnp.concatenate([a] * (q.shape[1] // LANES), axis=1)
    q_out[0] = _rope(q, tile(cos), tile(sa), tile(sb), HEAD_DIM // 2).astype(BF16)
    k_out[0] = _rope(main[:, 3200:3328], cos, sa, sb, HEAD_DIM // 2).astype(BF16)
    v_out[0] = main[:, 3328:3456].astype(BF16)


def _cd_in(x, mods, p, rope, lc):
    b, ltot, d = x.shape
    t = ROW_BLOCK
    nb, nbc = ltot // t, lc // t
    kern = functools.partial(_cd_in_kernel, starts=(0, nbc), ends=(nbc, nb))
    main, prev, nxt = _row_specs(t, d, ltot // HALO)
    tab = pl.BlockSpec((t, LANES), lambda bb, i: (i, 0))
    consts = [p['win'], p['cw'], p['cb'], p['dtb']]
    row = lambda w: pl.BlockSpec((1, t, w), lambda bb, i: (bb, i, 0))
    sds = lambda w, dt: jax.ShapeDtypeStruct((b, ltot, w), dt)
    kvw = SWA_KV_HEADS * HEAD_DIM
    return pl.pallas_call(
        kern, grid=(b, nb),
        in_specs=[main, prev, nxt,
                  pl.BlockSpec((1, 1, 6, d), lambda bb, i: (bb, jnp.where(i >= nbc, 1, 0), 0, 0)),
                  _const_spec((1, d)), tab, tab, tab] + [_const_spec(c.shape) for c in consts],
        out_specs=[row(SSM_INNER), row(SSM_INNER), row(2 * SSM_G * SSM_N), row(LANES),
                   row(SWA_HEADS * HEAD_DIM), row(kvw), row(kvw)],
        out_shape=[sds(SSM_INNER, F32), sds(SSM_INNER, F32), sds(2 * SSM_G * SSM_N, F32), sds(LANES, F32),
                   sds(SWA_HEADS * HEAD_DIM, BF16), sds(kvw, BF16), sds(kvw, BF16)],
        compiler_params=_cparams(("parallel", "parallel")),
        name="cd_in",
    )(x, x, x, mods, p['norm_g'], rope[0], rope[1], rope[2], *consts)


def _ssd_kernel(xs_ref, bc_ref, dt_ref, dtt_ref, a_ref, at_ref, ex_ref, dsk_ref, y_ref, h_ref):
    d = pl.program_id(0)
    n = pl.program_id(2)
    q = SSD_CHUNK
    hg = SSM_HEADS // SSM_G
    gw = hg * SSM_P

    @pl.when(n == 0)
    def _():
        h_ref[...] = jnp.zeros_like(h_ref)

    row = lax.broadcasted_iota(jnp.int32, (q, q), 0)
    col = lax.broadcasted_iota(jnp.int32, (q, q), 1)
    fwd = d == 0
    incl = jnp.where(fwd, row - col, col - row) >= 0
    inclb = incl.astype(BF16)

    dt_c = dt_ref[0, 0]
    dt_r = dtt_ref[0, 0]
    acum_c = _dot_exact_lhs(inclb, dt_c * a_ref[0])
    acum_r = _dot_exact_rhs_nt(dt_r * at_ref[0], inclb)
    ex = ex_ref[...]
    acum_x = _dot_exact_rhs(acum_c, ex)
    dt_x = _dot_exact_rhs(dt_c, ex)
    end_x = jnp.where(fwd, acum_x[q - 1:q], acum_x[0:1])
    grow = jnp.exp(acum_x)
    tail = jnp.exp(end_x - acum_x) * dt_x
    chunk_decay = jnp.exp(end_x)
    skip = dsk_ref[...] * (d == 0).astype(F32)

    xs = xs_ref[0]
    bcm = bc_ref[0]
    for g in range(SSM_G):
        bg = bcm[:, g * SSM_N:(g + 1) * SSM_N].astype(BF16)
        cg = bcm[:, (SSM_G + g) * SSM_N:(SSM_G + g + 1) * SSM_N].astype(BF16)
        cb = lax.dot_general(cg, bg, (((1,), (1,)), ((), ())), preferred_element_type=F32)
        gs = slice(g * gw, (g + 1) * gw)
        xg = xs[:, gs]
        yd = []
        for hh in range(hg):
            hd = g * hg + hh
            seg = jnp.where(incl, acum_c[:, hd:hd + 1] - acum_r[hd:hd + 1, :], NEG_BIG)
            wgt = cb * jnp.exp(seg) * dt_r[hd:hd + 1, :]
            yd.append(_bdot(wgt, xg[:, hh * SSM_P:(hh + 1) * SSM_P]))
        h_in = h_ref[g]
        y_off = jnp.dot(cg, h_in.astype(BF16), preferred_element_type=F32) * grow[:, gs]
        y_ref[0, 0, :, gs] = jnp.concatenate(yd, axis=1) + y_off + skip[:, gs] * xg
        h_ref[g] = h_in * chunk_decay[:, gs] + _bdot_tn(bg, xg * tail[:, gs])


def _ssd(xs, bcm, dt2, dtt2, p, lc):
    b, ltot, inner = xs.shape
    q = SSD_CHUNK
    nc, ncc = ltot // q, lc // q
    cidx = lambda d, bi, n: _scan_chunk_index(n, ncc, nc, d)
    hg = SSM_HEADS // SSM_G
    return pl.pallas_call(
        _ssd_kernel, grid=(2, b, nc),
        in_specs=[pl.BlockSpec((1, q, inner), lambda d, bi, n: (bi, cidx(d, bi, n), 0)),
                  pl.BlockSpec((1, q, 2 * SSM_G * SSM_N), lambda d, bi, n: (bi, cidx(d, bi, n), 0)),
                  pl.BlockSpec((1, 1, q, LANES), lambda d, bi, n: (d, bi, cidx(d, bi, n), 0)),
                  pl.BlockSpec((1, 1, SSM_HEADS, q), lambda d, bi, n: (d, bi, 0, cidx(d, bi, n))),
                  pl.BlockSpec((1, 1, LANES), lambda d, bi, n: (d, 0, 0)),
                  pl.BlockSpec((1, SSM_HEADS, 1), lambda d, bi, n: (d, 0, 0)),
                  _const_spec(p['ex'].shape), _const_spec(p['dsk'].shape)],
        out_specs=pl.BlockSpec((1, 1, q, inner), lambda d, bi, n: (d, bi, cidx(d, bi, n), 0)),
        out_shape=jax.ShapeDtypeStruct((2, b, ltot, inner), F32),
        scratch_shapes=[pltpu.VMEM((SSM_G, SSM_N, hg * SSM_P), F32)],
        compiler_params=_cparams(("parallel", "parallel", "arbitrary")),
        name="ssd_scan",
    )(xs, bcm, dt2, dtt2, p['a_row'], p['a_col'], p['ex'], p['dsk'])


def _swa_kernel(sink_ref, q_ref, kc_ref, kp_ref, k0_ref, kn_ref, vc_ref, vp_ref, v0_ref, vn_ref, o_ref,
                *, nblocks, scale):
    i = pl.program_id(1)
    t = q_ref.shape[1]
    nctx = kc_ref.shape[1]
    hg = SWA_HEADS // SWA_KV_HEADS
    kall = jnp.concatenate([kc_ref[0], kp_ref[0], k0_ref[0], kn_ref[0]], axis=0)
    vall = jnp.concatenate([vc_ref[0], vp_ref[0], v0_ref[0], vn_ref[0]], axis=0)
    nk = nctx + 3 * t
    r = lax.broadcasted_iota(jnp.int32, (hg * t, nk), 0) % t
    c = lax.broadcasted_iota(jnp.int32, (hg * t, nk), 1) - nctx
    lo = jnp.where(i > 0, r, t)
    hi = jnp.where(i < nblocks - 1, r + 2 * t, 2 * t - 1)
    okg = jnp.logical_or(c < 0, jnp.logical_and(c >= lo, c <= hi))
    rowh = lax.broadcasted_iota(jnp.int32, (hg * t, 1), 0) // t
    qb = q_ref[0]
    for g in range(SWA_KV_HEADS):
        qg = jnp.concatenate([qb[:, (g * hg + j) * HEAD_DIM:(g * hg + j + 1) * HEAD_DIM] for j in range(hg)],
                             axis=0)
        kg = kall[:, g * HEAD_DIM:(g + 1) * HEAD_DIM]
        vg = vall[:, g * HEAD_DIM:(g + 1) * HEAD_DIM]
        s = lax.dot_general(qg, kg, (((1,), (1,)), ((), ())), preferred_element_type=F32) * scale
        s = jnp.where(okg, s, NEG_BIG)
        sink = jnp.zeros((hg * t, 1), F32)
        for j in range(hg):
            sink = jnp.where(rowh == j, sink_ref[g * hg + j], sink)
        mx = jnp.maximum(jnp.max(s, axis=-1, keepdims=True), sink)
        e = jnp.exp(s - mx)
        den = jnp.sum(e, axis=-1, keepdims=True) + jnp.exp(sink - mx)
        og = jnp.dot(e.astype(BF16), vg, preferred_element_type=F32) / den
        for j in range(hg):
            hd = g * hg + j
            o_ref[0, :, hd * HEAD_DIM:(hd + 1) * HEAD_DIM] = og[j * t:(j + 1) * t]


def _swa(q, k, v, sink, lc):
    b, ltot, qw = q.shape
    t = SWA_BLOCK
    nb = (ltot - lc) // t
    off = lc // t
    kvw = k.shape[-1]
    kern = functools.partial(_swa_kernel, nblocks=nb, scale=HEAD_DIM ** -0.5)
    ctx = pl.BlockSpec((1, lc, kvw), lambda bb, i: (bb, 0, 0))
    prev = pl.BlockSpec((1, t, kvw), lambda bb, i: (bb, off + jnp.maximum(i - 1, 0), 0))
    cur = pl.BlockSpec((1, t, kvw), lambda bb, i: (bb, off + i, 0))
    nxt = pl.BlockSpec((1, t, kvw), lambda bb, i: (bb, off + jnp.minimum(i + 1, nb - 1), 0))
    return pl.pallas_call(
        kern, grid=(b, nb),
        in_specs=[pl.BlockSpec(memory_space=pltpu.SMEM),
                  pl.BlockSpec((1, t, qw), lambda bb, i: (bb, off + i, 0)),
                  ctx, prev, cur, nxt, ctx, prev, cur, nxt],
        out_specs=pl.BlockSpec((1, t, qw), lambda bb, i: (bb, i, 0)),
        out_shape=jax.ShapeDtypeStruct((b, ltot - lc, qw), F32),
        compiler_params=_cparams(("parallel", "parallel")),
        name="swa_attn",
    )(sink, q, k, k, k, k, v, v, v, v)


def _cd_out_kernel(x_ref, y_ref, z_ref, att_ref, mods_ref, ng_ref, wout_ref, o_ref):
    u = (y_ref[0, 0] + y_ref[1, 0]) * _silu(z_ref[0])
    gw = SSM_INNER // SSM_G
    un = jnp.concatenate([_rms(u[:, g * gw:(g + 1) * gw]) for g in range(SSM_G)], axis=1) * ng_ref[...]
    feat = jnp.concatenate([un.astype(BF16), att_ref[0].astype(BF16)], axis=1)
    o = jnp.dot(feat, wout_ref[...], preferred_element_type=F32)
    o_ref[0] = x_ref[0] + mods_ref[0, 0][2:3] * o


def _cd_out(x, y, z, att, mods, p, lc):
    b, ltot, d = x.shape
    t = ROW_BLOCK
    off = lc // t
    nbl = (ltot - lc) // t
    full = lambda w: pl.BlockSpec((1, t, w), lambda bb, i: (bb, i + off, 0))
    return pl.pallas_call(
        _cd_out_kernel, grid=(b, nbl),
        in_specs=[full(d), pl.BlockSpec((2, 1, t, SSM_INNER), lambda bb, i: (0, bb, i + off, 0)),
                  full(SSM_INNER), pl.BlockSpec((1, t, att.shape[-1]), lambda bb, i: (bb, i, 0)),
                  pl.BlockSpec((1, 1, 6, d), lambda bb, i: (bb, 1, 0, 0)),
                  _const_spec(p['ng'].shape), _const_spec(p['wout'].shape)],
        out_specs=pl.BlockSpec((1, t, d), lambda bb, i: (bb, i, 0)),
        out_shape=jax.ShapeDtypeStruct((b, ltot - lc, d), F32),
        compiler_params=_cparams(("parallel", "parallel")),
        name="cd_out",
    )(x, y, z, att, mods, p['ng'], p['wout'])


def _pad_cols(w, width):
    return jnp.pad(w, ((0, 0), (0, width - w.shape[1])))


def _rope_tables(n_tok, lc, rot_dim, place):
    rows = n_tok // GRID_W
    rp, cp = jnp.meshgrid(jnp.arange(rows, dtype=F32), jnp.arange(GRID_W, dtype=F32), indexing='ij')
    n_freq = rot_dim // 4
    inv_freq = ROPE_THETA ** (-jnp.arange(n_freq, dtype=F32) / n_freq)
    ang = jnp.concatenate([rp.reshape(-1, 1) * inv_freq, cp.reshape(-1, 1) * inv_freq], axis=-1)
    cos = jnp.concatenate([jnp.ones((lc, rot_dim // 2), F32), jnp.cos(ang)], axis=0)
    sin = jnp.concatenate([jnp.zeros((lc, rot_dim // 2), F32), jnp.sin(ang)], axis=0)
    zero = jnp.zeros_like(sin)
    return place(cos, cos, 1.0), place(-sin, zero, 0.0), place(zero, sin, 0.0)


def _mla_place(first, second, fill):
    n = first.shape[0]
    return jnp.concatenate([jnp.full((n, MLA_NOPE), fill, F32), first, second,
                            jnp.zeros((n, LANES - MLA_NOPE - MLA_ROPE), F32)], axis=1)


def _swa_place(first, second, fill):
    return jnp.concatenate([first, second] * (LANES // HEAD_DIM), axis=1)


def _block_diag_ones(n, group):
    idx = np.arange(n) // group
    return jnp.asarray(idx[:, None] == idx[None, :], BF16)


def _ffn_params(i, norm_ffn_g, ffn_w_up, ffn_conv_w, ffn_conv_b, ffn_w_down):
    d = ffn_w_up.shape[1]
    nch = D_FF // FFN_CHUNK
    chunks = lambda w: w.reshape(w.shape[0], nch, FFN_CHUNK).transpose(1, 0, 2)
    wup = ffn_w_up[i].astype(BF16)
    return {
        'norm_g': norm_ffn_g[i].reshape(1, d),
        'wg': chunks(wup[:, :D_FF]), 'wv': chunks(wup[:, D_FF:]),
        'cwg': chunks(ffn_conv_w[i][:, :D_FF]), 'cwv': chunks(ffn_conv_w[i][:, D_FF:]),
        'cbg': chunks(ffn_conv_b[i][None, :D_FF]), 'cbv': chunks(ffn_conv_b[i][None, D_FF:]),
        'wd': ffn_w_down[i].astype(BF16),
    }


def kernel(x, c, ctx, c_ctx, ada_w, ada_b, norm_mix_g, norm_ffn_g, ffn_w_up, ffn_conv_w, ffn_conv_b, ffn_w_down, final_norm_g, ab_w_in, ab_w_out, mla_q_norm_g, mla_w_q_up, mla_kv_norm_g, mla_w_kv_up, rwkv_mu_prev, rwkv_mu_next, rwkv_w0, rwkv_w2, rwkv_a0, rwkv_a2, rwkv_g2, rwkv_k_k, rwkv_k_a, rwkv_r_k, rwkv_ln_g, rwkv_ln_b, cd_w_in, cd_w_out, ssm_conv_w, ssm_conv_b, ssm_dt_bias, ssm_a_log, ssm_d, ssm_norm_g, swa_sink):
    b, l, d = x.shape
    lc = ctx.shape[1]
    assert ada_w.shape[0] == 2 and lc % ROW_BLOCK == 0 and l % ROW_BLOCK == 0
    ltot = lc + l
    xall = jnp.concatenate([ctx, x], axis=1)

    nrow = -(-(b + 1) // 8) * 8
    cond = jnp.zeros((nrow, d), F32).at[:b].set(c).at[b].set(c_ctx)
    ada = _ada_mods(cond, ada_w, ada_b)
    def mods_of(i):
        lat = ada[i, :b].reshape(b, 1, 6, d)
        cx = jnp.broadcast_to(ada[i, b].reshape(1, 1, 6, d), (b, 1, 6, d))
        return jnp.concatenate([cx, lat], axis=1)

    row = lambda v: v.reshape(1, -1)
    nbc, nb = lc // ROW_BLOCK, ltot // ROW_BLOCK
    seg_of = lambda i: jnp.where(i >= nbc, 1, 0)

    w_in = ab_w_in[0]
    padw = lambda w, n: jnp.pad(w, ((0, 0), (0, n - w.shape[1])))
    o1, o2, o3 = MLA_Q_RANK, MLA_Q_RANK + MLA_KV_RANK, MLA_Q_RANK + MLA_KV_RANK + MLA_ROPE
    w_kr = jnp.pad(w_in[:, o2:o3], ((0, 0), (MLA_NOPE, LANES - MLA_NOPE - MLA_ROPE)))
    rw = w_in[:, o3:]
    rw_sizes = (3 * RWKV_W, DECAY_LORA, ICLR_LORA, GATE_LORA)
    def pad_rw(v):
        a, bq, cq_, dq = jnp.split(v, np.cumsum(rw_sizes)[:-1].tolist(), axis=-1)
        z = jnp.zeros(v.shape[:-1] + (LANES - DECAY_LORA,), v.dtype)
        return jnp.concatenate([a, bq, z, cq_, z, dq], axis=-1)
    win_ab = jnp.concatenate([w_in[:, :o2], w_kr, pad_rw(rw)], axis=1).astype(BF16)
    qd = MLA_NOPE + MLA_ROPE
    wq = jnp.pad(mla_w_q_up[0].reshape(MLA_Q_RANK, MLA_HEADS, qd), ((0, 0), (0, 0), (0, LANES - qd)))
    wkv = mla_w_kv_up[0].reshape(MLA_KV_RANK, MLA_HEADS, MLA_NOPE + MLA_V)
    wkk = jnp.pad(wkv[:, :, :MLA_NOPE], ((0, 0), (0, 0), (0, LANES - MLA_NOPE)))
    padrows = lambda w: jnp.pad(w, ((0, 0), (0, LANES - w.shape[1]), (0, 0)))
    p_ab = {
        'norm_g': row(norm_mix_g[0]), 'win': win_ab,
        'gq': row(mla_q_norm_g[0]), 'wq': wq.reshape(MLA_Q_RANK, MLA_HEADS * LANES).astype(BF16),
        'gkv': row(mla_kv_norm_g[0]), 'wkk': wkk.reshape(MLA_KV_RANK, MLA_HEADS * LANES).astype(BF16),
        'wvv': wkv[:, :, MLA_NOPE:].reshape(MLA_KV_RANK, MLA_HEADS * MLA_V).astype(BF16),
        'mup': row(pad_rw(rwkv_mu_prev[0])), 'mun': row(pad_rw(rwkv_mu_next[0])),
        'g2': rwkv_g2[0].astype(BF16),
        'w0': rwkv_w0[0].reshape(2, 1, RWKV_W), 'w2': padrows(rwkv_w2[0]).astype(BF16),
        'a0': rwkv_a0[0].reshape(2, 1, RWKV_W), 'a2': padrows(rwkv_a2[0]).astype(BF16),
        'kkk': row(rwkv_k_k[0]), 'ka': row(rwkv_k_a[0]), 'rk': row(rwkv_r_k[0]),
        'bd': _block_diag_ones(RWKV_W, RWKV_N),
        'lng': row(rwkv_ln_g[0]), 'lnb': row(rwkv_ln_b[0]), 'wout': ab_w_out[0].astype(BF16),
    }
    mods0 = mods_of(0)
    rope_mla = _rope_tables(l, lc, MLA_ROPE, _mla_place)
    (q, k, v, r, vv, kk, g, bonus, lw, ke, bb) = _ab_in(xall, mods0, p_ab, rope_mla, lc)
    att_c, att_l = _mla_attn(q, k, v, lc)
    yf, yb = _rwkv_scan(r, vv, kk, lw, ke, bb, lc)
    x1 = _ab_out(xall, att_c, att_l, yf, yb, bonus, g, mods0, p_ab, lc)
    x2 = _ffn(x1, mods0, _ffn_params(0, norm_ffn_g, ffn_w_up, ffn_conv_w, ffn_conv_b, ffn_w_down),
              ROW_BLOCK, (0, lc, ltot), seg_of, None)

    cw_in = cd_w_in[0]
    s1 = SSM_INNER
    s2 = s1 + SSM_INNER + 2 * SSM_G * SSM_N
    s3 = s2 + 2 * SSM_HEADS
    win_cd = jnp.concatenate([cw_in[:, :s2], padw(cw_in[:, s2:s3], LANES), cw_in[:, s3:]], axis=1).astype(BF16)
    a_neg = -jnp.exp(ssm_a_log[0])
    heads_to_lanes = np.zeros((LANES, SSM_INNER), np.float32)
    for hd in range(SSM_HEADS):
        heads_to_lanes[hd, hd * SSM_P:(hd + 1) * SSM_P] = 1.0
    p_cd = {
        'norm_g': row(norm_mix_g[1]), 'win': win_cd,
        'cw': ssm_conv_w[0], 'cb': row(ssm_conv_b[0]),
        'dtb': row(jnp.pad(ssm_dt_bias[0].reshape(-1), (0, LANES - 2 * SSM_HEADS))),
        'a_row': jnp.pad(a_neg, ((0, 0), (0, LANES - SSM_HEADS))).reshape(2, 1, LANES),
        'a_col': a_neg.reshape(2, SSM_HEADS, 1),
        'ex': jnp.asarray(heads_to_lanes, BF16),
        'dsk': row(jnp.repeat(ssm_d[0], SSM_P)),
        'ng': row(ssm_norm_g[0]), 'wout': cd_w_out[0].astype(BF16),
    }
    mods1 = mods_of(1)
    rope_swa = _rope_tables(l, lc, HEAD_DIM, _swa_place)
    z, xs, bcm, dt, sq, sk, sv = _cd_in(x2, mods1, p_cd, rope_swa, lc)
    dt2 = jnp.stack([jnp.pad(dt[:, :, dd * SSM_HEADS:(dd + 1) * SSM_HEADS],
                             ((0, 0), (0, 0), (0, LANES - SSM_HEADS))) for dd in range(2)])
    dtt2 = jnp.stack([jnp.swapaxes(dt[:, :, dd * SSM_HEADS:(dd + 1) * SSM_HEADS], 1, 2) for dd in range(2)])
    ys = _ssd(xs, bcm, dt2, dtt2, p_cd, lc)
    satt = _swa(sq, sk, sv, swa_sink[0], lc)
    x3 = _cd_out(x2, ys, z, satt, mods1, p_cd, lc)
    t1 = FFN_ROW_BLOCK_LATENT if l % FFN_ROW_BLOCK_LATENT == 0 else ROW_BLOCK
    return _ffn(x3, mods1, _ffn_params(1, norm_ffn_g, ffn_w_up, ffn_conv_w, ffn_conv_b, ffn_w_down),
                t1, (0, l), lambda i: 1, row(final_norm_g))
```

```python
import functools
import math

import jax
import jax.numpy as jnp
import numpy as np
from jax import lax
from jax.experimental import pallas as pl
from jax.experimental.pallas import tpu as pltpu

F32 = jnp.float32
BF16 = jnp.bfloat16

GRID_W = 64
ROPE_THETA = 10000.0
NORM_EPS = 1e-6
HEAD_DIM = 64
MLA_HEADS, MLA_NOPE, MLA_ROPE, MLA_V = 8, 64, 32, 64
MLA_Q_RANK, MLA_KV_RANK = 384, 256
RWKV_HEADS, RWKV_N = 8, 64
RWKV_W = RWKV_HEADS * RWKV_N
DECAY_LORA, ICLR_LORA, GATE_LORA = 64, 64, 128
GN_EPS = 64e-5
SSM_HEADS, SSM_P, SSM_G, SSM_N = 16, 64, 2, 128
SSM_INNER = SSM_HEADS * SSM_P
SSM_CONV = 5
SWA_HEADS, SWA_KV_HEADS, WINDOW = 8, 2, 128
D_FF = 2816
FFN_CONV = 3

LANES = 128
HALO = 16
ROW_BLOCK = 256
MLA_Q_BLOCKS = 2
RWKV_CHUNK = 64
RWKV_GROUP = 4
RWKV_PACK = 4
SSD_CHUNK = 128
SWA_BLOCK = 128
FFN_CHUNK = 256
FFN_DOWN_GROUP = 4
FFN_ROW_BLOCK_LATENT = 512
VMEM_LIMIT = 56 * 1024 * 1024
NEG_BIG = -1e30


def _cparams(sem):
    return pltpu.CompilerParams(dimension_semantics=sem, vmem_limit_bytes=VMEM_LIMIT)


def _sigmoid(x):
    return 1.0 / (1.0 + jnp.exp(-x))


def _silu(x):
    return x * _sigmoid(x)


def _rms(x, eps=NORM_EPS):
    return x * lax.rsqrt(jnp.mean(x * x, axis=-1, keepdims=True) + eps)


def _bdot(a, b):
    return jnp.dot(a.astype(BF16), b.astype(BF16), preferred_element_type=F32)


def _bdot_nt(a, b):
    return lax.dot_general(a.astype(BF16), b.astype(BF16), (((1,), (1,)), ((), ())),
                           preferred_element_type=F32)


def _bdot_tn(a, b):
    return lax.dot_general(a.astype(BF16), b.astype(BF16), (((0,), (0,)), ((), ())),
                           preferred_element_type=F32)


def _split3(x):
    hi = x.astype(BF16)
    r1 = x - hi.astype(F32)
    mid = r1.astype(BF16)
    lo = (r1 - mid.astype(F32)).astype(BF16)
    return hi, mid, lo


def _dot_exact_rhs(x, m):
    hi, mid, lo = _split3(x)
    return (jnp.dot(hi, m, preferred_element_type=F32) + jnp.dot(mid, m, preferred_element_type=F32)
            + jnp.dot(lo, m, preferred_element_type=F32))


def _group_sum(x, m):
    hi = x.astype(BF16)
    lo = (x - hi.astype(F32)).astype(BF16)
    return jnp.dot(hi, m, preferred_element_type=F32) + jnp.dot(lo, m, preferred_element_type=F32)


def _dot_exact_lhs(m, x):
    hi, mid, lo = _split3(x)
    return (jnp.dot(m, hi, preferred_element_type=F32) + jnp.dot(m, mid, preferred_element_type=F32)
            + jnp.dot(m, lo, preferred_element_type=F32))


def _dot_exact_rhs_nt(x, m):
    hi, mid, lo = _split3(x)
    dn = (((1,), (1,)), ((), ()))
    return (lax.dot_general(hi, m, dn, preferred_element_type=F32)
            + lax.dot_general(mid, m, dn, preferred_element_type=F32)
            + lax.dot_general(lo, m, dn, preferred_element_type=F32))


def _rope(t, cos, sin_a, sin_b, half):
    w = t.shape[-1]
    return t * cos + pltpu.roll(t, w - half, 1) * sin_a + pltpu.roll(t, half, 1) * sin_b


def _modnorm(x, g, shift, scale):
    return (_rms(x) * g) * (1.0 + scale) + shift


def _seg_flags(i, starts, ends):
    prev_ok = functools.reduce(jnp.logical_and, [i != s for s in starts])
    next_ok = functools.reduce(jnp.logical_and, [i != (e - 1) for e in ends])
    return prev_ok, next_ok


def _h_with_halo(x_ref, xp_ref, xn_ref, g, shift, scale, prev_ok, next_ok):
    h = _modnorm(x_ref[0], g, shift, scale)
    hp = _modnorm(xp_ref[0], g, shift, scale) * prev_ok.astype(F32)
    hn = _modnorm(xn_ref[0], g, shift, scale) * next_ok.astype(F32)
    return jnp.concatenate([hp.astype(BF16), h.astype(BF16), hn.astype(BF16)], axis=0)


def _shift_rows(x, delta, t):
    m = x.shape[0]
    return pltpu.roll(x, (m - delta) % m, 0)[HALO:HALO + t]


def _ada_kernel(c_ref, w_ref, b_ref, o_ref):
    o_ref[0] = _bdot(_silu(c_ref[...]), w_ref[0]) + b_ref[0]


def _ada_mods(cond, ada_w, ada_b):
    depth, d, n = ada_w.shape
    rows = cond.shape[0]
    tn = 1024
    return pl.pallas_call(
        _ada_kernel,
        grid=(depth, n // tn),
        in_specs=[pl.BlockSpec((rows, d), lambda l, j: (0, 0)),
                  pl.BlockSpec((1, d, tn), lambda l, j: (l, 0, j)),
                  pl.BlockSpec((1, 1, tn), lambda l, j: (l, 0, j))],
        out_specs=pl.BlockSpec((1, rows, tn), lambda l, j: (l, 0, j)),
        out_shape=jax.ShapeDtypeStruct((depth, rows, n), F32),
        compiler_params=_cparams(("parallel", "parallel")),
        name="ada_mods",
    )(cond, ada_w, ada_b.reshape(depth, 1, n))


def _row_specs(t, w, nrows8, off=0):
    per = t // HALO
    nh = nrows8
    main = pl.BlockSpec((1, t, w), lambda b, i: (b, i + off, 0))
    prev = pl.BlockSpec((1, HALO, w), lambda b, i: (b, jnp.maximum((i + off) * per - 1, 0), 0))
    nxt = pl.BlockSpec((1, HALO, w), lambda b, i: (b, jnp.minimum((i + off + 1) * per, nh - 1), 0))
    return main, prev, nxt


def _const_spec(shape):
    nd = len(shape)
    return pl.BlockSpec(shape, lambda *_: (0,) * nd)


def _resident_spec(shape):
    nd = len(shape)
    return pl.BlockSpec(shape, lambda *_: (0,) * nd, pipeline_mode=pl.Buffered(1))


def _ab_in_kernel(x_ref, xp_ref, xn_ref, mods_ref, g_ref, cos_ref, sa_ref, sb_ref, win_ref,
                  gq_ref, wq_ref, gkv_ref, wkk_ref, wvv_ref, mup_ref, mun_ref, g2_ref,
                  w0_ref, w2_ref, a0_ref, a2_ref, kkk_ref, ka_ref, rk_ref, bd_ref,
                  q_out, k_out, v_out, r_out, vv_out, kk_out, g_out, bonus_out,
                  lw_out, ke_out, b_out, *, starts, ends):
    i = pl.program_id(1)
    t = x_ref.shape[1]
    prev_ok, next_ok = _seg_flags(i, starts, ends)
    m = mods_ref[0, 0]
    h_all = _h_with_halo(x_ref, xp_ref, xn_ref, g_ref[...], m[0:1], m[1:2], prev_ok, next_ok)
    pin = jnp.dot(h_all, win_ref[...], preferred_element_type=F32)

    cos, sa, sb = cos_ref[...], sa_ref[...], sb_ref[...]
    main = pin[HALO:HALO + t]
    cq, ckv, kr = main[:, 0:384], main[:, 384:640], main[:, 640:768]
    qf = _bdot(_rms(cq) * gq_ref[...], wq_ref[...])
    kvn = (_rms(ckv) * gkv_ref[...]).astype(BF16)
    kf = jnp.dot(kvn, wkk_ref[...], preferred_element_type=F32)
    v_out[0] = jnp.dot(kvn, wvv_ref[...], preferred_element_type=F32).astype(BF16)
    krr = _rope(kr, cos, sa, sb, MLA_ROPE // 2)
    for h in range(MLA_HEADS):
        sl = slice(h * LANES, (h + 1) * LANES)
        q_out[0, h] = _rope(qf[:, sl], cos, sa, sb, MLA_ROPE // 2).astype(BF16)
        k_out[0, h] = (kf[:, sl] + krr).astype(BF16)

    rw = pin[:, 768:]
    x = rw[HALO:HALO + t]
    xp = _shift_rows(rw, -1, t)
    xn = _shift_rows(rw, 1, t)
    xs = x + mup_ref[...] * (xp - x) + mun_ref[...] * (xn - x)
    w = RWKV_W
    r, k, v = xs[:, 0:w], xs[:, w:2 * w], xs[:, 2 * w:3 * w]
    xw, xa, xg = xs[:, 3 * w:3 * w + 128], xs[:, 3 * w + 128:3 * w + 256], xs[:, 3 * w + 256:3 * w + 384]
    bd = bd_ref[...]
    g_out[0] = _bdot(_sigmoid(xg), g2_ref[...])
    kkr = k * kkk_ref[...]
    kk = kkr * lax.rsqrt(_group_sum(kkr * kkr, bd) + 1e-12)
    r_out[0] = r
    vv_out[0] = v
    kk_out[0] = kk
    bonus_out[0] = _group_sum(r * k * rk_ref[...], bd) * v
    tw = jnp.tanh(xw).astype(BF16)
    xab = xa.astype(BF16)
    for d in range(2):
        wl = w0_ref[d] + jnp.dot(tw, w2_ref[d], preferred_element_type=F32)
        lw_out[d, 0] = -math.exp(-0.5) * _sigmoid(wl)
        a = _sigmoid(a0_ref[d] + jnp.dot(xab, a2_ref[d], preferred_element_type=F32))
        ke_out[d, 0] = k * (1.0 + (a - 1.0) * ka_ref[...])
        b_out[d, 0] = kk * a


def _ab_in(x, mods, p, rope, lc):
    b, ltot, d = x.shape
    t = ROW_BLOCK
    nb, nbc = ltot // t, lc // t
    kern = functools.partial(_ab_in_kernel, starts=(0, nbc), ends=(nbc, nb))
    main, prev, nxt = _row_specs(t, d, ltot // HALO)
    tab = pl.BlockSpec((t, LANES), lambda bb, i: (i, 0))
    consts = [p['win'], p['gq'], p['wq'], p['gkv'], p['wkk'], p['wvv'], p['mup'], p['mun'], p['g2'],
              p['w0'], p['w2'], p['a0'], p['a2'], p['kkk'], p['ka'], p['rk'], p['bd']]
    in_specs = ([main, prev, nxt,
                 pl.BlockSpec((1, 1, 6, d), lambda bb, i: (bb, jnp.where(i >= nbc, 1, 0), 0, 0)),
                 _const_spec((1, d)), tab, tab, tab] + [_const_spec(c.shape) for c in consts])
    hq = pl.BlockSpec((1, MLA_HEADS, t, LANES), lambda bb, i: (bb, 0, i, 0))
    row = pl.BlockSpec((1, t, RWKV_W), lambda bb, i: (bb, i, 0))
    drow = pl.BlockSpec((2, 1, t, RWKV_W), lambda bb, i: (0, bb, i, 0))
    f_rows = jax.ShapeDtypeStruct((b, ltot, RWKV_W), F32)
    f_drows = jax.ShapeDtypeStruct((2, b, ltot, RWKV_W), F32)
    return pl.pallas_call(
        kern, grid=(b, nb), in_specs=in_specs,
        out_specs=[hq, hq, row, row, row, row, row, row, drow, drow, drow],
        out_shape=[jax.ShapeDtypeStruct((b, MLA_HEADS, ltot, LANES), BF16),
                   jax.ShapeDtypeStruct((b, MLA_HEADS, ltot, LANES), BF16),
                   jax.ShapeDtypeStruct((b, ltot, MLA_HEADS * MLA_V), BF16),
                   f_rows, f_rows, f_rows, f_rows, f_rows, f_drows, f_drows, f_drows],
        compiler_params=_cparams(("parallel", "parallel")),
        name="ab_in",
    )(x, x, x, mods, p['norm_g'], rope[0], rope[1], rope[2], *consts)


def _mla_attn_kernel(*refs, nq, exp2_scale):
    q_refs, (k_ref, v_ref, o_ref) = refs[:nq], refs[nq:]
    for hp in range(MLA_HEADS // 2):
        outs = []
        for h in (2 * hp, 2 * hp + 1):
            q = q_refs[0][0, h] if nq == 1 else jnp.concatenate([r[0, h] for r in q_refs], axis=0)
            s = lax.dot_general(q, k_ref[0, h], (((1,), (1,)), ((), ())), preferred_element_type=F32)
            mx = jnp.max(s, axis=-1, keepdims=True)
            e = jnp.exp2((s - mx) * exp2_scale)
            den = jnp.sum(e, axis=-1, keepdims=True)
            o2 = jnp.dot(e.astype(BF16), v_ref[0, :, hp * LANES:(hp + 1) * LANES],
                         preferred_element_type=F32)
            outs.append(o2 / den)
        lane = lax.broadcasted_iota(jnp.int32, outs[0].shape, 1)
        o_ref[0, :, hp * LANES:(hp + 1) * LANES] = jnp.where(lane < MLA_V, outs[0], outs[1])


def _mla_attn(q, k, v, lc):
    b, hh, ltot, _ = q.shape
    l = ltot - lc
    vw = hh * MLA_V
    exp2_scale = (MLA_NOPE + MLA_ROPE) ** -0.5 * math.log2(math.e)
    att_c = pl.pallas_call(
        functools.partial(_mla_attn_kernel, nq=1, exp2_scale=exp2_scale), grid=(b,),
        in_specs=[pl.BlockSpec((1, hh, lc, LANES), lambda bb: (bb, 0, 0, 0)),
                  pl.BlockSpec((1, hh, lc, LANES), lambda bb: (bb, 0, 0, 0)),
                  pl.BlockSpec((1, lc, vw), lambda bb: (bb, 0, 0))],
        out_specs=pl.BlockSpec((1, lc, vw), lambda bb: (bb, 0, 0)),
        out_shape=jax.ShapeDtypeStruct((b, lc, vw), F32),
        compiler_params=_cparams(("parallel",)),
        name="mla_attn_ctx",
    )(q, k, v)
    nq = MLA_Q_BLOCKS if l % (MLA_Q_BLOCKS * lc) == 0 else 1
    off = 1
    qspec = lambda j: pl.BlockSpec((1, hh, lc, LANES), lambda bb, i, j=j: (bb, 0, off + nq * i + j, 0))
    att_l = pl.pallas_call(
        functools.partial(_mla_attn_kernel, nq=nq, exp2_scale=exp2_scale), grid=(b, l // (nq * lc)),
        in_specs=[qspec(j) for j in range(nq)]
        + [pl.BlockSpec((1, hh, ltot, LANES), lambda bb, i: (bb, 0, 0, 0)),
           pl.BlockSpec((1, ltot, vw), lambda bb, i: (bb, 0, 0))],
        out_specs=pl.BlockSpec((1, nq * lc, vw), lambda bb, i: (bb, i, 0)),
        out_shape=jax.ShapeDtypeStruct((b, l, vw), F32),
        compiler_params=_cparams(("parallel", "arbitrary")),
        name="mla_attn_lat",
    )(*([q] * nq), k, v)
    return att_c, att_l


def _rwkv_scan_kernel(rf_ref, vf_ref, kkf_ref, lwf_ref, kef_ref, bf_ref,
                      rb_ref, vb_ref, kkb_ref, lwb_ref, keb_ref, bb_ref,
                      yf_ref, yb_ref, s_ref, *, nchunks):
    n = pl.program_id(1)
    c = RWKV_CHUNK
    rows = nchunks * c

    @pl.when(n == 0)
    def _():
        s_ref[...] = jnp.zeros_like(s_ref)

    rowb = lax.broadcasted_iota(jnp.int32, (rows, rows), 0)
    colb = lax.broadcasted_iota(jnp.int32, (rows, rows), 1)
    same = (rowb // c) == (colb // c)
    hpg = RWKV_PACK
    gw = hpg * RWKV_N
    ngrp = RWKV_HEADS // hpg
    row = lax.broadcasted_iota(jnp.int32, (c, gw), 0)
    col = lax.broadcasted_iota(jnp.int32, (c, gw), 1) % RWKV_N
    eye = (row == col).astype(F32)
    lane_head = lax.broadcasted_iota(jnp.int32, (c, gw), 1) // RWKV_N
    head_masks = [lane_head == h for h in range(hpg)]
    diag_blocks = (lax.broadcasted_iota(jnp.int32, (gw, gw), 0) // RWKV_N
                   == lax.broadcasted_iota(jnp.int32, (gw, gw), 1) // RWKV_N)

    def bdiag(x):
        return jnp.concatenate([jnp.where(mk, x, 0.0) for mk in head_masks], axis=0).astype(BF16)

    def bmm(x, w):
        return jnp.dot(x.astype(BF16), bdiag(w), preferred_element_type=F32)

    def bmm_nt(x, w):
        return lax.dot_general(x.astype(BF16), bdiag(w), (((1,), (1,)), ((), ())), preferred_element_type=F32)

    dirs = []
    for sgn, (r_ref, v_ref, kk_ref, lw_ref, ke_ref, b_ref) in (
            (1, (rf_ref, vf_ref, kkf_ref, lwf_ref, kef_ref, bf_ref)),
            (-1, (rb_ref, vb_ref, kkb_ref, lwb_ref, keb_ref, bb_ref))):
        inclb = jnp.logical_and(same, sgn * (rowb - colb) >= 0)
        lw = lw_ref[0, 0]
        cum = _dot_exact_lhs(inclb.astype(BF16), lw)
        tot = _dot_exact_lhs(same.astype(BF16), lw)
        r, v, kk, ke, bb = r_ref[0], v_ref[0], kk_ref[0], ke_ref[0, 0], b_ref[0, 0]
        e_nc = jnp.exp(-cum)
        e_end = jnp.exp(tot - cum)
        dirs.append(dict(
            incl=sgn * (row - col) >= 0,
            strict=sgn * (row - col) > 0,
            v=v, rt=r * jnp.exp(cum), kt=kk * jnp.exp(cum - lw), bh=bb * e_nc, kh=ke * e_nc,
            bc=bb * e_end, kc=ke * e_end, gam=jnp.exp(tot)))

    chains = [(d, g, q) for g in range(nchunks) for q in range(ngrp) for d in range(2)]
    blk = lambda name, d, g, q: dirs[d][name][g * c:(g + 1) * c, q * gw:(q + 1) * gw]
    ms, ns, pbs, pks = [], [], [], []
    for d, g, q in chains:
        a = jnp.concatenate([blk('kt', d, g, q), blk('rt', d, g, q)], axis=0)
        mpb = bmm_nt(a, blk('bh', d, g, q))
        nk = bmm_nt(a, blk('kh', d, g, q))
        ms.append(jnp.where(dirs[d]['strict'], mpb[0:c], 0.0))
        pbs.append(jnp.where(dirs[d]['incl'], mpb[c:2 * c], 0.0))
        ns.append(jnp.where(dirs[d]['strict'], nk[0:c], 0.0))
        pks.append(jnp.where(dirs[d]['incl'], nk[c:2 * c], 0.0))
    xinv = [eye - m for m in ms]
    pw = [bmm(m, m) for m in ms]
    steps = int(math.log2(c)) - 1
    for it in range(steps):
        if it < steps - 1:
            xp = [bmm(jnp.concatenate([x, p], axis=0), p) for x, p in zip(xinv, pw)]
            xinv = [x + z[0:c] for x, z in zip(xinv, xp)]
            pw = [z[c:2 * c] for z in xp]
        else:
            xinv = [x + bmm(x, p) for x, p in zip(xinv, pw)]
    npk = [bmm(jnp.concatenate([nn, pk], axis=0), blk('v', *ch)) for ch, nn, pk in zip(chains, ns, pks)]
    ku = [-bmm(x, blk('kt', *ch)) for ch, x in zip(chains, xinv)]
    u0 = [-bmm(x, z[0:c]) for x, z in zip(xinv, npk)]
    ry = [(blk('rt', *ch) + bmm(pb, k_)).astype(BF16) for ch, pb, k_ in zip(chains, pbs, ku)]
    y0 = [bmm(pb, u_) + z[c:2 * c] for pb, u_, z in zip(pbs, u0, npk)]
    tmw, sadd = {}, {}
    for ch, k_, u_ in zip(chains, ku, u0):
        lhs = jnp.concatenate([jnp.concatenate([k_, u_], axis=1),
                               jnp.concatenate([jnp.zeros((c, gw), F32), blk('v', *ch)], axis=1)],
                              axis=0)
        rhs = jnp.concatenate([blk('bc', *ch), blk('kc', *ch)], axis=0)
        full = _bdot_tn(lhs, rhs)
        tmw[ch] = jnp.where(diag_blocks, full[0:gw], 0.0).astype(BF16)
        low = full[gw:2 * gw]
        sadd[ch] = functools.reduce(
            jnp.add, [jnp.where(head_masks[h], low[h * RWKV_N:(h + 1) * RWKV_N], 0.0) for h in range(hpg)])
    ry = dict(zip(chains, ry))
    y0 = dict(zip(chains, y0))

    state = {(d, q): s_ref[d, :, q * gw:(q + 1) * gw] for d in range(2) for q in range(ngrp)}
    youts = [[None] * nchunks for _ in range(2)]
    for step in range(nchunks):
        for d in range(2):
            g = step if d == 0 else nchunks - 1 - step
            ys = []
            for q in range(ngrp):
                ch = (d, g, q)
                s0 = state[(d, q)]
                ys.append(y0[ch] + lax.dot_general(ry[ch], bdiag(s0), (((1,), (1,)), ((), ())),
                                                   preferred_element_type=F32))
                state[(d, q)] = (s0 * dirs[d]['gam'][g * c:g * c + 1, q * gw:(q + 1) * gw]
                                 + jnp.dot(s0.astype(BF16), tmw[ch], preferred_element_type=F32) + sadd[ch])
            youts[d][g] = jnp.concatenate(ys, axis=1)
    yf_ref[0] = jnp.concatenate(youts[0], axis=0)
    yb_ref[0] = jnp.concatenate(youts[1], axis=0)
    for d in range(2):
        for q in range(ngrp):
            s_ref[d, :, q * gw:(q + 1) * gw] = state[(d, q)]


def _scan_chunk_index(n, nc_ctx, nc, d):
    bwd = jnp.where(n < nc_ctx, nc_ctx - 1 - n, nc - 1 - (n - nc_ctx))
    return jnp.where(d == 0, n, bwd)


def _rwkv_scan(r, v, kk, lw, ke, bb, lc):
    b, ltot, w = r.shape
    nch = RWKV_GROUP
    rows = nch * RWKV_CHUNK
    nc, ncc = ltot // rows, lc // rows
    specs = []
    for d in range(2):
        cidx = functools.partial(_scan_chunk_index, nc_ctx=ncc, nc=nc, d=d)
        shared = pl.BlockSpec((1, rows, w), lambda bi, n, cidx=cidx: (bi, cidx(n), 0))
        perdir = pl.BlockSpec((1, 1, rows, w), lambda bi, n, cidx=cidx, d=d: (d, bi, cidx(n), 0))
        specs.append((shared, perdir))
    kern = functools.partial(_rwkv_scan_kernel, nchunks=nch)
    out = jax.ShapeDtypeStruct((b, ltot, w), F32)
    return pl.pallas_call(
        kern, grid=(b, nc),
        in_specs=[specs[0][0]] * 3 + [specs[0][1]] * 3 + [specs[1][0]] * 3 + [specs[1][1]] * 3,
        out_specs=[specs[0][0], specs[1][0]],
        out_shape=[out, out],
        scratch_shapes=[pltpu.VMEM((2, RWKV_N, w), F32)],
        compiler_params=_cparams(("parallel", "arbitrary")),
        name="rwkv_scan",
    )(r, v, kk, lw, ke, bb, r, v, kk, lw, ke, bb)


def _ab_out_kernel(x_ref, attc_ref, attl_ref, yf_ref, yb_ref, bonus_ref, g_ref, mods_ref, lng_ref, lnb_ref,
                   bd_ref, wout_ref, o_ref, *, nbc):
    att = jnp.where(pl.program_id(1) < nbc, attc_ref[0], attl_ref[0])
    y = yf_ref[0] + yb_ref[0]
    bd = bd_ref[...]
    mu = _group_sum(y, bd) * (1.0 / RWKV_N)
    dl = y - mu
    var = _group_sum(dl * dl, bd) * (1.0 / RWKV_N)
    yn = dl * lax.rsqrt(var + GN_EPS) * lng_ref[...] + lnb_ref[...]
    rwo = (yn + bonus_ref[0]) * g_ref[0]
    feat = jnp.concatenate([att.astype(BF16), rwo.astype(BF16)], axis=1)
    o = jnp.dot(feat, wout_ref[...], preferred_element_type=F32)
    o_ref[0] = x_ref[0] + mods_ref[0, 0][2:3] * o


def _ab_out(x, att_c, att_l, yf, yb, bonus, g, mods, p, lc):
    b, ltot, d = x.shape
    t = ROW_BLOCK
    nbc = lc // t
    row = lambda w: pl.BlockSpec((1, t, w), lambda bb, i: (bb, i, 0))
    aw = att_c.shape[-1]
    consts = [p['lng'], p['lnb'], p['bd'], p['wout']]
    return pl.pallas_call(
        functools.partial(_ab_out_kernel, nbc=nbc), grid=(b, ltot // t),
        in_specs=[row(d),
                  pl.BlockSpec((1, t, aw), lambda bb, i: (bb, jnp.minimum(i, nbc - 1), 0)),
                  pl.BlockSpec((1, t, aw), lambda bb, i: (bb, jnp.maximum(i - nbc, 0), 0)),
                  row(RWKV_W), row(RWKV_W), row(RWKV_W), row(RWKV_W),
                  pl.BlockSpec((1, 1, 6, d), lambda bb, i: (bb, jnp.where(i >= nbc, 1, 0), 0, 0))]
        + [_const_spec(c.shape) for c in consts],
        out_specs=row(d),
        out_shape=jax.ShapeDtypeStruct((b, ltot, d), F32),
        compiler_params=_cparams(("parallel", "parallel")),
        name="ab_out",
    )(x, att_c, att_l, yf, yb, bonus, g, mods, *consts)


def _ffn_kernel(x_ref, xp_ref, xn_ref, mods_ref, g_ref, wg_ref, wv_ref, cwg_ref, cwv_ref,
                cbg_ref, cbv_ref, wd_ref, fg_ref, o_ref, act_ref, *, starts, ends, final_norm):
    i = pl.program_id(1)
    t = x_ref.shape[1]
    prev_ok, next_ok = _seg_flags(i, starts, ends)
    m = mods_ref[0, 0]
    h_all = _h_with_halo(x_ref, xp_ref, xn_ref, g_ref[...], m[3:4], m[4:5], prev_ok, next_ok)
    nch = wg_ref.shape[0]
    fc = wg_ref.shape[2]

    def conv(u, cw, cb):
        return (cw[0:1] * _shift_rows(u, -1, t) + cw[1:2] * u[HALO:HALO + t]
                + cw[2:3] * _shift_rows(u, 1, t) + cb)

    def up(c):
        return (jnp.dot(h_all, wg_ref[c], preferred_element_type=F32),
                jnp.dot(h_all, wv_ref[c], preferred_element_type=F32))

    nxt = up(0)
    acc = None
    lo = 0
    for c in range(nch):
        ug, uv = nxt
        if c + 1 < nch:
            nxt = up(c + 1)
        act = _silu(conv(ug, cwg_ref[c], cbg_ref[c])) * conv(uv, cwv_ref[c], cbv_ref[c])
        act_ref[:, c * fc:(c + 1) * fc] = act.astype(BF16)
        if (c + 1 - lo) == FFN_DOWN_GROUP or c + 1 == nch:
            part = jnp.dot(act_ref[:, lo * fc:(c + 1) * fc], wd_ref[lo * fc:(c + 1) * fc, :],
                           preferred_element_type=F32)
            acc = part if acc is None else acc + part
            lo = c + 1
    out = x_ref[0] + m[5:6] * acc
    if final_norm:
        out = _rms(out) * fg_ref[...]
    o_ref[0] = out


def _ffn(x, mods, p, t, seg_rows, seg_index, final_g):
    b, rows, d = x.shape
    bounds = [r // t for r in seg_rows]
    starts, ends = tuple(bounds[:-1]), tuple(bounds[1:])
    kern = functools.partial(_ffn_kernel, starts=starts, ends=ends, final_norm=final_g is not None)
    main, prev, nxt = _row_specs(t, d, rows // HALO)
    fg = final_g if final_g is not None else jnp.ones((1, d), F32)
    consts = [p['wg'], p['wv'], p['cwg'], p['cwv'], p['cbg'], p['cbv'], p['wd'], fg]
    return pl.pallas_call(
        kern, grid=(b, rows // t),
        in_specs=[main, prev, nxt,
                  pl.BlockSpec((1, 1, 6, d), lambda bb, i: (bb, seg_index(i), 0, 0)),
                  _const_spec((1, d))] + [_resident_spec(c.shape) for c in consts],
        out_specs=pl.BlockSpec((1, t, d), lambda bb, i: (bb, i, 0)),
        out_shape=jax.ShapeDtypeStruct((b, rows, d), F32),
        scratch_shapes=[pltpu.VMEM((t, D_FF), BF16)],
        compiler_params=_cparams(("parallel", "parallel")),
        name="conv_ffn",
    )(x, x, x, mods, p['norm_g'], *consts)


def _cd_in_kernel(x_ref, xp_ref, xn_ref, mods_ref, g_ref, cos_ref, sa_ref, sb_ref, win_ref,
                  cw_ref, cb_ref, dtb_ref,
                  z_out, xs_out, bc_out, dt_out, q_out, k_out, v_out, *, starts, ends):
    i = pl.program_id(1)
    t = x_ref.shape[1]
    prev_ok, next_ok = _seg_flags(i, starts, ends)
    m = mods_ref[0, 0]
    h_all = _h_with_halo(x_ref, xp_ref, xn_ref, g_ref[...], m[0:1], m[1:2], prev_ok, next_ok)
    pin = jnp.dot(h_all, win_ref[...], preferred_element_type=F32)
    main = pin[HALO:HALO + t]
    z_out[0] = main[:, 0:1024]
    xbc = pin[:, 1024:2560]
    cw = cw_ref[...]
    conv = cb_ref[...] + cw[2:3] * xbc[HALO:HALO + t]
    for j in (0, 1, 3, 4):
        conv = conv + cw[j:j + 1] * _shift_rows(xbc, j - SSM_CONV // 2, t)
    xc = _silu(conv)
    xs_out[0] = xc[:, 0:SSM_INNER]
    bc_out[0] = xc[:, SSM_INNER:]
    raw = main[:, 2560:2688] + dtb_ref[...]
    dt_out[0] = jnp.maximum(raw, 0.0) + jnp.log(1.0 + jnp.exp(-jnp.abs(raw)))
    cos, sa, sb = cos_ref[...], sa_ref[...], sb_ref[...]
    q = main[:, 2688:3200]
    tile = lambda a: jnp.concatenate([a] * (q.shape[1] // LANES), axis=1)
    q_out[0] = _rope(q, tile(cos), tile(sa), tile(sb), HEAD_DIM // 2).astype(BF16)
    k_out[0] = _rope(main[:, 3200:3328], cos, sa, sb, HEAD_DIM // 2).astype(BF16)
    v_out[0] = main[:, 3328:3456].astype(BF16)


def _cd_in(x, mods, p, rope, lc):
    b, ltot, d = x.shape
    t = ROW_BLOCK
    nb, nbc = ltot // t, lc // t
    kern = functools.partial(_cd_in_kernel, starts=(0, nbc), ends=(nbc, nb))
    main, prev, nxt = _row_specs(t, d, ltot // HALO)
    tab = pl.BlockSpec((t, LANES), lambda bb, i: (i, 0))
    consts = [p['win'], p['cw'], p['cb'], p['dtb']]
    row = lambda w: pl.BlockSpec((1, t, w), lambda bb, i: (bb, i, 0))
    sds = lambda w, dt: jax.ShapeDtypeStruct((b, ltot, w), dt)
    kvw = SWA_KV_HEADS * HEAD_DIM
    return pl.pallas_call(
        kern, grid=(b, nb),
        in_specs=[main, prev, nxt,
                  pl.BlockSpec((1, 1, 6, d), lambda bb, i: (bb, jnp.where(i >= nbc, 1, 0), 0, 0)),
                  _const_spec((1, d)), tab, tab, tab] + [_const_spec(c.shape) for c in consts],
        out_specs=[row(SSM_INNER), row(SSM_INNER), row(2 * SSM_G * SSM_N), row(LANES),
                   row(SWA_HEADS * HEAD_DIM), row(kvw), row(kvw)],
        out_shape=[sds(SSM_INNER, F32), sds(SSM_INNER, F32), sds(2 * SSM_G * SSM_N, F32), sds(LANES, F32),
                   sds(SWA_HEADS * HEAD_DIM, BF16), sds(kvw, BF16), sds(kvw, BF16)],
        compiler_params=_cparams(("parallel", "parallel")),
        name="cd_in",
    )(x, x, x, mods, p['norm_g'], rope[0], rope[1], rope[2], *consts)


def _ssd_kernel(xsf_ref, bcf_ref, dtf_ref, dttf_ref, xsb_ref, bcb_ref, dtb_ref, dttb_ref,
                a_ref, at_ref, ex_ref, dsk_ref, yf_ref, yb_ref, h_ref):
    n = pl.program_id(1)
    q = SSD_CHUNK
    hg = SSM_HEADS // SSM_G
    gw = hg * SSM_P

    @pl.when(n == 0)
    def _():
        h_ref[...] = jnp.zeros_like(h_ref)

    row = lax.broadcasted_iota(jnp.int32, (q, q), 0)
    col = lax.broadcasted_iota(jnp.int32, (q, q), 1)
    ex = ex_ref[...]
    dirs = []
    for d, (xs_ref, bc_ref, dt_ref, dtt_ref, y_ref) in enumerate(
            ((xsf_ref, bcf_ref, dtf_ref, dttf_ref, yf_ref), (xsb_ref, bcb_ref, dtb_ref, dttb_ref, yb_ref))):
        sgn = 1 if d == 0 else -1
        incl = sgn * (row - col) >= 0
        inclb = incl.astype(BF16)
        dt_c = dt_ref[0, 0]
        dt_r = dtt_ref[0, 0]
        acum_c = _dot_exact_lhs(inclb, dt_c * a_ref[d])
        acum_r = _dot_exact_rhs_nt(dt_r * at_ref[d], inclb)
        end_c = acum_c[q - 1:q] if d == 0 else acum_c[0:1]
        grow = _group_sum(jnp.exp(acum_c), ex)
        tail = _group_sum(jnp.exp(end_c - acum_c) * dt_c, ex)
        dirs.append(dict(incl=incl, dt_r=dt_r, acum_c=acum_c, acum_r=acum_r, grow=grow, tail=tail,
                         chunk_decay=grow[q - 1:q] if d == 0 else grow[0:1],
                         xs=xs_ref[0], bcm=bc_ref[0], y_ref=y_ref))

    for g in range(SSM_G):
        gs = slice(g * gw, (g + 1) * gw)
        for d in range(2):
            p = dirs[d]
            bg = p['bcm'][:, g * SSM_N:(g + 1) * SSM_N].astype(BF16)
            cg = p['bcm'][:, (SSM_G + g) * SSM_N:(SSM_G + g + 1) * SSM_N].astype(BF16)
            cb = lax.dot_general(cg, bg, (((1,), (1,)), ((), ())), preferred_element_type=F32)
            xg = p['xs'][:, gs]
            yd = []
            for hh in range(hg):
                hd = g * hg + hh
                seg = jnp.where(p['incl'], p['acum_c'][:, hd:hd + 1] - p['acum_r'][hd:hd + 1, :], NEG_BIG)
                wgt = cb * jnp.exp(seg) * p['dt_r'][hd:hd + 1, :]
                yd.append(_bdot(wgt, xg[:, hh * SSM_P:(hh + 1) * SSM_P]))
            h_in = h_ref[d, g]
            y = jnp.concatenate(yd, axis=1) + (
                jnp.dot(cg, h_in.astype(BF16), preferred_element_type=F32) * p['grow'][:, gs])
            if d == 0:
                y = y + dsk_ref[:, gs] * xg
            p['y_ref'][0, :, gs] = y
            h_ref[d, g] = h_in * p['chunk_decay'][:, gs] + _bdot_tn(bg, xg * p['tail'][:, gs])


def _ssd(xs, bcm, dt2, dtt2, p, lc):
    b, ltot, inner = xs.shape
    q = SSD_CHUNK
    nc, ncc = ltot // q, lc // q
    hg = SSM_HEADS // SSM_G
    in_specs, out_specs = [], []
    for d in range(2):
        cidx = functools.partial(_scan_chunk_index, nc_ctx=ncc, nc=nc, d=d)
        in_specs += [pl.BlockSpec((1, q, inner), lambda bi, n, cidx=cidx: (bi, cidx(n), 0)),
                     pl.BlockSpec((1, q, 2 * SSM_G * SSM_N), lambda bi, n, cidx=cidx: (bi, cidx(n), 0)),
                     pl.BlockSpec((1, 1, q, LANES), lambda bi, n, cidx=cidx, d=d: (d, bi, cidx(n), 0)),
                     pl.BlockSpec((1, 1, SSM_HEADS, q), lambda bi, n, cidx=cidx, d=d: (d, bi, 0, cidx(n)))]
        out_specs.append(pl.BlockSpec((1, q, inner), lambda bi, n, cidx=cidx: (bi, cidx(n), 0)))
    consts = [p['a_row'], p['a_col'], p['ex'], p['dsk']]
    out = jax.ShapeDtypeStruct((b, ltot, inner), F32)
    return pl.pallas_call(
        _ssd_kernel, grid=(b, nc),
        in_specs=in_specs + [_const_spec(c.shape) for c in consts],
        out_specs=out_specs, out_shape=[out, out],
        scratch_shapes=[pltpu.VMEM((2, SSM_G, SSM_N, hg * SSM_P), F32)],
        compiler_params=_cparams(("parallel", "arbitrary")),
        name="ssd_scan",
    )(xs, bcm, dt2, dtt2, xs, bcm, dt2, dtt2, *consts)


def _swa_kernel(sink_ref, bias_ref, q_ref, kc_ref, kp_ref, k0_ref, kn_ref, vc_ref, vp_ref, v0_ref, vn_ref,
                o_ref, *, scale):
    t = q_ref.shape[1]
    hg = SWA_HEADS // SWA_KV_HEADS
    kall = jnp.concatenate([kc_ref[0], kp_ref[0], k0_ref[0], kn_ref[0]], axis=0)
    vall = jnp.concatenate([vc_ref[0], vp_ref[0], v0_ref[0], vn_ref[0]], axis=0)
    bias = bias_ref[0]
    rowh = lax.broadcasted_iota(jnp.int32, (hg * t, 1), 0) // t
    qb = q_ref[0]
    for g in range(SWA_KV_HEADS):
        qg = jnp.concatenate([qb[:, (g * hg + j) * HEAD_DIM:(g * hg + j + 1) * HEAD_DIM] for j in range(hg)],
                             axis=0)
        kg = kall[:, g * HEAD_DIM:(g + 1) * HEAD_DIM]
        vg = vall[:, g * HEAD_DIM:(g + 1) * HEAD_DIM]
        s = lax.dot_general(qg, kg, (((1,), (1,)), ((), ())), preferred_element_type=F32) * scale + bias
        sink = jnp.zeros((hg * t, 1), F32)
        for j in range(hg):
            sink = jnp.where(rowh == j, sink_ref[g * hg + j], sink)
        mx = jnp.maximum(jnp.max(s, axis=-1, keepdims=True), sink)
        e = jnp.exp(s - mx)
        den = jnp.sum(e, axis=-1, keepdims=True) + jnp.exp(sink - mx)
        og = jnp.dot(e.astype(BF16), vg, preferred_element_type=F32) / den
        for j in range(hg):
            hd = g * hg + j
            o_ref[0, :, hd * HEAD_DIM:(hd + 1) * HEAD_DIM] = og[j * t:(j + 1) * t]


def _swa(q, k, v, sink, lc):
    b, ltot, qw = q.shape
    t = SWA_BLOCK
    nb = (ltot - lc) // t
    off = lc // t
    kvw = k.shape[-1]
    kern = functools.partial(_swa_kernel, scale=HEAD_DIM ** -0.5)
    hg = SWA_HEADS // SWA_KV_HEADS
    rr = (np.arange(hg * t) % t)[:, None]
    cc = np.arange(lc + 3 * t)[None, :] - lc
    variants = []
    for has_prev in (False, True):
        for has_next in (False, True):
            lo = rr if has_prev else t
            hi = rr + 2 * t if has_next else 2 * t - 1
            ok = np.broadcast_to((cc < 0) | ((cc >= lo) & (cc <= hi)), (hg * t, lc + 3 * t))
            variants.append(np.where(ok, 0.0, NEG_BIG).astype(np.float32))
    bias = jnp.asarray(np.stack(variants))
    bias_spec = pl.BlockSpec((1, hg * t, lc + 3 * t),
                             lambda bb, i: (2 * (i > 0).astype(jnp.int32) + (i < nb - 1).astype(jnp.int32), 0, 0))
    ctx = pl.BlockSpec((1, lc, kvw), lambda bb, i: (bb, 0, 0))
    prev = pl.BlockSpec((1, t, kvw), lambda bb, i: (bb, off + jnp.maximum(i - 1, 0), 0))
    cur = pl.BlockSpec((1, t, kvw), lambda bb, i: (bb, off + i, 0))
    nxt = pl.BlockSpec((1, t, kvw), lambda bb, i: (bb, off + jnp.minimum(i + 1, nb - 1), 0))
    return pl.pallas_call(
        kern, grid=(b, nb),
        in_specs=[pl.BlockSpec(memory_space=pltpu.SMEM), bias_spec,
                  pl.BlockSpec((1, t, qw), lambda bb, i: (bb, off + i, 0)),
                  ctx, prev, cur, nxt, ctx, prev, cur, nxt],
        out_specs=pl.BlockSpec((1, t, qw), lambda bb, i: (bb, i, 0)),
        out_shape=jax.ShapeDtypeStruct((b, ltot - lc, qw), F32),
        compiler_params=_cparams(("parallel", "parallel")),
        name="swa_attn",
    )(sink, bias, q, k, k, k, k, v, v, v, v)


def _cd_out_kernel(x_ref, yf_ref, yb_ref, z_ref, att_ref, mods_ref, ng_ref, wout_ref, o_ref):
    u = (yf_ref[0] + yb_ref[0]) * _silu(z_ref[0])
    gw = SSM_INNER // SSM_G
    un = jnp.concatenate([_rms(u[:, g * gw:(g + 1) * gw]) for g in range(SSM_G)], axis=1) * ng_ref[...]
    feat = jnp.concatenate([un.astype(BF16), att_ref[0].astype(BF16)], axis=1)
    o = jnp.dot(feat, wout_ref[...], preferred_element_type=F32)
    o_ref[0] = x_ref[0] + mods_ref[0, 0][2:3] * o


def _cd_out(x, yf, yb, z, att, mods, p, lc):
    b, ltot, d = x.shape
    t = ROW_BLOCK
    off = lc // t
    nbl = (ltot - lc) // t
    full = lambda w: pl.BlockSpec((1, t, w), lambda bb, i: (bb, i + off, 0))
    return pl.pallas_call(
        _cd_out_kernel, grid=(b, nbl),
        in_specs=[full(d), full(SSM_INNER), full(SSM_INNER),
                  full(SSM_INNER), pl.BlockSpec((1, t, att.shape[-1]), lambda bb, i: (bb, i, 0)),
                  pl.BlockSpec((1, 1, 6, d), lambda bb, i: (bb, 1, 0, 0)),
                  _const_spec(p['ng'].shape), _const_spec(p['wout'].shape)],
        out_specs=pl.BlockSpec((1, t, d), lambda bb, i: (bb, i, 0)),
        out_shape=jax.ShapeDtypeStruct((b, ltot - lc, d), F32),
        compiler_params=_cparams(("parallel", "parallel")),
        name="cd_out",
    )(x, yf, yb, z, att, mods, p['ng'], p['wout'])


def _pad_cols(w, width):
    return jnp.pad(w, ((0, 0), (0, width - w.shape[1])))


def _rope_tables(n_tok, lc, rot_dim, place):
    rows = n_tok // GRID_W
    rp, cp = jnp.meshgrid(jnp.arange(rows, dtype=F32), jnp.arange(GRID_W, dtype=F32), indexing='ij')
    n_freq = rot_dim // 4
    inv_freq = ROPE_THETA ** (-jnp.arange(n_freq, dtype=F32) / n_freq)
    ang = jnp.concatenate([rp.reshape(-1, 1) * inv_freq, cp.reshape(-1, 1) * inv_freq], axis=-1)
    cos = jnp.concatenate([jnp.ones((lc, rot_dim // 2), F32), jnp.cos(ang)], axis=0)
    sin = jnp.concatenate([jnp.zeros((lc, rot_dim // 2), F32), jnp.sin(ang)], axis=0)
    zero = jnp.zeros_like(sin)
    return place(cos, cos, 1.0), place(-sin, zero, 0.0), place(zero, sin, 0.0)


def _mla_place(first, second, fill):
    n = first.shape[0]
    return jnp.concatenate([jnp.full((n, MLA_NOPE), fill, F32), first, second,
                            jnp.zeros((n, LANES - MLA_NOPE - MLA_ROPE), F32)], axis=1)


def _swa_place(first, second, fill):
    return jnp.concatenate([first, second] * (LANES // HEAD_DIM), axis=1)


def _block_diag_ones(n, group):
    idx = np.arange(n) // group
    return jnp.asarray(idx[:, None] == idx[None, :], BF16)


def _ffn_params(i, norm_ffn_g, ffn_w_up, ffn_conv_w, ffn_conv_b, ffn_w_down):
    d = ffn_w_up.shape[1]
    nch = D_FF // FFN_CHUNK
    chunks = lambda w: w.reshape(w.shape[0], nch, FFN_CHUNK).transpose(1, 0, 2)
    wup = ffn_w_up[i].astype(BF16)
    return {
        'norm_g': norm_ffn_g[i].reshape(1, d),
        'wg': chunks(wup[:, :D_FF]), 'wv': chunks(wup[:, D_FF:]),
        'cwg': chunks(ffn_conv_w[i][:, :D_FF]), 'cwv': chunks(ffn_conv_w[i][:, D_FF:]),
        'cbg': chunks(ffn_conv_b[i][None, :D_FF]), 'cbv': chunks(ffn_conv_b[i][None, D_FF:]),
        'wd': ffn_w_down[i].astype(BF16),
    }


def kernel(x, c, ctx, c_ctx, ada_w, ada_b, norm_mix_g, norm_ffn_g, ffn_w_up, ffn_conv_w, ffn_conv_b, ffn_w_down, final_norm_g, ab_w_in, ab_w_out, mla_q_norm_g, mla_w_q_up, mla_kv_norm_g, mla_w_kv_up, rwkv_mu_prev, rwkv_mu_next, rwkv_w0, rwkv_w2, rwkv_a0, rwkv_a2, rwkv_g2, rwkv_k_k, rwkv_k_a, rwkv_r_k, rwkv_ln_g, rwkv_ln_b, cd_w_in, cd_w_out, ssm_conv_w, ssm_conv_b, ssm_dt_bias, ssm_a_log, ssm_d, ssm_norm_g, swa_sink):
    b, l, d = x.shape
    lc = ctx.shape[1]
    assert ada_w.shape[0] == 2 and lc % ROW_BLOCK == 0 and l % ROW_BLOCK == 0
    ltot = lc + l
    xall = jnp.concatenate([ctx, x], axis=1)

    nrow = -(-(b + 1) // 8) * 8
    cond = jnp.zeros((nrow, d), F32).at[:b].set(c).at[b].set(c_ctx)
    ada = _ada_mods(cond, ada_w, ada_b)
    def mods_of(i):
        lat = ada[i, :b].reshape(b, 1, 6, d)
        cx = jnp.broadcast_to(ada[i, b].reshape(1, 1, 6, d), (b, 1, 6, d))
        return jnp.concatenate([cx, lat], axis=1)

    row = lambda v: v.reshape(1, -1)
    nbc, nb = lc // ROW_BLOCK, ltot // ROW_BLOCK
    seg_of = lambda i: jnp.where(i >= nbc, 1, 0)

    w_in = ab_w_in[0]
    padw = lambda w, n: jnp.pad(w, ((0, 0), (0, n - w.shape[1])))
    o1, o2, o3 = MLA_Q_RANK, MLA_Q_RANK + MLA_KV_RANK, MLA_Q_RANK + MLA_KV_RANK + MLA_ROPE
    w_kr = jnp.pad(w_in[:, o2:o3], ((0, 0), (MLA_NOPE, LANES - MLA_NOPE - MLA_ROPE)))
    rw = w_in[:, o3:]
    rw_sizes = (3 * RWKV_W, DECAY_LORA, ICLR_LORA, GATE_LORA)
    def pad_rw(v):
        a, bq, cq_, dq = jnp.split(v, np.cumsum(rw_sizes)[:-1].tolist(), axis=-1)
        z = jnp.zeros(v.shape[:-1] + (LANES - DECAY_LORA,), v.dtype)
        return jnp.concatenate([a, bq, z, cq_, z, dq], axis=-1)
    win_ab = jnp.concatenate([w_in[:, :o2], w_kr, pad_rw(rw)], axis=1).astype(BF16)
    qd = MLA_NOPE + MLA_ROPE
    wq = jnp.pad(mla_w_q_up[0].reshape(MLA_Q_RANK, MLA_HEADS, qd), ((0, 0), (0, 0), (0, LANES - qd)))
    wkv = mla_w_kv_up[0].reshape(MLA_KV_RANK, MLA_HEADS, MLA_NOPE + MLA_V)
    wkk = jnp.pad(wkv[:, :, :MLA_NOPE], ((0, 0), (0, 0), (0, LANES - MLA_NOPE)))
    padrows = lambda w: jnp.pad(w, ((0, 0), (0, LANES - w.shape[1]), (0, 0)))
    p_ab = {
        'norm_g': row(norm_mix_g[0]), 'win': win_ab,
        'gq': row(mla_q_norm_g[0]), 'wq': wq.reshape(MLA_Q_RANK, MLA_HEADS * LANES).astype(BF16),
        'gkv': row(mla_kv_norm_g[0]), 'wkk': wkk.reshape(MLA_KV_RANK, MLA_HEADS * LANES).astype(BF16),
        'wvv': wkv[:, :, MLA_NOPE:].reshape(MLA_KV_RANK, MLA_HEADS * MLA_V).astype(BF16),
        'mup': row(pad_rw(rwkv_mu_prev[0])), 'mun': row(pad_rw(rwkv_mu_next[0])),
        'g2': rwkv_g2[0].astype(BF16),
        'w0': rwkv_w0[0].reshape(2, 1, RWKV_W), 'w2': padrows(rwkv_w2[0]).astype(BF16),
        'a0': rwkv_a0[0].reshape(2, 1, RWKV_W), 'a2': padrows(rwkv_a2[0]).astype(BF16),
        'kkk': row(rwkv_k_k[0]), 'ka': row(rwkv_k_a[0]), 'rk': row(rwkv_r_k[0]),
        'bd': _block_diag_ones(RWKV_W, RWKV_N),
        'lng': row(rwkv_ln_g[0]), 'lnb': row(rwkv_ln_b[0]), 'wout': ab_w_out[0].astype(BF16),
    }
    mods0 = mods_of(0)
    rope_mla = _rope_tables(l, lc, MLA_ROPE, _mla_place)
    (q, k, v, r, vv, kk, g, bonus, lw, ke, bb) = _ab_in(xall, mods0, p_ab, rope_mla, lc)
    att_c, att_l = _mla_attn(q, k, v, lc)
    yf, yb = _rwkv_scan(r, vv, kk, lw, ke, bb, lc)
    x1 = _ab_out(xall, att_c, att_l, yf, yb, bonus, g, mods0, p_ab, lc)
    x2 = _ffn(x1, mods0, _ffn_params(0, norm_ffn_g, ffn_w_up, ffn_conv_w, ffn_conv_b, ffn_w_down),
              ROW_BLOCK, (0, lc, ltot), seg_of, None)

    cw_in = cd_w_in[0]
    s1 = SSM_INNER
    s2 = s1 + SSM_INNER + 2 * SSM_G * SSM_N
    s3 = s2 + 2 * SSM_HEADS
    win_cd = jnp.concatenate([cw_in[:, :s2], padw(cw_in[:, s2:s3], LANES), cw_in[:, s3:]], axis=1).astype(BF16)
    a_neg = -jnp.exp(ssm_a_log[0])
    heads_to_lanes = np.zeros((LANES, SSM_INNER), np.float32)
    for hd in range(SSM_HEADS):
        heads_to_lanes[hd, hd * SSM_P:(hd + 1) * SSM_P] = 1.0
    p_cd = {
        'norm_g': row(norm_mix_g[1]), 'win': win_cd,
        'cw': ssm_conv_w[0], 'cb': row(ssm_conv_b[0]),
        'dtb': row(jnp.pad(ssm_dt_bias[0].reshape(-1), (0, LANES - 2 * SSM_HEADS))),
        'a_row': jnp.pad(a_neg, ((0, 0), (0, LANES - SSM_HEADS))).reshape(2, 1, LANES),
        'a_col': a_neg.reshape(2, SSM_HEADS, 1),
        'ex': jnp.asarray(heads_to_lanes, BF16),
        'dsk': row(jnp.repeat(ssm_d[0], SSM_P)),
        'ng': row(ssm_norm_g[0]), 'wout': cd_w_out[0].astype(BF16),
    }
    mods1 = mods_of(1)
    rope_swa = _rope_tables(l, lc, HEAD_DIM, _swa_place)
    z, xs, bcm, dt, sq, sk, sv = _cd_in(x2, mods1, p_cd, rope_swa, lc)
    dt2 = jnp.stack([jnp.pad(dt[:, :, dd * SSM_HEADS:(dd + 1) * SSM_HEADS],
                             ((0, 0), (0, 0), (0, LANES - SSM_HEADS))) for dd in range(2)])
    dtt2 = jnp.stack([jnp.swapaxes(dt[:, :, dd * SSM_HEADS:(dd + 1) * SSM_HEADS], 1, 2) for dd in range(2)])
    ysf, ysb = _ssd(xs, bcm, dt2, dtt2, p_cd, lc)
    satt = _swa(sq, sk, sv, swa_sink[0], lc)
    x3 = _cd_out(x2, ysf, ysb, z, satt, mods1, p_cd, lc)
    t1 = FFN_ROW_BLOCK_LATENT if l % FFN_ROW_BLOCK_LATENT == 0 else ROW_BLOCK
    return _ffn(x3, mods1, _ffn_params(1, norm_ffn_g, ffn_w_up, ffn_conv_w, ffn_conv_b, ffn_w_down),
                t1, (0, l), lambda i: 1, row(final_norm_g))
```

```python
import functools
import math

import jax
import jax.numpy as jnp
import numpy as np
from jax import lax
from jax.experimental import pallas as pl
from jax.experimental.pallas import tpu as pltpu

F32 = jnp.float32
BF16 = jnp.bfloat16

GRID_W = 64
ROPE_THETA = 10000.0
NORM_EPS = 1e-6
HEAD_DIM = 64
MLA_HEADS, MLA_NOPE, MLA_ROPE, MLA_V = 8, 64, 32, 64
MLA_Q_RANK, MLA_KV_RANK = 384, 256
RWKV_HEADS, RWKV_N = 8, 64
RWKV_W = RWKV_HEADS * RWKV_N
DECAY_LORA, ICLR_LORA, GATE_LORA = 64, 64, 128
GN_EPS = 64e-5
SSM_HEADS, SSM_P, SSM_G, SSM_N = 16, 64, 2, 128
SSM_INNER = SSM_HEADS * SSM_P
SSM_CONV = 5
SWA_HEADS, SWA_KV_HEADS, WINDOW = 8, 2, 128
D_FF = 2816
FFN_CONV = 3

LANES = 128
HALO = 16
ROW_BLOCK = 256
MLA_Q_BLOCKS = 2
RWKV_CHUNK = 64
RWKV_GROUP = 4
RWKV_PACK = 4
SSD_CHUNK = 128
SWA_BLOCK = 128
FFN_CHUNK = 256
FFN_DOWN_GROUP = 4
FFN_ROW_BLOCK_LATENT = 512
VMEM_LIMIT = 56 * 1024 * 1024
NEG_BIG = -1e30


def _cparams(sem):
    return pltpu.CompilerParams(dimension_semantics=sem, vmem_limit_bytes=VMEM_LIMIT)


def _sigmoid(x):
    return 1.0 / (1.0 + jnp.exp(-x))


def _silu(x):
    return x * _sigmoid(x)


def _rms(x, eps=NORM_EPS):
    return x * lax.rsqrt(jnp.mean(x * x, axis=-1, keepdims=True) + eps)


def _bdot(a, b):
    return jnp.dot(a.astype(BF16), b.astype(BF16), preferred_element_type=F32)


def _bdot_nt(a, b):
    return lax.dot_general(a.astype(BF16), b.astype(BF16), (((1,), (1,)), ((), ())),
                           preferred_element_type=F32)


def _bdot_tn(a, b):
    return lax.dot_general(a.astype(BF16), b.astype(BF16), (((0,), (0,)), ((), ())),
                           preferred_element_type=F32)


def _split3(x):
    hi = x.astype(BF16)
    r1 = x - hi.astype(F32)
    mid = r1.astype(BF16)
    lo = (r1 - mid.astype(F32)).astype(BF16)
    return hi, mid, lo


def _dot_exact_rhs(x, m):
    hi, mid, lo = _split3(x)
    return (jnp.dot(hi, m, preferred_element_type=F32) + jnp.dot(mid, m, preferred_element_type=F32)
            + jnp.dot(lo, m, preferred_element_type=F32))


def _group_sum(x, m):
    hi = x.astype(BF16)
    lo = (x - hi.astype(F32)).astype(BF16)
    return jnp.dot(hi, m, preferred_element_type=F32) + jnp.dot(lo, m, preferred_element_type=F32)


def _dot_exact_lhs(m, x):
    hi, mid, lo = _split3(x)
    return (jnp.dot(m, hi, preferred_element_type=F32) + jnp.dot(m, mid, preferred_element_type=F32)
            + jnp.dot(m, lo, preferred_element_type=F32))


def _dot_exact_rhs_nt(x, m):
    hi, mid, lo = _split3(x)
    dn = (((1,), (1,)), ((), ()))
    return (lax.dot_general(hi, m, dn, preferred_element_type=F32)
            + lax.dot_general(mid, m, dn, preferred_element_type=F32)
            + lax.dot_general(lo, m, dn, preferred_element_type=F32))


def _rope(t, cos, sin_a, sin_b, half):
    w = t.shape[-1]
    return t * cos + pltpu.roll(t, w - half, 1) * sin_a + pltpu.roll(t, half, 1) * sin_b


def _modnorm(x, g, shift, scale):
    return (_rms(x) * g) * (1.0 + scale) + shift


def _seg_flags(i, starts, ends):
    prev_ok = functools.reduce(jnp.logical_and, [i != s for s in starts])
    next_ok = functools.reduce(jnp.logical_and, [i != (e - 1) for e in ends])
    return prev_ok, next_ok


def _h_with_halo(x_ref, xp_ref, xn_ref, g, shift, scale, prev_ok, next_ok):
    h = _modnorm(x_ref[0], g, shift, scale)
    hp = _modnorm(xp_ref[0], g, shift, scale) * prev_ok.astype(F32)
    hn = _modnorm(xn_ref[0], g, shift, scale) * next_ok.astype(F32)
    return jnp.concatenate([hp.astype(BF16), h.astype(BF16), hn.astype(BF16)], axis=0)


def _shift_rows(x, delta, t):
    m = x.shape[0]
    return pltpu.roll(x, (m - delta) % m, 0)[HALO:HALO + t]


def _ada_kernel(c_ref, w_ref, b_ref, o_ref):
    o_ref[0] = _bdot(_silu(c_ref[...]), w_ref[0]) + b_ref[0]


def _ada_mods(cond, ada_w, ada_b):
    depth, d, n = ada_w.shape
    rows = cond.shape[0]
    tn = 1024
    return pl.pallas_call(
        _ada_kernel,
        grid=(depth, n // tn),
        in_specs=[pl.BlockSpec((rows, d), lambda l, j: (0, 0)),
                  pl.BlockSpec((1, d, tn), lambda l, j: (l, 0, j)),
                  pl.BlockSpec((1, 1, tn), lambda l, j: (l, 0, j))],
        out_specs=pl.BlockSpec((1, rows, tn), lambda l, j: (l, 0, j)),
        out_shape=jax.ShapeDtypeStruct((depth, rows, n), F32),
        compiler_params=_cparams(("parallel", "parallel")),
        name="ada_mods",
    )(cond, ada_w, ada_b.reshape(depth, 1, n))


def _row_specs(t, w, nrows8, off=0):
    per = t // HALO
    nh = nrows8
    main = pl.BlockSpec((1, t, w), lambda b, i: (b, i + off, 0))
    prev = pl.BlockSpec((1, HALO, w), lambda b, i: (b, jnp.maximum((i + off) * per - 1, 0), 0))
    nxt = pl.BlockSpec((1, HALO, w), lambda b, i: (b, jnp.minimum((i + off + 1) * per, nh - 1), 0))
    return main, prev, nxt


def _const_spec(shape):
    nd = len(shape)
    return pl.BlockSpec(shape, lambda *_: (0,) * nd)


def _resident_spec(shape):
    nd = len(shape)
    return pl.BlockSpec(shape, lambda *_: (0,) * nd, pipeline_mode=pl.Buffered(1))


def _ab_in_kernel(x_ref, xp_ref, xn_ref, mods_ref, g_ref, cos_ref, sa_ref, sb_ref, win_ref,
                  gq_ref, wq_ref, gkv_ref, wkk_ref, wvv_ref, mup_ref, mun_ref, g2_ref,
                  w0_ref, w2_ref, a0_ref, a2_ref, kkk_ref, ka_ref, rk_ref, bd_ref,
                  q_out, k_out, v_out, r_out, vv_out, kk_out, g_out, bonus_out,
                  lw_out, ke_out, b_out, *, starts, ends):
    i = pl.program_id(1)
    t = x_ref.shape[1]
    prev_ok, next_ok = _seg_flags(i, starts, ends)
    m = mods_ref[0, 0]
    h_all = _h_with_halo(x_ref, xp_ref, xn_ref, g_ref[...], m[0:1], m[1:2], prev_ok, next_ok)
    pin = jnp.dot(h_all, win_ref[...], preferred_element_type=F32)

    cos, sa, sb = cos_ref[...], sa_ref[...], sb_ref[...]
    main = pin[HALO:HALO + t]
    cq, ckv, kr = main[:, 0:384], main[:, 384:640], main[:, 640:768]
    qf = _bdot(_rms(cq) * gq_ref[...], wq_ref[...])
    kvn = (_rms(ckv) * gkv_ref[...]).astype(BF16)
    kf = jnp.dot(kvn, wkk_ref[...], preferred_element_type=F32)
    v_out[0] = jnp.dot(kvn, wvv_ref[...], preferred_element_type=F32).astype(BF16)
    krr = _rope(kr, cos, sa, sb, MLA_ROPE // 2)
    for h in range(MLA_HEADS):
        sl = slice(h * LANES, (h + 1) * LANES)
        q_out[0, h] = _rope(qf[:, sl], cos, sa, sb, MLA_ROPE // 2).astype(BF16)
        k_out[0, h] = (kf[:, sl] + krr).astype(BF16)

    rw = pin[:, 768:]
    x = rw[HALO:HALO + t]
    xp = _shift_rows(rw, -1, t)
    xn = _shift_rows(rw, 1, t)
    xs = x + mup_ref[...] * (xp - x) + mun_ref[...] * (xn - x)
    w = RWKV_W
    r, k, v = xs[:, 0:w], xs[:, w:2 * w], xs[:, 2 * w:3 * w]
    xw, xa, xg = xs[:, 3 * w:3 * w + 128], xs[:, 3 * w + 128:3 * w + 256], xs[:, 3 * w + 256:3 * w + 384]
    bd = bd_ref[...]
    g_out[0] = _bdot(_sigmoid(xg), g2_ref[...]).astype(g_out.dtype)
    kkr = k * kkk_ref[...]
    kk = kkr * lax.rsqrt(_group_sum(kkr * kkr, bd) + 1e-12)
    r_out[0] = r.astype(r_out.dtype)
    vv_out[0] = v.astype(vv_out.dtype)
    kk_out[0] = kk.astype(kk_out.dtype)
    bonus_out[0] = (_group_sum(r * k * rk_ref[...], bd) * v).astype(bonus_out.dtype)
    tw = jnp.tanh(xw).astype(BF16)
    xab = xa.astype(BF16)
    for d in range(2):
        wl = w0_ref[d] + jnp.dot(tw, w2_ref[d], preferred_element_type=F32)
        lw_out[d, 0] = -math.exp(-0.5) * _sigmoid(wl)
        a = _sigmoid(a0_ref[d] + jnp.dot(xab, a2_ref[d], preferred_element_type=F32))
        ke_out[d, 0] = (k * (1.0 + (a - 1.0) * ka_ref[...])).astype(ke_out.dtype)
        b_out[d, 0] = (kk * a).astype(b_out.dtype)


def _ab_in(x, mods, p, rope, lc):
    b, ltot, d = x.shape
    t = ROW_BLOCK
    nb, nbc = ltot // t, lc // t
    kern = functools.partial(_ab_in_kernel, starts=(0, nbc), ends=(nbc, nb))
    main, prev, nxt = _row_specs(t, d, ltot // HALO)
    tab = pl.BlockSpec((t, LANES), lambda bb, i: (i, 0))
    consts = [p['win'], p['gq'], p['wq'], p['gkv'], p['wkk'], p['wvv'], p['mup'], p['mun'], p['g2'],
              p['w0'], p['w2'], p['a0'], p['a2'], p['kkk'], p['ka'], p['rk'], p['bd']]
    in_specs = ([main, prev, nxt,
                 pl.BlockSpec((1, 1, 6, d), lambda bb, i: (bb, jnp.where(i >= nbc, 1, 0), 0, 0)),
                 _const_spec((1, d)), tab, tab, tab] + [_const_spec(c.shape) for c in consts])
    hq = pl.BlockSpec((1, MLA_HEADS, t, LANES), lambda bb, i: (bb, 0, i, 0))
    row = pl.BlockSpec((1, t, RWKV_W), lambda bb, i: (bb, i, 0))
    drow = pl.BlockSpec((2, 1, t, RWKV_W), lambda bb, i: (0, bb, i, 0))
    f_rows = jax.ShapeDtypeStruct((b, ltot, RWKV_W), BF16)
    f_drows = jax.ShapeDtypeStruct((2, b, ltot, RWKV_W), BF16)
    f_logw = jax.ShapeDtypeStruct((2, b, ltot, RWKV_W), F32)
    return pl.pallas_call(
        kern, grid=(b, nb), in_specs=in_specs,
        out_specs=[hq, hq, row, row, row, row, row, row, drow, drow, drow],
        out_shape=[jax.ShapeDtypeStruct((b, MLA_HEADS, ltot, LANES), BF16),
                   jax.ShapeDtypeStruct((b, MLA_HEADS, ltot, LANES), BF16),
                   jax.ShapeDtypeStruct((b, ltot, MLA_HEADS * MLA_V), BF16),
                   f_rows, f_rows, f_rows, f_rows, f_rows, f_logw, f_drows, f_drows],
        compiler_params=_cparams(("parallel", "parallel")),
        name="ab_in",
    )(x, x, x, mods, p['norm_g'], rope[0], rope[1], rope[2], *consts)


def _mla_attn_kernel(*refs, nq, exp2_scale):
    q_refs, (k_ref, v_ref, o_ref) = refs[:nq], refs[nq:]
    for hp in range(MLA_HEADS // 2):
        outs = []
        for h in (2 * hp, 2 * hp + 1):
            q = q_refs[0][0, h] if nq == 1 else jnp.concatenate([r[0, h] for r in q_refs], axis=0)
            s = lax.dot_general(q, k_ref[0, h], (((1,), (1,)), ((), ())), preferred_element_type=F32)
            mx = jnp.max(s, axis=-1, keepdims=True)
            e = jnp.exp2((s - mx) * exp2_scale)
            den = jnp.sum(e, axis=-1, keepdims=True)
            o2 = jnp.dot(e.astype(BF16), v_ref[0, :, hp * LANES:(hp + 1) * LANES],
                         preferred_element_type=F32)
            outs.append(o2 / den)
        lane = lax.broadcasted_iota(jnp.int32, outs[0].shape, 1)
        o_ref[0, :, hp * LANES:(hp + 1) * LANES] = jnp.where(lane < MLA_V, outs[0], outs[1]).astype(o_ref.dtype)


def _mla_attn(q, k, v, lc):
    b, hh, ltot, _ = q.shape
    l = ltot - lc
    vw = hh * MLA_V
    exp2_scale = (MLA_NOPE + MLA_ROPE) ** -0.5 * math.log2(math.e)
    att_c = pl.pallas_call(
        functools.partial(_mla_attn_kernel, nq=1, exp2_scale=exp2_scale), grid=(b,),
        in_specs=[pl.BlockSpec((1, hh, lc, LANES), lambda bb: (bb, 0, 0, 0)),
                  pl.BlockSpec((1, hh, lc, LANES), lambda bb: (bb, 0, 0, 0)),
                  pl.BlockSpec((1, lc, vw), lambda bb: (bb, 0, 0))],
        out_specs=pl.BlockSpec((1, lc, vw), lambda bb: (bb, 0, 0)),
        out_shape=jax.ShapeDtypeStruct((b, lc, vw), BF16),
        compiler_params=_cparams(("parallel",)),
        name="mla_attn_ctx",
    )(q, k, v)
    nq = MLA_Q_BLOCKS if l % (MLA_Q_BLOCKS * lc) == 0 else 1
    off = 1
    qspec = lambda j: pl.BlockSpec((1, hh, lc, LANES), lambda bb, i, j=j: (bb, 0, off + nq * i + j, 0))
    att_l = pl.pallas_call(
        functools.partial(_mla_attn_kernel, nq=nq, exp2_scale=exp2_scale), grid=(b, l // (nq * lc)),
        in_specs=[qspec(j) for j in range(nq)]
        + [pl.BlockSpec((1, hh, ltot, LANES), lambda bb, i: (bb, 0, 0, 0)),
           pl.BlockSpec((1, ltot, vw), lambda bb, i: (bb, 0, 0))],
        out_specs=pl.BlockSpec((1, nq * lc, vw), lambda bb, i: (bb, i, 0)),
        out_shape=jax.ShapeDtypeStruct((b, l, vw), BF16),
        compiler_params=_cparams(("parallel", "arbitrary")),
        name="mla_attn_lat",
    )(*([q] * nq), k, v)
    return att_c, att_l


def _rwkv_scan_kernel(rf_ref, vf_ref, kkf_ref, lwf_ref, kef_ref, bf_ref,
                      rb_ref, vb_ref, kkb_ref, lwb_ref, keb_ref, bb_ref,
                      yf_ref, yb_ref, s_ref, *, nchunks):
    n = pl.program_id(1)
    c = RWKV_CHUNK
    rows = nchunks * c

    @pl.when(n == 0)
    def _():
        s_ref[...] = jnp.zeros_like(s_ref)

    rowb = lax.broadcasted_iota(jnp.int32, (rows, rows), 0)
    colb = lax.broadcasted_iota(jnp.int32, (rows, rows), 1)
    same = (rowb // c) == (colb // c)
    hpg = RWKV_PACK
    gw = hpg * RWKV_N
    ngrp = RWKV_HEADS // hpg
    row = lax.broadcasted_iota(jnp.int32, (c, gw), 0)
    col = lax.broadcasted_iota(jnp.int32, (c, gw), 1) % RWKV_N
    eye = (row == col).astype(F32)
    lane_head = lax.broadcasted_iota(jnp.int32, (c, gw), 1) // RWKV_N
    head_masks = [lane_head == h for h in range(hpg)]
    diag_blocks = (lax.broadcasted_iota(jnp.int32, (gw, gw), 0) // RWKV_N
                   == lax.broadcasted_iota(jnp.int32, (gw, gw), 1) // RWKV_N)

    def bdiag(x):
        return jnp.concatenate([jnp.where(mk, x, 0.0) for mk in head_masks], axis=0).astype(BF16)

    def bmm(x, w):
        return jnp.dot(x.astype(BF16), bdiag(w), preferred_element_type=F32)

    def bmm_nt(x, w):
        return lax.dot_general(x.astype(BF16), bdiag(w), (((1,), (1,)), ((), ())), preferred_element_type=F32)

    dirs = []
    for sgn, (r_ref, v_ref, kk_ref, lw_ref, ke_ref, b_ref) in (
            (1, (rf_ref, vf_ref, kkf_ref, lwf_ref, kef_ref, bf_ref)),
            (-1, (rb_ref, vb_ref, kkb_ref, lwb_ref, keb_ref, bb_ref))):
        inclb = jnp.logical_and(same, sgn * (rowb - colb) >= 0)
        lw = lw_ref[0, 0]
        cum = _dot_exact_lhs(inclb.astype(BF16), lw)
        tot = _dot_exact_lhs(same.astype(BF16), lw)
        r, v, kk, ke, bb = (x.astype(F32) for x in (r_ref[0], v_ref[0], kk_ref[0], ke_ref[0, 0], b_ref[0, 0]))
        e_nc = jnp.exp(-cum)
        e_end = jnp.exp(tot - cum)
        dirs.append(dict(
            incl=sgn * (row - col) >= 0,
            strict=sgn * (row - col) > 0,
            v=v, rt=r * jnp.exp(cum), kt=kk * jnp.exp(cum - lw), bh=bb * e_nc, kh=ke * e_nc,
            bc=bb * e_end, kc=ke * e_end, gam=jnp.exp(tot)))

    chains = [(d, g, q) for g in range(nchunks) for q in range(ngrp) for d in range(2)]
    blk = lambda name, d, g, q: dirs[d][name][g * c:(g + 1) * c, q * gw:(q + 1) * gw]
    ms, ns, pbs, pks = [], [], [], []
    for d, g, q in chains:
        a = jnp.concatenate([blk('kt', d, g, q), blk('rt', d, g, q)], axis=0)
        mpb = bmm_nt(a, blk('bh', d, g, q))
        nk = bmm_nt(a, blk('kh', d, g, q))
        ms.append(jnp.where(dirs[d]['strict'], mpb[0:c], 0.0))
        pbs.append(jnp.where(dirs[d]['incl'], mpb[c:2 * c], 0.0))
        ns.append(jnp.where(dirs[d]['strict'], nk[0:c], 0.0))
        pks.append(jnp.where(dirs[d]['incl'], nk[c:2 * c], 0.0))
    xinv = [eye - m for m in ms]
    pw = [bmm(m, m) for m in ms]
    steps = int(math.log2(c)) - 1
    for it in range(steps):
        if it < steps - 1:
            xp = [bmm(jnp.concatenate([x, p], axis=0), p) for x, p in zip(xinv, pw)]
            xinv = [x + z[0:c] for x, z in zip(xinv, xp)]
            pw = [z[c:2 * c] for z in xp]
        else:
            xinv = [x + bmm(x, p) for x, p in zip(xinv, pw)]
    npk = [bmm(jnp.concatenate([nn, pk], axis=0), blk('v', *ch)) for ch, nn, pk in zip(chains, ns, pks)]
    ku = [-bmm(x, blk('kt', *ch)) for ch, x in zip(chains, xinv)]
    u0 = [-bmm(x, z[0:c]) for x, z in zip(xinv, npk)]
    ry = [(blk('rt', *ch) + bmm(pb, k_)).astype(BF16) for ch, pb, k_ in zip(chains, pbs, ku)]
    y0 = [bmm(pb, u_) + z[c:2 * c] for pb, u_, z in zip(pbs, u0, npk)]
    tmw, sadd = {}, {}
    for ch, k_, u_ in zip(chains, ku, u0):
        lhs = jnp.concatenate([jnp.concatenate([k_, u_], axis=1),
                               jnp.concatenate([jnp.zeros((c, gw), F32), blk('v', *ch)], axis=1)],
                              axis=0)
        rhs = jnp.concatenate([blk('bc', *ch), blk('kc', *ch)], axis=0)
        full = _bdot_tn(lhs, rhs)
        tmw[ch] = jnp.where(diag_blocks, full[0:gw], 0.0).astype(BF16)
        low = full[gw:2 * gw]
        sadd[ch] = functools.reduce(
            jnp.add, [jnp.where(head_masks[h], low[h * RWKV_N:(h + 1) * RWKV_N], 0.0) for h in range(hpg)])
    ry = dict(zip(chains, ry))
    y0 = dict(zip(chains, y0))

    state = {(d, q): s_ref[d, :, q * gw:(q + 1) * gw] for d in range(2) for q in range(ngrp)}
    youts = [[None] * nchunks for _ in range(2)]
    for step in range(nchunks):
        for d in range(2):
            g = step if d == 0 else nchunks - 1 - step
            ys = []
            for q in range(ngrp):
                ch = (d, g, q)
                s0 = state[(d, q)]
                ys.append(y0[ch] + lax.dot_general(ry[ch], bdiag(s0), (((1,), (1,)), ((), ())),
                                                   preferred_element_type=F32))
                state[(d, q)] = (s0 * dirs[d]['gam'][g * c:g * c + 1, q * gw:(q + 1) * gw]
                                 + jnp.dot(s0.astype(BF16), tmw[ch], preferred_element_type=F32) + sadd[ch])
            youts[d][g] = jnp.concatenate(ys, axis=1)
    yf_ref[0] = jnp.concatenate(youts[0], axis=0).astype(yf_ref.dtype)
    yb_ref[0] = jnp.concatenate(youts[1], axis=0).astype(yb_ref.dtype)
    for d in range(2):
        for q in range(ngrp):
            s_ref[d, :, q * gw:(q + 1) * gw] = state[(d, q)]


def _scan_chunk_index(n, nc_ctx, nc, d):
    bwd = jnp.where(n < nc_ctx, nc_ctx - 1 - n, nc - 1 - (n - nc_ctx))
    return jnp.where(d == 0, n, bwd)


def _rwkv_scan(r, v, kk, lw, ke, bb, lc):
    b, ltot, w = r.shape
    nch = RWKV_GROUP
    rows = nch * RWKV_CHUNK
    nc, ncc = ltot // rows, lc // rows
    specs = []
    for d in range(2):
        cidx = functools.partial(_scan_chunk_index, nc_ctx=ncc, nc=nc, d=d)
        shared = pl.BlockSpec((1, rows, w), lambda bi, n, cidx=cidx: (bi, cidx(n), 0))
        perdir = pl.BlockSpec((1, 1, rows, w), lambda bi, n, cidx=cidx, d=d: (d, bi, cidx(n), 0))
        specs.append((shared, perdir))
    kern = functools.partial(_rwkv_scan_kernel, nchunks=nch)
    out = jax.ShapeDtypeStruct((b, ltot, w), BF16)
    return pl.pallas_call(
        kern, grid=(b, nc),
        in_specs=[specs[0][0]] * 3 + [specs[0][1]] * 3 + [specs[1][0]] * 3 + [specs[1][1]] * 3,
        out_specs=[specs[0][0], specs[1][0]],
        out_shape=[out, out],
        scratch_shapes=[pltpu.VMEM((2, RWKV_N, w), F32)],
        compiler_params=_cparams(("parallel", "arbitrary")),
        name="rwkv_scan",
    )(r, v, kk, lw, ke, bb, r, v, kk, lw, ke, bb)


def _ab_out_kernel(x_ref, attc_ref, attl_ref, yf_ref, yb_ref, bonus_ref, g_ref, mods_ref, lng_ref, lnb_ref,
                   bd_ref, wout_ref, o_ref, *, nbc):
    att = jnp.where(pl.program_id(1) < nbc, attc_ref[0].astype(F32), attl_ref[0].astype(F32))
    y = yf_ref[0].astype(F32) + yb_ref[0].astype(F32)
    bd = bd_ref[...]
    mu = _group_sum(y, bd) * (1.0 / RWKV_N)
    dl = y - mu
    var = _group_sum(dl * dl, bd) * (1.0 / RWKV_N)
    yn = dl * lax.rsqrt(var + GN_EPS) * lng_ref[...] + lnb_ref[...]
    rwo = (yn + bonus_ref[0].astype(F32)) * g_ref[0].astype(F32)
    feat = jnp.concatenate([att.astype(BF16), rwo.astype(BF16)], axis=1)
    o = jnp.dot(feat, wout_ref[...], preferred_element_type=F32)
    o_ref[0] = x_ref[0] + mods_ref[0, 0][2:3] * o


def _ab_out(x, att_c, att_l, yf, yb, bonus, g, mods, p, lc):
    b, ltot, d = x.shape
    t = ROW_BLOCK
    nbc = lc // t
    row = lambda w: pl.BlockSpec((1, t, w), lambda bb, i: (bb, i, 0))
    aw = att_c.shape[-1]
    consts = [p['lng'], p['lnb'], p['bd'], p['wout']]
    return pl.pallas_call(
        functools.partial(_ab_out_kernel, nbc=nbc), grid=(b, ltot // t),
        in_specs=[row(d),
                  pl.BlockSpec((1, t, aw), lambda bb, i: (bb, jnp.minimum(i, nbc - 1), 0)),
                  pl.BlockSpec((1, t, aw), lambda bb, i: (bb, jnp.maximum(i - nbc, 0), 0)),
                  row(RWKV_W), row(RWKV_W), row(RWKV_W), row(RWKV_W),
                  pl.BlockSpec((1, 1, 6, d), lambda bb, i: (bb, jnp.where(i >= nbc, 1, 0), 0, 0))]
        + [_const_spec(c.shape) for c in consts],
        out_specs=row(d),
        out_shape=jax.ShapeDtypeStruct((b, ltot, d), F32),
        compiler_params=_cparams(("parallel", "parallel")),
        name="ab_out",
    )(x, att_c, att_l, yf, yb, bonus, g, mods, *consts)


def _ffn_kernel(x_ref, xp_ref, xn_ref, mods_ref, g_ref, wg_ref, wv_ref, cwg_ref, cwv_ref,
                cbg_ref, cbv_ref, wd_ref, fg_ref, o_ref, act_ref, *, starts, ends, final_norm):
    i = pl.program_id(1)
    t = x_ref.shape[1]
    prev_ok, next_ok = _seg_flags(i, starts, ends)
    m = mods_ref[0, 0]
    h_all = _h_with_halo(x_ref, xp_ref, xn_ref, g_ref[...], m[3:4], m[4:5], prev_ok, next_ok)
    nch = wg_ref.shape[0]
    fc = wg_ref.shape[2]

    def conv(u, cw, cb):
        return (cw[0:1] * _shift_rows(u, -1, t) + cw[1:2] * u[HALO:HALO + t]
                + cw[2:3] * _shift_rows(u, 1, t) + cb)

    def up(c):
        return (jnp.dot(h_all, wg_ref[c], preferred_element_type=F32),
                jnp.dot(h_all, wv_ref[c], preferred_element_type=F32))

    nxt = up(0)
    acc = None
    lo = 0
    for c in range(nch):
        ug, uv = nxt
        if c + 1 < nch:
            nxt = up(c + 1)
        act = _silu(conv(ug, cwg_ref[c], cbg_ref[c])) * conv(uv, cwv_ref[c], cbv_ref[c])
        act_ref[:, c * fc:(c + 1) * fc] = act.astype(BF16)
        if (c + 1 - lo) == FFN_DOWN_GROUP or c + 1 == nch:
            part = jnp.dot(act_ref[:, lo * fc:(c + 1) * fc], wd_ref[lo * fc:(c + 1) * fc, :],
                           preferred_element_type=F32)
            acc = part if acc is None else acc + part
            lo = c + 1
    out = x_ref[0] + m[5:6] * acc
    if final_norm:
        out = _rms(out) * fg_ref[...]
    o_ref[0] = out


def _ffn(x, mods, p, t, seg_rows, seg_index, final_g):
    b, rows, d = x.shape
    bounds = [r // t for r in seg_rows]
    starts, ends = tuple(bounds[:-1]), tuple(bounds[1:])
    kern = functools.partial(_ffn_kernel, starts=starts, ends=ends, final_norm=final_g is not None)
    main, prev, nxt = _row_specs(t, d, rows // HALO)
    fg = final_g if final_g is not None else jnp.ones((1, d), F32)
    consts = [p['wg'], p['wv'], p['cwg'], p['cwv'], p['cbg'], p['cbv'], p['wd'], fg]
    return pl.pallas_call(
        kern, grid=(b, rows // t),
        in_specs=[main, prev, nxt,
                  pl.BlockSpec((1, 1, 6, d), lambda bb, i: (bb, seg_index(i), 0, 0)),
                  _const_spec((1, d))] + [_resident_spec(c.shape) for c in consts],
        out_specs=pl.BlockSpec((1, t, d), lambda bb, i: (bb, i, 0)),
        out_shape=jax.ShapeDtypeStruct((b, rows, d), F32),
        scratch_shapes=[pltpu.VMEM((t, D_FF), BF16)],
        compiler_params=_cparams(("parallel", "parallel")),
        name="conv_ffn",
    )(x, x, x, mods, p['norm_g'], *consts)


def _cd_in_kernel(x_ref, xp_ref, xn_ref, mods_ref, g_ref, cos_ref, sa_ref, sb_ref, win_ref,
                  cw_ref, cb_ref, dtb_ref,
                  z_out, xs_out, bc_out, dt_out, q_out, k_out, v_out, *, starts, ends):
    i = pl.program_id(1)
    t = x_ref.shape[1]
    prev_ok, next_ok = _seg_flags(i, starts, ends)
    m = mods_ref[0, 0]
    h_all = _h_with_halo(x_ref, xp_ref, xn_ref, g_ref[...], m[0:1], m[1:2], prev_ok, next_ok)
    pin = jnp.dot(h_all, win_ref[...], preferred_element_type=F32)
    main = pin[HALO:HALO + t]
    z_out[0] = main[:, 0:1024].astype(z_out.dtype)
    xbc = pin[:, 1024:2560]
    cw = cw_ref[...]
    conv = cb_ref[...] + cw[2:3] * xbc[HALO:HALO + t]
    for j in (0, 1, 3, 4):
        conv = conv + cw[j:j + 1] * _shift_rows(xbc, j - SSM_CONV // 2, t)
    xc = _silu(conv)
    xs_out[0] = xc[:, 0:SSM_INNER].astype(xs_out.dtype)
    bc_out[0] = xc[:, SSM_INNER:].astype(bc_out.dtype)
    raw = main[:, 2560:2688] + dtb_ref[...]
    dt_out[0] = jnp.maximum(raw, 0.0) + jnp.log(1.0 + jnp.exp(-jnp.abs(raw)))
    cos, sa, sb = cos_ref[...], sa_ref[...], sb_ref[...]
    q = main[:, 2688:3200]
    tile = lambda a: jnp.concatenate([a] * (q.shape[1] // LANES), axis=1)
    q_out[0] = _rope(q, tile(cos), tile(sa), tile(sb), HEAD_DIM // 2).astype(BF16)
    k_out[0] = _rope(main[:, 3200:3328], cos, sa, sb, HEAD_DIM // 2).astype(BF16)
    v_out[0] = main[:, 3328:3456].astype(BF16)


def _cd_in(x, mods, p, rope, lc):
    b, ltot, d = x.shape
    t = ROW_BLOCK
    nb, nbc = ltot // t, lc // t
    kern = functools.partial(_cd_in_kernel, starts=(0, nbc), ends=(nbc, nb))
    main, prev, nxt = _row_specs(t, d, ltot // HALO)
    tab = pl.BlockSpec((t, LANES), lambda bb, i: (i, 0))
    consts = [p['win'], p['cw'], p['cb'], p['dtb']]
    row = lambda w: pl.BlockSpec((1, t, w), lambda bb, i: (bb, i, 0))
    sds = lambda w, dt: jax.ShapeDtypeStruct((b, ltot, w), dt)
    kvw = SWA_KV_HEADS * HEAD_DIM
    return pl.pallas_call(
        kern, grid=(b, nb),
        in_specs=[main, prev, nxt,
                  pl.BlockSpec((1, 1, 6, d), lambda bb, i: (bb, jnp.where(i >= nbc, 1, 0), 0, 0)),
                  _const_spec((1, d)), tab, tab, tab] + [_const_spec(c.shape) for c in consts],
        out_specs=[row(SSM_INNER), row(SSM_INNER), row(2 * SSM_G * SSM_N), row(LANES),
                   row(SWA_HEADS * HEAD_DIM), row(kvw), row(kvw)],
        out_shape=[sds(SSM_INNER, BF16), sds(SSM_INNER, BF16), sds(2 * SSM_G * SSM_N, BF16), sds(LANES, F32),
                   sds(SWA_HEADS * HEAD_DIM, BF16), sds(kvw, BF16), sds(kvw, BF16)],
        compiler_params=_cparams(("parallel", "parallel")),
        name="cd_in",
    )(x, x, x, mods, p['norm_g'], rope[0], rope[1], rope[2], *consts)


def _ssd_kernel(xsf_ref, bcf_ref, dtf_ref, dttf_ref, xsb_ref, bcb_ref, dtb_ref, dttb_ref,
                a_ref, at_ref, ex_ref, dsk_ref, yf_ref, yb_ref, h_ref):
    n = pl.program_id(1)
    q = SSD_CHUNK
    hg = SSM_HEADS // SSM_G
    gw = hg * SSM_P

    @pl.when(n == 0)
    def _():
        h_ref[...] = jnp.zeros_like(h_ref)

    row = lax.broadcasted_iota(jnp.int32, (q, q), 0)
    col = lax.broadcasted_iota(jnp.int32, (q, q), 1)
    ex = ex_ref[...]
    dirs = []
    for d, (xs_ref, bc_ref, dt_ref, dtt_ref, y_ref) in enumerate(
            ((xsf_ref, bcf_ref, dtf_ref, dttf_ref, yf_ref), (xsb_ref, bcb_ref, dtb_ref, dttb_ref, yb_ref))):
        sgn = 1 if d == 0 else -1
        incl = sgn * (row - col) >= 0
        inclb = incl.astype(BF16)
        dt_c = dt_ref[0, 0]
        dt_r = dtt_ref[0, 0]
        acum_c = _dot_exact_lhs(inclb, dt_c * a_ref[d])
        acum_r = _dot_exact_rhs_nt(dt_r * at_ref[d], inclb)
        end_c = acum_c[q - 1:q] if d == 0 else acum_c[0:1]
        grow = _group_sum(jnp.exp(acum_c), ex)
        tail = _group_sum(jnp.exp(end_c - acum_c) * dt_c, ex)
        dirs.append(dict(incl=incl, dt_r=dt_r, acum_c=acum_c, acum_r=acum_r, grow=grow, tail=tail,
                         chunk_decay=grow[q - 1:q] if d == 0 else grow[0:1],
                         xs=xs_ref[0].astype(F32), bcm=bc_ref[0], y_ref=y_ref))

    for g in range(SSM_G):
        gs = slice(g * gw, (g + 1) * gw)
        for d in range(2):
            p = dirs[d]
            bg = p['bcm'][:, g * SSM_N:(g + 1) * SSM_N].astype(BF16)
            cg = p['bcm'][:, (SSM_G + g) * SSM_N:(SSM_G + g + 1) * SSM_N].astype(BF16)
            cb = lax.dot_general(cg, bg, (((1,), (1,)), ((), ())), preferred_element_type=F32)
            xg = p['xs'][:, gs]
            yd = []
            for hh in range(hg):
                hd = g * hg + hh
                seg = jnp.where(p['incl'], p['acum_c'][:, hd:hd + 1] - p['acum_r'][hd:hd + 1, :], NEG_BIG)
                wgt = cb * jnp.exp(seg) * p['dt_r'][hd:hd + 1, :]
                yd.append(_bdot(wgt, xg[:, hh * SSM_P:(hh + 1) * SSM_P]))
            h_in = h_ref[d, g]
            y = jnp.concatenate(yd, axis=1) + (
                jnp.dot(cg, h_in.astype(BF16), preferred_element_type=F32) * p['grow'][:, gs])
            if d == 0:
                y = y + dsk_ref[:, gs] * xg
            p['y_ref'][0, :, gs] = y.astype(p['y_ref'].dtype)
            h_ref[d, g] = h_in * p['chunk_decay'][:, gs] + _bdot_tn(bg, xg * p['tail'][:, gs])


def _ssd(xs, bcm, dt2, dtt2, p, lc):
    b, ltot, inner = xs.shape
    q = SSD_CHUNK
    nc, ncc = ltot // q, lc // q
    hg = SSM_HEADS // SSM_G
    in_specs, out_specs = [], []
    for d in range(2):
        cidx = functools.partial(_scan_chunk_index, nc_ctx=ncc, nc=nc, d=d)
        in_specs += [pl.BlockSpec((1, q, inner), lambda bi, n, cidx=cidx: (bi, cidx(n), 0)),
                     pl.BlockSpec((1, q, 2 * SSM_G * SSM_N), lambda bi, n, cidx=cidx: (bi, cidx(n), 0)),
                     pl.BlockSpec((1, 1, q, LANES), lambda bi, n, cidx=cidx, d=d: (d, bi, cidx(n), 0)),
                     pl.BlockSpec((1, 1, SSM_HEADS, q), lambda bi, n, cidx=cidx, d=d: (d, bi, 0, cidx(n)))]
        out_specs.append(pl.BlockSpec((1, q, inner), lambda bi, n, cidx=cidx: (bi, cidx(n), 0)))
    consts = [p['a_row'], p['a_col'], p['ex'], p['dsk']]
    out = jax.ShapeDtypeStruct((b, ltot, inner), BF16)
    return pl.pallas_call(
        _ssd_kernel, grid=(b, nc),
        in_specs=in_specs + [_const_spec(c.shape) for c in consts],
        out_specs=out_specs, out_shape=[out, out],
        scratch_shapes=[pltpu.VMEM((2, SSM_G, SSM_N, hg * SSM_P), F32)],
        compiler_params=_cparams(("parallel", "arbitrary")),
        name="ssd_scan",
    )(xs, bcm, dt2, dtt2, xs, bcm, dt2, dtt2, *consts)


def _swa_kernel(sink_ref, bias_ref, q_ref, kc_ref, kp_ref, k0_ref, kn_ref, vc_ref, vp_ref, v0_ref, vn_ref,
                o_ref, *, scale):
    t = q_ref.shape[1]
    hg = SWA_HEADS // SWA_KV_HEADS
    kall = jnp.concatenate([kc_ref[0], kp_ref[0], k0_ref[0], kn_ref[0]], axis=0)
    vall = jnp.concatenate([vc_ref[0], vp_ref[0], v0_ref[0], vn_ref[0]], axis=0)
    bias = bias_ref[0]
    rowh = lax.broadcasted_iota(jnp.int32, (hg * t, 1), 0) // t
    qb = q_ref[0]
    heads = []
    for g in range(SWA_KV_HEADS):
        qg = jnp.concatenate([qb[:, (g * hg + j) * HEAD_DIM:(g * hg + j + 1) * HEAD_DIM] for j in range(hg)],
                             axis=0)
        kg = kall[:, g * HEAD_DIM:(g + 1) * HEAD_DIM]
        vg = vall[:, g * HEAD_DIM:(g + 1) * HEAD_DIM]
        s = lax.dot_general(qg, kg, (((1,), (1,)), ((), ())), preferred_element_type=F32) * scale + bias
        sink = jnp.zeros((hg * t, 1), F32)
        for j in range(hg):
            sink = jnp.where(rowh == j, sink_ref[g * hg + j], sink)
        mx = jnp.maximum(jnp.max(s, axis=-1, keepdims=True), sink)
        e = jnp.exp(s - mx)
        den = jnp.sum(e, axis=-1, keepdims=True) + jnp.exp(sink - mx)
        og = jnp.dot(e.astype(BF16), vg, preferred_element_type=F32) / den
        heads += [og[j * t:(j + 1) * t] for j in range(hg)]
    o_ref[0] = jnp.concatenate(heads, axis=1).astype(o_ref.dtype)


def _swa(q, k, v, sink, lc):
    b, ltot, qw = q.shape
    t = SWA_BLOCK
    nb = (ltot - lc) // t
    off = lc // t
    kvw = k.shape[-1]
    kern = functools.partial(_swa_kernel, scale=HEAD_DIM ** -0.5)
    hg = SWA_HEADS // SWA_KV_HEADS
    rr = (np.arange(hg * t) % t)[:, None]
    cc = np.arange(lc + 3 * t)[None, :] - lc
    variants = []
    for has_prev in (False, True):
        for has_next in (False, True):
            lo = rr if has_prev else t
            hi = rr + 2 * t if has_next else 2 * t - 1
            ok = np.broadcast_to((cc < 0) | ((cc >= lo) & (cc <= hi)), (hg * t, lc + 3 * t))
            variants.append(np.where(ok, 0.0, NEG_BIG).astype(np.float32))
    bias = jnp.asarray(np.stack(variants))
    bias_spec = pl.BlockSpec((1, hg * t, lc + 3 * t),
                             lambda bb, i: (2 * (i > 0).astype(jnp.int32) + (i < nb - 1).astype(jnp.int32), 0, 0))
    ctx = pl.BlockSpec((1, lc, kvw), lambda bb, i: (bb, 0, 0))
    prev = pl.BlockSpec((1, t, kvw), lambda bb, i: (bb, off + jnp.maximum(i - 1, 0), 0))
    cur = pl.BlockSpec((1, t, kvw), lambda bb, i: (bb, off + i, 0))
    nxt = pl.BlockSpec((1, t, kvw), lambda bb, i: (bb, off + jnp.minimum(i + 1, nb - 1), 0))
    return pl.pallas_call(
        kern, grid=(b, nb),
        in_specs=[pl.BlockSpec(memory_space=pltpu.SMEM), bias_spec,
                  pl.BlockSpec((1, t, qw), lambda bb, i: (bb, off + i, 0)),
                  ctx, prev, cur, nxt, ctx, prev, cur, nxt],
        out_specs=pl.BlockSpec((1, t, qw), lambda bb, i: (bb, i, 0)),
        out_shape=jax.ShapeDtypeStruct((b, ltot - lc, qw), BF16),
        compiler_params=_cparams(("parallel", "parallel")),
        name="swa_attn",
    )(sink, bias, q, k, k, k, k, v, v, v, v)


def _cd_out_kernel(x_ref, yf_ref, yb_ref, z_ref, att_ref, mods_ref, ng_ref, wout_ref, o_ref):
    u = (yf_ref[0].astype(F32) + yb_ref[0].astype(F32)) * _silu(z_ref[0].astype(F32))
    gw = SSM_INNER // SSM_G
    un = jnp.concatenate([_rms(u[:, g * gw:(g + 1) * gw]) for g in range(SSM_G)], axis=1) * ng_ref[...]
    feat = jnp.concatenate([un.astype(BF16), att_ref[0].astype(BF16)], axis=1)
    o = jnp.dot(feat, wout_ref[...], preferred_element_type=F32)
    o_ref[0] = x_ref[0] + mods_ref[0, 0][2:3] * o


def _cd_out(x, yf, yb, z, att, mods, p, lc):
    b, ltot, d = x.shape
    t = ROW_BLOCK
    off = lc // t
    nbl = (ltot - lc) // t
    full = lambda w: pl.BlockSpec((1, t, w), lambda bb, i: (bb, i + off, 0))
    return pl.pallas_call(
        _cd_out_kernel, grid=(b, nbl),
        in_specs=[full(d), full(SSM_INNER), full(SSM_INNER),
                  full(SSM_INNER), pl.BlockSpec((1, t, att.shape[-1]), lambda bb, i: (bb, i, 0)),
                  pl.BlockSpec((1, 1, 6, d), lambda bb, i: (bb, 1, 0, 0)),
                  _const_spec(p['ng'].shape), _const_spec(p['wout'].shape)],
        out_specs=pl.BlockSpec((1, t, d), lambda bb, i: (bb, i, 0)),
        out_shape=jax.ShapeDtypeStruct((b, ltot - lc, d), F32),
        compiler_params=_cparams(("parallel", "parallel")),
        name="cd_out",
    )(x, yf, yb, z, att, mods, p['ng'], p['wout'])


def _pad_cols(w, width):
    return jnp.pad(w, ((0, 0), (0, width - w.shape[1])))


def _rope_tables(n_tok, lc, rot_dim, place):
    rows = n_tok // GRID_W
    rp, cp = jnp.meshgrid(jnp.arange(rows, dtype=F32), jnp.arange(GRID_W, dtype=F32), indexing='ij')
    n_freq = rot_dim // 4
    inv_freq = ROPE_THETA ** (-jnp.arange(n_freq, dtype=F32) / n_freq)
    ang = jnp.concatenate([rp.reshape(-1, 1) * inv_freq, cp.reshape(-1, 1) * inv_freq], axis=-1)
    cos = jnp.concatenate([jnp.ones((lc, rot_dim // 2), F32), jnp.cos(ang)], axis=0)
    sin = jnp.concatenate([jnp.zeros((lc, rot_dim // 2), F32), jnp.sin(ang)], axis=0)
    zero = jnp.zeros_like(sin)
    return place(cos, cos, 1.0), place(-sin, zero, 0.0), place(zero, sin, 0.0)


def _mla_place(first, second, fill):
    n = first.shape[0]
    return jnp.concatenate([jnp.full((n, MLA_NOPE), fill, F32), first, second,
                            jnp.zeros((n, LANES - MLA_NOPE - MLA_ROPE), F32)], axis=1)


def _swa_place(first, second, fill):
    return jnp.concatenate([first, second] * (LANES // HEAD_DIM), axis=1)


def _block_diag_ones(n, group):
    idx = np.arange(n) // group
    return jnp.asarray(idx[:, None] == idx[None, :], BF16)


def _ffn_params(i, norm_ffn_g, ffn_w_up, ffn_conv_w, ffn_conv_b, ffn_w_down):
    d = ffn_w_up.shape[1]
    nch = D_FF // FFN_CHUNK
    chunks = lambda w: w.reshape(w.shape[0], nch, FFN_CHUNK).transpose(1, 0, 2)
    wup = ffn_w_up[i].astype(BF16)
    return {
        'norm_g': norm_ffn_g[i].reshape(1, d),
        'wg': chunks(wup[:, :D_FF]), 'wv': chunks(wup[:, D_FF:]),
        'cwg': chunks(ffn_conv_w[i][:, :D_FF]), 'cwv': chunks(ffn_conv_w[i][:, D_FF:]),
        'cbg': chunks(ffn_conv_b[i][None, :D_FF]), 'cbv': chunks(ffn_conv_b[i][None, D_FF:]),
        'wd': ffn_w_down[i].astype(BF16),
    }


def kernel(x, c, ctx, c_ctx, ada_w, ada_b, norm_mix_g, norm_ffn_g, ffn_w_up, ffn_conv_w, ffn_conv_b, ffn_w_down, final_norm_g, ab_w_in, ab_w_out, mla_q_norm_g, mla_w_q_up, mla_kv_norm_g, mla_w_kv_up, rwkv_mu_prev, rwkv_mu_next, rwkv_w0, rwkv_w2, rwkv_a0, rwkv_a2, rwkv_g2, rwkv_k_k, rwkv_k_a, rwkv_r_k, rwkv_ln_g, rwkv_ln_b, cd_w_in, cd_w_out, ssm_conv_w, ssm_conv_b, ssm_dt_bias, ssm_a_log, ssm_d, ssm_norm_g, swa_sink):
    b, l, d = x.shape
    lc = ctx.shape[1]
    assert ada_w.shape[0] == 2 and lc % ROW_BLOCK == 0 and l % ROW_BLOCK == 0
    ltot = lc + l
    xall = jnp.concatenate([ctx, x], axis=1)

    nrow = -(-(b + 1) // 8) * 8
    cond = jnp.zeros((nrow, d), F32).at[:b].set(c).at[b].set(c_ctx)
    ada = _ada_mods(cond, ada_w, ada_b)
    def mods_of(i):
        lat = ada[i, :b].reshape(b, 1, 6, d)
        cx = jnp.broadcast_to(ada[i, b].reshape(1, 1, 6, d), (b, 1, 6, d))
        return jnp.concatenate([cx, lat], axis=1)

    row = lambda v: v.reshape(1, -1)
    nbc, nb = lc // ROW_BLOCK, ltot // ROW_BLOCK
    seg_of = lambda i: jnp.where(i >= nbc, 1, 0)

    w_in = ab_w_in[0]
    padw = lambda w, n: jnp.pad(w, ((0, 0), (0, n - w.shape[1])))
    o1, o2, o3 = MLA_Q_RANK, MLA_Q_RANK + MLA_KV_RANK, MLA_Q_RANK + MLA_KV_RANK + MLA_ROPE
    w_kr = jnp.pad(w_in[:, o2:o3], ((0, 0), (MLA_NOPE, LANES - MLA_NOPE - MLA_ROPE)))
    rw = w_in[:, o3:]
    rw_sizes = (3 * RWKV_W, DECAY_LORA, ICLR_LORA, GATE_LORA)
    def pad_rw(v):
        a, bq, cq_, dq = jnp.split(v, np.cumsum(rw_sizes)[:-1].tolist(), axis=-1)
        z = jnp.zeros(v.shape[:-1] + (LANES - DECAY_LORA,), v.dtype)
        return jnp.concatenate([a, bq, z, cq_, z, dq], axis=-1)
    win_ab = jnp.concatenate([w_in[:, :o2], w_kr, pad_rw(rw)], axis=1).astype(BF16)
    qd = MLA_NOPE + MLA_ROPE
    wq = jnp.pad(mla_w_q_up[0].reshape(MLA_Q_RANK, MLA_HEADS, qd), ((0, 0), (0, 0), (0, LANES - qd)))
    wkv = mla_w_kv_up[0].reshape(MLA_KV_RANK, MLA_HEADS, MLA_NOPE + MLA_V)
    wkk = jnp.pad(wkv[:, :, :MLA_NOPE], ((0, 0), (0, 0), (0, LANES - MLA_NOPE)))
    padrows = lambda w: jnp.pad(w, ((0, 0), (0, LANES - w.shape[1]), (0, 0)))
    p_ab = {
        'norm_g': row(norm_mix_g[0]), 'win': win_ab,
        'gq': row(mla_q_norm_g[0]), 'wq': wq.reshape(MLA_Q_RANK, MLA_HEADS * LANES).astype(BF16),
        'gkv': row(mla_kv_norm_g[0]), 'wkk': wkk.reshape(MLA_KV_RANK, MLA_HEADS * LANES).astype(BF16),
        'wvv': wkv[:, :, MLA_NOPE:].reshape(MLA_KV_RANK, MLA_HEADS * MLA_V).astype(BF16),
        'mup': row(pad_rw(rwkv_mu_prev[0])), 'mun': row(pad_rw(rwkv_mu_next[0])),
        'g2': rwkv_g2[0].astype(BF16),
        'w0': rwkv_w0[0].reshape(2, 1, RWKV_W), 'w2': padrows(rwkv_w2[0]).astype(BF16),
        'a0': rwkv_a0[0].reshape(2, 1, RWKV_W), 'a2': padrows(rwkv_a2[0]).astype(BF16),
        'kkk': row(rwkv_k_k[0]), 'ka': row(rwkv_k_a[0]), 'rk': row(rwkv_r_k[0]),
        'bd': _block_diag_ones(RWKV_W, RWKV_N),
        'lng': row(rwkv_ln_g[0]), 'lnb': row(rwkv_ln_b[0]), 'wout': ab_w_out[0].astype(BF16),
    }
    mods0 = mods_of(0)
    rope_mla = _rope_tables(l, lc, MLA_ROPE, _mla_place)
    (q, k, v, r, vv, kk, g, bonus, lw, ke, bb) = _ab_in(xall, mods0, p_ab, rope_mla, lc)
    att_c, att_l = _mla_attn(q, k, v, lc)
    yf, yb = _rwkv_scan(r, vv, kk, lw, ke, bb, lc)
    x1 = _ab_out(xall, att_c, att_l, yf, yb, bonus, g, mods0, p_ab, lc)
    x2 = _ffn(x1, mods0, _ffn_params(0, norm_ffn_g, ffn_w_up, ffn_conv_w, ffn_conv_b, ffn_w_down),
              ROW_BLOCK, (0, lc, ltot), seg_of, None)

    cw_in = cd_w_in[0]
    s1 = SSM_INNER
    s2 = s1 + SSM_INNER + 2 * SSM_G * SSM_N
    s3 = s2 + 2 * SSM_HEADS
    win_cd = jnp.concatenate([cw_in[:, :s2], padw(cw_in[:, s2:s3], LANES), cw_in[:, s3:]], axis=1).astype(BF16)
    a_neg = -jnp.exp(ssm_a_log[0])
    heads_to_lanes = np.zeros((LANES, SSM_INNER), np.float32)
    for hd in range(SSM_HEADS):
        heads_to_lanes[hd, hd * SSM_P:(hd + 1) * SSM_P] = 1.0
    p_cd = {
        'norm_g': row(norm_mix_g[1]), 'win': win_cd,
        'cw': ssm_conv_w[0], 'cb': row(ssm_conv_b[0]),
        'dtb': row(jnp.pad(ssm_dt_bias[0].reshape(-1), (0, LANES - 2 * SSM_HEADS))),
        'a_row': jnp.pad(a_neg, ((0, 0), (0, LANES - SSM_HEADS))).reshape(2, 1, LANES),
        'a_col': a_neg.reshape(2, SSM_HEADS, 1),
        'ex': jnp.asarray(heads_to_lanes, BF16),
        'dsk': row(jnp.repeat(ssm_d[0], SSM_P)),
        'ng': row(ssm_norm_g[0]), 'wout': cd_w_out[0].astype(BF16),
    }
    mods1 = mods_of(1)
    rope_swa = _rope_tables(l, lc, HEAD_DIM, _swa_place)
    z, xs, bcm, dt, sq, sk, sv = _cd_in(x2, mods1, p_cd, rope_swa, lc)
    dt2 = jnp.stack([jnp.pad(dt[:, :, dd * SSM_HEADS:(dd + 1) * SSM_HEADS],
                             ((0, 0), (0, 0), (0, LANES - SSM_HEADS))) for dd in range(2)])
    dtt2 = jnp.stack([jnp.swapaxes(dt[:, :, dd * SSM_HEADS:(dd + 1) * SSM_HEADS], 1, 2) for dd in range(2)])
    ysf, ysb = _ssd(xs, bcm, dt2, dtt2, p_cd, lc)
    satt = _swa(sq, sk, sv, swa_sink[0], lc)
    x3 = _cd_out(x2, ysf, ysb, z, satt, mods1, p_cd, lc)
    t1 = FFN_ROW_BLOCK_LATENT if l % FFN_ROW_BLOCK_LATENT == 0 else ROW_BLOCK
    return _ffn(x3, mods1, _ffn_params(1, norm_ffn_g, ffn_w_up, ffn_conv_w, ffn_conv_b, ffn_w_down),
                t1, (0, l), lambda i: 1, row(final_norm_g))
```

```python
import functools
import math

import jax
import jax.numpy as jnp
import numpy as np
from jax import lax
from jax.experimental import pallas as pl
from jax.experimental.pallas import tpu as pltpu

F32 = jnp.float32
BF16 = jnp.bfloat16

GRID_W = 64
ROPE_THETA = 10000.0
NORM_EPS = 1e-6
HEAD_DIM = 64
MLA_HEADS, MLA_NOPE, MLA_ROPE, MLA_V = 8, 64, 32, 64
MLA_Q_RANK, MLA_KV_RANK = 384, 256
RWKV_HEADS, RWKV_N = 8, 64
RWKV_W = RWKV_HEADS * RWKV_N
DECAY_LORA, ICLR_LORA, GATE_LORA = 64, 64, 128
GN_EPS = 64e-5
SSM_HEADS, SSM_P, SSM_G, SSM_N = 16, 64, 2, 128
SSM_INNER = SSM_HEADS * SSM_P
SSM_CONV = 5
SWA_HEADS, SWA_KV_HEADS, WINDOW = 8, 2, 128
D_FF = 2816
FFN_CONV = 3

LANES = 128
HALO = 16
ROW_BLOCK = 256
MLA_Q_BLOCKS = 2
RWKV_CHUNK = 64
RWKV_GROUP = 4
RWKV_PACK = 4
SSD_CHUNK = 128
CD_CONV_CHUNK = 512
SWA_BLOCK = 128
FFN_CHUNK = 256
FFN_DOWN_GROUP = 4
FFN_ROW_BLOCK_LATENT = 512
VMEM_LIMIT = 56 * 1024 * 1024
NEG_BIG = -1e30


def _cparams(sem):
    return pltpu.CompilerParams(dimension_semantics=sem, vmem_limit_bytes=VMEM_LIMIT)


def _sigmoid(x):
    return 1.0 / (1.0 + jnp.exp(-x))


def _silu(x):
    return x * _sigmoid(x)


def _rms(x, eps=NORM_EPS):
    return x * lax.rsqrt(jnp.mean(x * x, axis=-1, keepdims=True) + eps)


def _bdot(a, b):
    return jnp.dot(a.astype(BF16), b.astype(BF16), preferred_element_type=F32)


def _bdot_nt(a, b):
    return lax.dot_general(a.astype(BF16), b.astype(BF16), (((1,), (1,)), ((), ())),
                           preferred_element_type=F32)


def _bdot_tn(a, b):
    return lax.dot_general(a.astype(BF16), b.astype(BF16), (((0,), (0,)), ((), ())),
                           preferred_element_type=F32)


def _split3(x):
    hi = x.astype(BF16)
    r1 = x - hi.astype(F32)
    mid = r1.astype(BF16)
    lo = (r1 - mid.astype(F32)).astype(BF16)
    return hi, mid, lo


def _dot_exact_rhs(x, m):
    hi, mid, lo = _split3(x)
    return (jnp.dot(hi, m, preferred_element_type=F32) + jnp.dot(mid, m, preferred_element_type=F32)
            + jnp.dot(lo, m, preferred_element_type=F32))


def _group_sum(x, m):
    hi = x.astype(BF16)
    lo = (x - hi.astype(F32)).astype(BF16)
    return jnp.dot(hi, m, preferred_element_type=F32) + jnp.dot(lo, m, preferred_element_type=F32)


def _dot_exact_lhs(m, x):
    hi, mid, lo = _split3(x)
    return (jnp.dot(m, hi, preferred_element_type=F32) + jnp.dot(m, mid, preferred_element_type=F32)
            + jnp.dot(m, lo, preferred_element_type=F32))


def _dot_exact_rhs_nt(x, m):
    hi, mid, lo = _split3(x)
    dn = (((1,), (1,)), ((), ()))
    return (lax.dot_general(hi, m, dn, preferred_element_type=F32)
            + lax.dot_general(mid, m, dn, preferred_element_type=F32)
            + lax.dot_general(lo, m, dn, preferred_element_type=F32))


def _rope(t, cos, sin_a, sin_b, half):
    w = t.shape[-1]
    return t * cos + pltpu.roll(t, w - half, 1) * sin_a + pltpu.roll(t, half, 1) * sin_b


def _modnorm(x, g, shift, scale):
    return (_rms(x) * g) * (1.0 + scale) + shift


def _seg_flags(i, starts, ends):
    prev_ok = functools.reduce(jnp.logical_and, [i != s for s in starts])
    next_ok = functools.reduce(jnp.logical_and, [i != (e - 1) for e in ends])
    return prev_ok, next_ok


def _h_with_halo_rows(x, xp, xn, g, shift, scale, prev_ok, next_ok):
    h = _modnorm(x, g, shift, scale)
    hp = _modnorm(xp, g, shift, scale) * prev_ok.astype(F32)
    hn = _modnorm(xn, g, shift, scale) * next_ok.astype(F32)
    return jnp.concatenate([hp.astype(BF16), h.astype(BF16), hn.astype(BF16)], axis=0)


def _h_with_halo(x_ref, xp_ref, xn_ref, g, shift, scale, prev_ok, next_ok):
    return _h_with_halo_rows(x_ref[0], xp_ref[0], xn_ref[0], g, shift, scale, prev_ok, next_ok)


def _shift_rows(x, delta, t):
    m = x.shape[0]
    return pltpu.roll(x, (m - delta) % m, 0)[HALO:HALO + t]


def _ada_kernel(c_ref, w_ref, b_ref, o_ref):
    o_ref[0] = _bdot(_silu(c_ref[...]), w_ref[0]) + b_ref[0]


def _ada_mods(cond, ada_w, ada_b):
    depth, d, n = ada_w.shape
    rows = cond.shape[0]
    tn = 1024
    return pl.pallas_call(
        _ada_kernel,
        grid=(depth, n // tn),
        in_specs=[pl.BlockSpec((rows, d), lambda l, j: (0, 0)),
                  pl.BlockSpec((1, d, tn), lambda l, j: (l, 0, j)),
                  pl.BlockSpec((1, 1, tn), lambda l, j: (l, 0, j))],
        out_specs=pl.BlockSpec((1, rows, tn), lambda l, j: (l, 0, j)),
        out_shape=jax.ShapeDtypeStruct((depth, rows, n), F32),
        compiler_params=_cparams(("parallel", "parallel")),
        name="ada_mods",
    )(cond, ada_w, ada_b.reshape(depth, 1, n))


def _row_specs(t, w, nrows8, off=0):
    per = t // HALO
    nh = nrows8
    main = pl.BlockSpec((1, t, w), lambda b, i: (b, i + off, 0))
    prev = pl.BlockSpec((1, HALO, w), lambda b, i: (b, jnp.maximum((i + off) * per - 1, 0), 0))
    nxt = pl.BlockSpec((1, HALO, w), lambda b, i: (b, jnp.minimum((i + off + 1) * per, nh - 1), 0))
    return main, prev, nxt


def _split_row_specs(t, w, lc, l):
    per = t // HALO
    nbc = lc // t
    specs = []
    for is_ctx, rows in ((True, lc), (False, l)):
        nh = rows // HALO
        nblk = rows // t
        blk = (lambda i: jnp.minimum(i, nbc - 1)) if is_ctx else (lambda i: jnp.clip(i - nbc, 0, nblk - 1))
        specs += [pl.BlockSpec((1, t, w), lambda b, i, blk=blk: (b, blk(i), 0)),
                  pl.BlockSpec((1, HALO, w), lambda b, i, blk=blk: (b, jnp.maximum(blk(i) * per - 1, 0), 0)),
                  pl.BlockSpec((1, HALO, w),
                               lambda b, i, blk=blk, nh=nh: (b, jnp.minimum((blk(i) + 1) * per, nh - 1), 0))]
    return specs


def _const_spec(shape):
    nd = len(shape)
    return pl.BlockSpec(shape, lambda *_: (0,) * nd)


def _resident_spec(shape):
    nd = len(shape)
    return pl.BlockSpec(shape, lambda *_: (0,) * nd, pipeline_mode=pl.Buffered(1))


def _ab_in_kernel(c_ref, cp_ref, cn_ref, x_ref, xp_ref, xn_ref, mods_ref, g_ref, cos_ref, sa_ref, sb_ref, win_ref,
                  gq_ref, wq_ref, gkv_ref, wkk_ref, wvv_ref, mup_ref, mun_ref, g2_ref,
                  w0_ref, w2_ref, a0_ref, a2_ref, kkk_ref, ka_ref, rk_ref, bd_ref,
                  q_out, k_out, v_out, r_out, vv_out, kk_out, g_out, bonus_out,
                  lw_out, ke_out, b_out, *, starts, ends):
    i = pl.program_id(1)
    t = x_ref.shape[1]
    prev_ok, next_ok = _seg_flags(i, starts, ends)
    m = mods_ref[0, 0]
    is_ctx = i < ends[0]
    pick = lambda a, b: jnp.where(is_ctx, a[0], b[0])
    h_all = _h_with_halo_rows(pick(c_ref, x_ref), pick(cp_ref, xp_ref), pick(cn_ref, xn_ref),
                              g_ref[...], m[0:1], m[1:2], prev_ok, next_ok)
    pin = jnp.dot(h_all, win_ref[...], preferred_element_type=F32)

    cos, sa, sb = cos_ref[...], sa_ref[...], sb_ref[...]
    main = pin[HALO:HALO + t]
    cq, ckv, kr = main[:, 0:384], main[:, 384:640], main[:, 640:768]
    qf = _bdot(_rms(cq) * gq_ref[...], wq_ref[...])
    kvn = (_rms(ckv) * gkv_ref[...]).astype(BF16)
    kf = jnp.dot(kvn, wkk_ref[...], preferred_element_type=F32)
    v_out[0] = jnp.dot(kvn, wvv_ref[...], preferred_element_type=F32).astype(BF16)
    krr = _rope(kr, cos, sa, sb, MLA_ROPE // 2)
    for h in range(MLA_HEADS):
        sl = slice(h * LANES, (h + 1) * LANES)
        q_out[0, h] = _rope(qf[:, sl], cos, sa, sb, MLA_ROPE // 2).astype(BF16)
        k_out[0, h] = (kf[:, sl] + krr).astype(BF16)

    rw = pin[:, 768:]
    x = rw[HALO:HALO + t]
    xp = _shift_rows(rw, -1, t)
    xn = _shift_rows(rw, 1, t)
    xs = x + mup_ref[...] * (xp - x) + mun_ref[...] * (xn - x)
    w = RWKV_W
    r, k, v = xs[:, 0:w], xs[:, w:2 * w], xs[:, 2 * w:3 * w]
    xw, xa, xg = xs[:, 3 * w:3 * w + 128], xs[:, 3 * w + 128:3 * w + 256], xs[:, 3 * w + 256:3 * w + 384]
    bd = bd_ref[...]
    g_out[0] = _bdot(_sigmoid(xg), g2_ref[...]).astype(g_out.dtype)
    kkr = k * kkk_ref[...]
    kk = kkr * lax.rsqrt(_group_sum(kkr * kkr, bd) + 1e-12)
    r_out[0] = r.astype(r_out.dtype)
    vv_out[0] = v.astype(vv_out.dtype)
    kk_out[0] = kk.astype(kk_out.dtype)
    bonus_out[0] = (_group_sum(r * k * rk_ref[...], bd) * v).astype(bonus_out.dtype)
    tw = jnp.tanh(xw).astype(BF16)
    xab = xa.astype(BF16)
    for d in range(2):
        wl = w0_ref[d] + jnp.dot(tw, w2_ref[d], preferred_element_type=F32)
        lw_out[d, 0] = -math.exp(-0.5) * _sigmoid(wl)
        a = _sigmoid(a0_ref[d] + jnp.dot(xab, a2_ref[d], preferred_element_type=F32))
        ke_out[d, 0] = (k * (1.0 + (a - 1.0) * ka_ref[...])).astype(ke_out.dtype)
        b_out[d, 0] = (kk * a).astype(b_out.dtype)


def _ab_in(ctx, x, mods, p, rope):
    b, l, d = x.shape
    lc = ctx.shape[1]
    ltot = lc + l
    t = ROW_BLOCK
    nb, nbc = ltot // t, lc // t
    kern = functools.partial(_ab_in_kernel, starts=(0, nbc), ends=(nbc, nb))
    tab = pl.BlockSpec((t, LANES), lambda bb, i: (i, 0))
    consts = [p['win'], p['gq'], p['wq'], p['gkv'], p['wkk'], p['wvv'], p['mup'], p['mun'], p['g2'],
              p['w0'], p['w2'], p['a0'], p['a2'], p['kkk'], p['ka'], p['rk'], p['bd']]
    in_specs = (_split_row_specs(t, d, lc, l)
                + [pl.BlockSpec((1, 1, 6, d), lambda bb, i: (bb, jnp.where(i >= nbc, 1, 0), 0, 0)),
                 _const_spec((1, d)), tab, tab, tab] + [_const_spec(c.shape) for c in consts])
    hq = pl.BlockSpec((1, MLA_HEADS, t, LANES), lambda bb, i: (bb, 0, i, 0))
    row = pl.BlockSpec((1, t, RWKV_W), lambda bb, i: (bb, i, 0))
    drow = pl.BlockSpec((2, 1, t, RWKV_W), lambda bb, i: (0, bb, i, 0))
    f_rows = jax.ShapeDtypeStruct((b, ltot, RWKV_W), BF16)
    f_drows = jax.ShapeDtypeStruct((2, b, ltot, RWKV_W), BF16)
    f_logw = jax.ShapeDtypeStruct((2, b, ltot, RWKV_W), F32)
    return pl.pallas_call(
        kern, grid=(b, nb), in_specs=in_specs,
        out_specs=[hq, hq, row, row, row, row, row, row, drow, drow, drow],
        out_shape=[jax.ShapeDtypeStruct((b, MLA_HEADS, ltot, LANES), BF16),
                   jax.ShapeDtypeStruct((b, MLA_HEADS, ltot, LANES), BF16),
                   jax.ShapeDtypeStruct((b, ltot, MLA_HEADS * MLA_V), BF16),
                   f_rows, f_rows, f_rows, f_rows, f_rows, f_logw, f_drows, f_drows],
        compiler_params=_cparams(("parallel", "parallel")),
        name="ab_in",
    )(ctx, ctx, ctx, x, x, x, mods, p['norm_g'], rope[0], rope[1], rope[2], *consts)


def _mla_attn_kernel(*refs, nq, exp2_scale):
    q_refs, (k_ref, v_ref, o_ref) = refs[:nq], refs[nq:]
    for hp in range(MLA_HEADS // 2):
        outs = []
        for h in (2 * hp, 2 * hp + 1):
            q = q_refs[0][0, h] if nq == 1 else jnp.concatenate([r[0, h] for r in q_refs], axis=0)
            s = lax.dot_general(q, k_ref[0, h], (((1,), (1,)), ((), ())), preferred_element_type=F32)
            mx = jnp.max(s, axis=-1, keepdims=True)
            e = jnp.exp2((s - mx) * exp2_scale)
            den = jnp.sum(e, axis=-1, keepdims=True)
            o2 = jnp.dot(e.astype(BF16), v_ref[0, :, hp * LANES:(hp + 1) * LANES],
                         preferred_element_type=F32)
            outs.append(o2 / den)
        lane = lax.broadcasted_iota(jnp.int32, outs[0].shape, 1)
        o_ref[0, :, hp * LANES:(hp + 1) * LANES] = jnp.where(lane < MLA_V, outs[0], outs[1]).astype(o_ref.dtype)


def _mla_attn(q, k, v, lc):
    b, hh, ltot, _ = q.shape
    l = ltot - lc
    vw = hh * MLA_V
    exp2_scale = (MLA_NOPE + MLA_ROPE) ** -0.5 * math.log2(math.e)
    att_c = pl.pallas_call(
        functools.partial(_mla_attn_kernel, nq=1, exp2_scale=exp2_scale), grid=(b,),
        in_specs=[pl.BlockSpec((1, hh, lc, LANES), lambda bb: (bb, 0, 0, 0)),
                  pl.BlockSpec((1, hh, lc, LANES), lambda bb: (bb, 0, 0, 0)),
                  pl.BlockSpec((1, lc, vw), lambda bb: (bb, 0, 0))],
        out_specs=pl.BlockSpec((1, lc, vw), lambda bb: (bb, 0, 0)),
        out_shape=jax.ShapeDtypeStruct((b, lc, vw), BF16),
        compiler_params=_cparams(("parallel",)),
        name="mla_attn_ctx",
    )(q, k, v)
    nq = MLA_Q_BLOCKS if l % (MLA_Q_BLOCKS * lc) == 0 else 1
    off = 1
    qspec = lambda j: pl.BlockSpec((1, hh, lc, LANES), lambda bb, i, j=j: (bb, 0, off + nq * i + j, 0))
    att_l = pl.pallas_call(
        functools.partial(_mla_attn_kernel, nq=nq, exp2_scale=exp2_scale), grid=(b, l // (nq * lc)),
        in_specs=[qspec(j) for j in range(nq)]
        + [pl.BlockSpec((1, hh, ltot, LANES), lambda bb, i: (bb, 0, 0, 0)),
           pl.BlockSpec((1, ltot, vw), lambda bb, i: (bb, 0, 0))],
        out_specs=pl.BlockSpec((1, nq * lc, vw), lambda bb, i: (bb, i, 0)),
        out_shape=jax.ShapeDtypeStruct((b, l, vw), BF16),
        compiler_params=_cparams(("parallel", "arbitrary")),
        name="mla_attn_lat",
    )(*([q] * nq), k, v)
    return att_c, att_l


def _rwkv_scan_kernel(rf_ref, vf_ref, kkf_ref, lwf_ref, kef_ref, bf_ref,
                      rb_ref, vb_ref, kkb_ref, lwb_ref, keb_ref, bb_ref,
                      yf_ref, yb_ref, s_ref, *, nchunks):
    n = pl.program_id(1)
    c = RWKV_CHUNK
    rows = nchunks * c

    @pl.when(n == 0)
    def _():
        s_ref[...] = jnp.zeros_like(s_ref)

    rowb = lax.broadcasted_iota(jnp.int32, (rows, rows), 0)
    colb = lax.broadcasted_iota(jnp.int32, (rows, rows), 1)
    same = (rowb // c) == (colb // c)
    hpg = RWKV_PACK
    gw = hpg * RWKV_N
    ngrp = RWKV_HEADS // hpg
    row = lax.broadcasted_iota(jnp.int32, (c, gw), 0)
    col = lax.broadcasted_iota(jnp.int32, (c, gw), 1) % RWKV_N
    eye = (row == col).astype(F32)
    lane_head = lax.broadcasted_iota(jnp.int32, (c, gw), 1) // RWKV_N
    head_masks = [lane_head == h for h in range(hpg)]
    diag_blocks = (lax.broadcasted_iota(jnp.int32, (gw, gw), 0) // RWKV_N
                   == lax.broadcasted_iota(jnp.int32, (gw, gw), 1) // RWKV_N)

    def bdiag(x):
        return jnp.concatenate([jnp.where(mk, x, 0.0) for mk in head_masks], axis=0).astype(BF16)

    def bmm(x, w):
        return jnp.dot(x.astype(BF16), bdiag(w), preferred_element_type=F32)

    def bmm_nt(x, w):
        return lax.dot_general(x.astype(BF16), bdiag(w), (((1,), (1,)), ((), ())), preferred_element_type=F32)

    dirs = []
    for sgn, (r_ref, v_ref, kk_ref, lw_ref, ke_ref, b_ref) in (
            (1, (rf_ref, vf_ref, kkf_ref, lwf_ref, kef_ref, bf_ref)),
            (-1, (rb_ref, vb_ref, kkb_ref, lwb_ref, keb_ref, bb_ref))):
        inclb = jnp.logical_and(same, sgn * (rowb - colb) >= 0)
        lw = lw_ref[0, 0]
        cum = _dot_exact_lhs(inclb.astype(BF16), lw)
        tot = _dot_exact_lhs(same.astype(BF16), lw)
        r, v, kk, ke, bb = (x.astype(F32) for x in (r_ref[0], v_ref[0], kk_ref[0], ke_ref[0, 0], b_ref[0, 0]))
        e_nc = jnp.exp(-cum)
        e_end = jnp.exp(tot - cum)
        dirs.append(dict(
            incl=sgn * (row - col) >= 0,
            strict=sgn * (row - col) > 0,
            v=v, rt=r * jnp.exp(cum), kt=kk * jnp.exp(cum - lw), bh=bb * e_nc, kh=ke * e_nc,
            bc=bb * e_end, kc=ke * e_end, gam=jnp.exp(tot)))

    chains = [(d, g, q) for g in range(nchunks) for q in range(ngrp) for d in range(2)]
    blk = lambda name, d, g, q: dirs[d][name][g * c:(g + 1) * c, q * gw:(q + 1) * gw]
    ms, ns, pbs, pks = [], [], [], []
    for d, g, q in chains:
        a = jnp.concatenate([blk('kt', d, g, q), blk('rt', d, g, q)], axis=0)
        mpb = bmm_nt(a, blk('bh', d, g, q))
        nk = bmm_nt(a, blk('kh', d, g, q))
        ms.append(jnp.where(dirs[d]['strict'], mpb[0:c], 0.0))
        pbs.append(jnp.where(dirs[d]['incl'], mpb[c:2 * c], 0.0))
        ns.append(jnp.where(dirs[d]['strict'], nk[0:c], 0.0))
        pks.append(jnp.where(dirs[d]['incl'], nk[c:2 * c], 0.0))
    xinv = [eye - m for m in ms]
    pw = [bmm(m, m) for m in ms]
    steps = int(math.log2(c)) - 1
    for it in range(steps):
        if it < steps - 1:
            xp = [bmm(jnp.concatenate([x, p], axis=0), p) for x, p in zip(xinv, pw)]
            xinv = [x + z[0:c] for x, z in zip(xinv, xp)]
            pw = [z[c:2 * c] for z in xp]
        else:
            xinv = [x + bmm(x, p) for x, p in zip(xinv, pw)]
    npk = [bmm(jnp.concatenate([nn, pk], axis=0), blk('v', *ch)) for ch, nn, pk in zip(chains, ns, pks)]
    ku = [-bmm(x, blk('kt', *ch)) for ch, x in zip(chains, xinv)]
    u0 = [-bmm(x, z[0:c]) for x, z in zip(xinv, npk)]
    ry = [(blk('rt', *ch) + bmm(pb, k_)).astype(BF16) for ch, pb, k_ in zip(chains, pbs, ku)]
    y0 = [bmm(pb, u_) + z[c:2 * c] for pb, u_, z in zip(pbs, u0, npk)]
    tmw, sadd = {}, {}
    for ch, k_, u_ in zip(chains, ku, u0):
        lhs = jnp.concatenate([jnp.concatenate([k_, u_], axis=1),
                               jnp.concatenate([jnp.zeros((c, gw), F32), blk('v', *ch)], axis=1)],
                              axis=0)
        rhs = jnp.concatenate([blk('bc', *ch), blk('kc', *ch)], axis=0)
        full = _bdot_tn(lhs, rhs)
        tmw[ch] = jnp.where(diag_blocks, full[0:gw], 0.0).astype(BF16)
        low = full[gw:2 * gw]
        sadd[ch] = functools.reduce(
            jnp.add, [jnp.where(head_masks[h], low[h * RWKV_N:(h + 1) * RWKV_N], 0.0) for h in range(hpg)])
    ry = dict(zip(chains, ry))
    y0 = dict(zip(chains, y0))

    state = {(d, q): s_ref[d, :, q * gw:(q + 1) * gw] for d in range(2) for q in range(ngrp)}
    youts = [[None] * nchunks for _ in range(2)]
    for step in range(nchunks):
        for d in range(2):
            g = step if d == 0 else nchunks - 1 - step
            ys = []
            for q in range(ngrp):
                ch = (d, g, q)
                s0 = state[(d, q)]
                ys.append(y0[ch] + lax.dot_general(ry[ch], bdiag(s0), (((1,), (1,)), ((), ())),
                                                   preferred_element_type=F32))
                state[(d, q)] = (s0 * dirs[d]['gam'][g * c:g * c + 1, q * gw:(q + 1) * gw]
                                 + jnp.dot(s0.astype(BF16), tmw[ch], preferred_element_type=F32) + sadd[ch])
            youts[d][g] = jnp.concatenate(ys, axis=1)
    yf_ref[0] = jnp.concatenate(youts[0], axis=0).astype(yf_ref.dtype)
    yb_ref[0] = jnp.concatenate(youts[1], axis=0).astype(yb_ref.dtype)
    for d in range(2):
        for q in range(ngrp):
            s_ref[d, :, q * gw:(q + 1) * gw] = state[(d, q)]


def _scan_chunk_index(n, nc_ctx, nc, d):
    bwd = jnp.where(n < nc_ctx, nc_ctx - 1 - n, nc - 1 - (n - nc_ctx))
    return jnp.where(d == 0, n, bwd)


def _rwkv_scan(r, v, kk, lw, ke, bb, lc):
    b, ltot, w = r.shape
    nch = RWKV_GROUP
    rows = nch * RWKV_CHUNK
    nc, ncc = ltot // rows, lc // rows
    specs = []
    for d in range(2):
        cidx = functools.partial(_scan_chunk_index, nc_ctx=ncc, nc=nc, d=d)
        shared = pl.BlockSpec((1, rows, w), lambda bi, n, cidx=cidx: (bi, cidx(n), 0))
        perdir = pl.BlockSpec((1, 1, rows, w), lambda bi, n, cidx=cidx, d=d: (d, bi, cidx(n), 0))
        specs.append((shared, perdir))
    kern = functools.partial(_rwkv_scan_kernel, nchunks=nch)
    out = jax.ShapeDtypeStruct((b, ltot, w), BF16)
    return pl.pallas_call(
        kern, grid=(b, nc),
        in_specs=[specs[0][0]] * 3 + [specs[0][1]] * 3 + [specs[1][0]] * 3 + [specs[1][1]] * 3,
        out_specs=[specs[0][0], specs[1][0]],
        out_shape=[out, out],
        scratch_shapes=[pltpu.VMEM((2, RWKV_N, w), F32)],
        compiler_params=_cparams(("parallel", "arbitrary")),
        name="rwkv_scan",
    )(r, v, kk, lw, ke, bb, r, v, kk, lw, ke, bb)


def _ab_out_kernel(c_ref, x_ref, attc_ref, attl_ref, yf_ref, yb_ref, bonus_ref, g_ref, mods_ref, lng_ref,
                   lnb_ref, bd_ref, wout_ref, o_ref, *, nbc):
    is_ctx = pl.program_id(1) < nbc
    att = jnp.where(is_ctx, attc_ref[0].astype(F32), attl_ref[0].astype(F32))
    y = yf_ref[0].astype(F32) + yb_ref[0].astype(F32)
    bd = bd_ref[...]
    mu = _group_sum(y, bd) * (1.0 / RWKV_N)
    dl = y - mu
    var = _group_sum(dl * dl, bd) * (1.0 / RWKV_N)
    yn = dl * lax.rsqrt(var + GN_EPS) * lng_ref[...] + lnb_ref[...]
    rwo = (yn + bonus_ref[0].astype(F32)) * g_ref[0].astype(F32)
    feat = jnp.concatenate([att.astype(BF16), rwo.astype(BF16)], axis=1)
    o = jnp.dot(feat, wout_ref[...], preferred_element_type=F32)
    o_ref[0] = jnp.where(is_ctx, c_ref[0], x_ref[0]) + mods_ref[0, 0][2:3] * o


def _ab_out(ctx, x, att_c, att_l, yf, yb, bonus, g, mods, p):
    b, l, d = x.shape
    lc = ctx.shape[1]
    ltot = lc + l
    t = ROW_BLOCK
    nbc = lc // t
    row = lambda w: pl.BlockSpec((1, t, w), lambda bb, i: (bb, i, 0))
    aw = att_c.shape[-1]
    consts = [p['lng'], p['lnb'], p['bd'], p['wout']]
    return pl.pallas_call(
        functools.partial(_ab_out_kernel, nbc=nbc), grid=(b, ltot // t),
        in_specs=[pl.BlockSpec((1, t, d), lambda bb, i: (bb, jnp.minimum(i, nbc - 1), 0)),
                  pl.BlockSpec((1, t, d), lambda bb, i: (bb, jnp.maximum(i - nbc, 0), 0)),
                  pl.BlockSpec((1, t, aw), lambda bb, i: (bb, jnp.minimum(i, nbc - 1), 0)),
                  pl.BlockSpec((1, t, aw), lambda bb, i: (bb, jnp.maximum(i - nbc, 0), 0)),
                  row(RWKV_W), row(RWKV_W), row(RWKV_W), row(RWKV_W),
                  pl.BlockSpec((1, 1, 6, d), lambda bb, i: (bb, jnp.where(i >= nbc, 1, 0), 0, 0))]
        + [_const_spec(c.shape) for c in consts],
        out_specs=row(d),
        out_shape=jax.ShapeDtypeStruct((b, ltot, d), F32),
        compiler_params=_cparams(("parallel", "parallel")),
        name="ab_out",
    )(ctx, x, att_c, att_l, yf, yb, bonus, g, mods, *consts)


def _ffn_kernel(x_ref, xp_ref, xn_ref, mods_ref, g_ref, wg_ref, wv_ref, cwg_ref, cwv_ref,
                cbg_ref, cbv_ref, wd_ref, fg_ref, o_ref, act_ref, *, starts, ends, final_norm):
    i = pl.program_id(1)
    t = x_ref.shape[1]
    prev_ok, next_ok = _seg_flags(i, starts, ends)
    m = mods_ref[0, 0]
    h_all = _h_with_halo(x_ref, xp_ref, xn_ref, g_ref[...], m[3:4], m[4:5], prev_ok, next_ok)
    nch = wg_ref.shape[0]
    fc = wg_ref.shape[2]

    def conv(u, cw, cb):
        return (cw[0:1] * _shift_rows(u, -1, t) + cw[1:2] * u[HALO:HALO + t]
                + cw[2:3] * _shift_rows(u, 1, t) + cb)

    def up(c):
        return (jnp.dot(h_all, wg_ref[c], preferred_element_type=F32),
                jnp.dot(h_all, wv_ref[c], preferred_element_type=F32))

    nxt = up(0)
    acc = None
    lo = 0
    for c in range(nch):
        ug, uv = nxt
        if c + 1 < nch:
            nxt = up(c + 1)
        act = _silu(conv(ug, cwg_ref[c], cbg_ref[c])) * conv(uv, cwv_ref[c], cbv_ref[c])
        act_ref[:, c * fc:(c + 1) * fc] = act.astype(BF16)
        if (c + 1 - lo) == FFN_DOWN_GROUP or c + 1 == nch:
            part = jnp.dot(act_ref[:, lo * fc:(c + 1) * fc], wd_ref[lo * fc:(c + 1) * fc, :],
                           preferred_element_type=F32)
            acc = part if acc is None else acc + part
            lo = c + 1
    out = x_ref[0] + m[5:6] * acc
    if final_norm:
        out = _rms(out) * fg_ref[...]
    o_ref[0] = out


def _ffn(x, mods, p, t, seg_rows, seg_index, final_g):
    b, rows, d = x.shape
    bounds = [r // t for r in seg_rows]
    starts, ends = tuple(bounds[:-1]), tuple(bounds[1:])
    kern = functools.partial(_ffn_kernel, starts=starts, ends=ends, final_norm=final_g is not None)
    main, prev, nxt = _row_specs(t, d, rows // HALO)
    fg = final_g if final_g is not None else jnp.ones((1, d), F32)
    consts = [p['wg'], p['wv'], p['cwg'], p['cwv'], p['cbg'], p['cbv'], p['wd'], fg]
    return pl.pallas_call(
        kern, grid=(b, rows // t),
        in_specs=[main, prev, nxt,
                  pl.BlockSpec((1, 1, 6, d), lambda bb, i: (bb, seg_index(i), 0, 0)),
                  _const_spec((1, d))] + [_resident_spec(c.shape) for c in consts],
        out_specs=pl.BlockSpec((1, t, d), lambda bb, i: (bb, i, 0)),
        out_shape=jax.ShapeDtypeStruct((b, rows, d), F32),
        scratch_shapes=[pltpu.VMEM((t, D_FF), BF16)],
        compiler_params=_cparams(("parallel", "parallel")),
        name="conv_ffn",
    )(x, x, x, mods, p['norm_g'], *consts)


def _cd_in_kernel(x_ref, xp_ref, xn_ref, mods_ref, g_ref, cos_ref, sa_ref, sb_ref, win_ref,
                  cw_ref, cb_ref, dtb_ref,
                  z_out, xs_out, bc_out, dt_out, q_out, k_out, v_out, *, starts, ends):
    i = pl.program_id(1)
    t = x_ref.shape[1]
    prev_ok, next_ok = _seg_flags(i, starts, ends)
    m = mods_ref[0, 0]
    h_all = _h_with_halo(x_ref, xp_ref, xn_ref, g_ref[...], m[0:1], m[1:2], prev_ok, next_ok)
    h_main = h_all[HALO:HALO + t]
    cwid = CD_CONV_CHUNK
    c0, c1 = SSM_INNER, 2 * SSM_INNER + 2 * SSM_G * SSM_N
    npc = (c1 - c0) // cwid

    def proj(j):
        return jnp.dot(h_all, win_ref[:, c0 + j * cwid:c0 + (j + 1) * cwid], preferred_element_type=F32)

    pieces = []
    nxt = proj(0)
    for j in range(npc):
        xbc = nxt
        if j + 1 < npc:
            nxt = proj(j + 1)
        else:
            zp = jnp.dot(h_main, win_ref[:, 0:c0], preferred_element_type=F32)
        cw = cw_ref[:, j * cwid:(j + 1) * cwid]
        conv = cb_ref[:, j * cwid:(j + 1) * cwid] + cw[2:3] * xbc[HALO:HALO + t]
        for tap in (0, 1, 3, 4):
            conv = conv + cw[tap:tap + 1] * _shift_rows(xbc, tap - SSM_CONV // 2, t)
        pieces.append(_silu(conv))
    rest = jnp.dot(h_main, win_ref[:, c1:], preferred_element_type=F32)
    z_out[0] = zp.astype(z_out.dtype)
    xc = jnp.concatenate(pieces, axis=1)
    xs_out[0] = xc[:, 0:SSM_INNER].astype(xs_out.dtype)
    bc_out[0] = xc[:, SSM_INNER:].astype(bc_out.dtype)
    raw = rest[:, 0:LANES] + dtb_ref[...]
    dt_out[0] = jnp.maximum(raw, 0.0) + jnp.log(1.0 + jnp.exp(-jnp.abs(raw)))
    cos, sa, sb = cos_ref[...], sa_ref[...], sb_ref[...]
    qw = SWA_HEADS * HEAD_DIM
    kvw = SWA_KV_HEADS * HEAD_DIM
    q = rest[:, LANES:LANES + qw]
    tile = lambda a: jnp.concatenate([a] * (qw // LANES), axis=1)
    q_out[0] = _rope(q, tile(cos), tile(sa), tile(sb), HEAD_DIM // 2).astype(BF16)
    k_out[0] = _rope(rest[:, LANES + qw:LANES + qw + kvw], cos, sa, sb, HEAD_DIM // 2).astype(BF16)
    v_out[0] = rest[:, LANES + qw + kvw:LANES + qw + 2 * kvw].astype(BF16)


def _cd_in(x, mods, p, rope, lc):
    b, ltot, d = x.shape
    t = ROW_BLOCK
    nb, nbc = ltot // t, lc // t
    kern = functools.partial(_cd_in_kernel, starts=(0, nbc), ends=(nbc, nb))
    main, prev, nxt = _row_specs(t, d, ltot // HALO)
    tab = pl.BlockSpec((t, LANES), lambda bb, i: (i, 0))
    consts = [p['win'], p['cw'], p['cb'], p['dtb']]
    row = lambda w: pl.BlockSpec((1, t, w), lambda bb, i: (bb, i, 0))
    sds = lambda w, dt: jax.ShapeDtypeStruct((b, ltot, w), dt)
    kvw = SWA_KV_HEADS * HEAD_DIM
    return pl.pallas_call(
        kern, grid=(b, nb),
        in_specs=[main, prev, nxt,
                  pl.BlockSpec((1, 1, 6, d), lambda bb, i: (bb, jnp.where(i >= nbc, 1, 0), 0, 0)),
                  _const_spec((1, d)), tab, tab, tab] + [_const_spec(c.shape) for c in consts],
        out_specs=[row(SSM_INNER), row(SSM_INNER), row(2 * SSM_G * SSM_N), row(LANES),
                   row(SWA_HEADS * HEAD_DIM), row(kvw), row(kvw)],
        out_shape=[sds(SSM_INNER, BF16), sds(SSM_INNER, BF16), sds(2 * SSM_G * SSM_N, BF16), sds(LANES, F32),
                   sds(SWA_HEADS * HEAD_DIM, BF16), sds(kvw, BF16), sds(kvw, BF16)],
        compiler_params=_cparams(("parallel", "parallel")),
        name="cd_in",
    )(x, x, x, mods, p['norm_g'], rope[0], rope[1], rope[2], *consts)


def _ssd_kernel(xsf_ref, bcf_ref, dtf_ref, dttf_ref, xsb_ref, bcb_ref, dtb_ref, dttb_ref,
                a_ref, at_ref, ex_ref, dsk_ref, yf_ref, yb_ref, h_ref):
    n = pl.program_id(1)
    q = SSD_CHUNK
    hg = SSM_HEADS // SSM_G
    gw = hg * SSM_P

    @pl.when(n == 0)
    def _():
        h_ref[...] = jnp.zeros_like(h_ref)

    row = lax.broadcasted_iota(jnp.int32, (q, q), 0)
    col = lax.broadcasted_iota(jnp.int32, (q, q), 1)
    ex = ex_ref[...]
    dirs = []
    for d, (xs_ref, bc_ref, dt_ref, dtt_ref, y_ref) in enumerate(
            ((xsf_ref, bcf_ref, dtf_ref, dttf_ref, yf_ref), (xsb_ref, bcb_ref, dtb_ref, dttb_ref, yb_ref))):
        sgn = 1 if d == 0 else -1
        incl = sgn * (row - col) >= 0
        inclb = incl.astype(BF16)
        dt_c = dt_ref[0, 0]
        dt_r = dtt_ref[0, 0]
        acum_c = _dot_exact_lhs(inclb, dt_c * a_ref[d])
        acum_r = _dot_exact_rhs_nt(dt_r * at_ref[d], inclb)
        end_c = acum_c[q - 1:q] if d == 0 else acum_c[0:1]
        grow = _group_sum(jnp.exp(acum_c), ex)
        tail = _group_sum(jnp.exp(end_c - acum_c) * dt_c, ex)
        dirs.append(dict(incl=incl, dt_r=dt_r, acum_c=acum_c, acum_r=acum_r, grow=grow, tail=tail,
                         chunk_decay=grow[q - 1:q] if d == 0 else grow[0:1],
                         xs=xs_ref[0].astype(F32), bcm=bc_ref[0], y_ref=y_ref))

    for g in range(SSM_G):
        gs = slice(g * gw, (g + 1) * gw)
        for d in range(2):
            p = dirs[d]
            bg = p['bcm'][:, g * SSM_N:(g + 1) * SSM_N].astype(BF16)
            cg = p['bcm'][:, (SSM_G + g) * SSM_N:(SSM_G + g + 1) * SSM_N].astype(BF16)
            cb = lax.dot_general(cg, bg, (((1,), (1,)), ((), ())), preferred_element_type=F32)
            xg = p['xs'][:, gs]
            yd = []
            for hh in range(hg):
                hd = g * hg + hh
                seg = jnp.where(p['incl'], p['acum_c'][:, hd:hd + 1] - p['acum_r'][hd:hd + 1, :], NEG_BIG)
                wgt = cb * jnp.exp(seg) * p['dt_r'][hd:hd + 1, :]
                yd.append(_bdot(wgt, xg[:, hh * SSM_P:(hh + 1) * SSM_P]))
            h_in = h_ref[d, g]
            y = jnp.concatenate(yd, axis=1) + (
                jnp.dot(cg, h_in.astype(BF16), preferred_element_type=F32) * p['grow'][:, gs])
            if d == 0:
                y = y + dsk_ref[:, gs] * xg
            p['y_ref'][0, :, gs] = y.astype(p['y_ref'].dtype)
            h_ref[d, g] = h_in * p['chunk_decay'][:, gs] + _bdot_tn(bg, xg * p['tail'][:, gs])


def _ssd(xs, bcm, dt2, dtt2, p, lc):
    b, ltot, inner = xs.shape
    q = SSD_CHUNK
    nc, ncc = ltot // q, lc // q
    hg = SSM_HEADS // SSM_G
    in_specs, out_specs = [], []
    for d in range(2):
        cidx = functools.partial(_scan_chunk_index, nc_ctx=ncc, nc=nc, d=d)
        in_specs += [pl.BlockSpec((1, q, inner), lambda bi, n, cidx=cidx: (bi, cidx(n), 0)),
                     pl.BlockSpec((1, q, 2 * SSM_G * SSM_N), lambda bi, n, cidx=cidx: (bi, cidx(n), 0)),
                     pl.BlockSpec((1, 1, q, LANES), lambda bi, n, cidx=cidx, d=d: (d, bi, cidx(n), 0)),
                     pl.BlockSpec((1, 1, SSM_HEADS, q), lambda bi, n, cidx=cidx, d=d: (d, bi, 0, cidx(n)))]
        out_specs.append(pl.BlockSpec((1, q, inner), lambda bi, n, cidx=cidx: (bi, cidx(n), 0)))
    consts = [p['a_row'], p['a_col'], p['ex'], p['dsk']]
    out = jax.ShapeDtypeStruct((b, ltot, inner), BF16)
    return pl.pallas_call(
        _ssd_kernel, grid=(b, nc),
        in_specs=in_specs + [_const_spec(c.shape) for c in consts],
        out_specs=out_specs, out_shape=[out, out],
        scratch_shapes=[pltpu.VMEM((2, SSM_G, SSM_N, hg * SSM_P), F32)],
        compiler_params=_cparams(("parallel", "arbitrary")),
        name="ssd_scan",
    )(xs, bcm, dt2, dtt2, xs, bcm, dt2, dtt2, *consts)


def _swa_kernel(sink_ref, bias_ref, q_ref, kc_ref, kp_ref, k0_ref, kn_ref, vc_ref, vp_ref, v0_ref, vn_ref,
                o_ref, *, scale):
    t = q_ref.shape[1]
    hg = SWA_HEADS // SWA_KV_HEADS
    kall = jnp.concatenate([kc_ref[0], kp_ref[0], k0_ref[0], kn_ref[0]], axis=0)
    vall = jnp.concatenate([vc_ref[0], vp_ref[0], v0_ref[0], vn_ref[0]], axis=0)
    bias = bias_ref[0]
    rowh = lax.broadcasted_iota(jnp.int32, (hg * t, 1), 0) // t
    qb = q_ref[0]
    heads = []
    for g in range(SWA_KV_HEADS):
        qg = jnp.concatenate([qb[:, (g * hg + j) * HEAD_DIM:(g * hg + j + 1) * HEAD_DIM] for j in range(hg)],
                             axis=0)
        kg = kall[:, g * HEAD_DIM:(g + 1) * HEAD_DIM]
        vg = vall[:, g * HEAD_DIM:(g + 1) * HEAD_DIM]
        s = lax.dot_general(qg, kg, (((1,), (1,)), ((), ())), preferred_element_type=F32) * scale + bias
        sink = jnp.zeros((hg * t, 1), F32)
        for j in range(hg):
            sink = jnp.where(rowh == j, sink_ref[g * hg + j], sink)
        mx = jnp.maximum(jnp.max(s, axis=-1, keepdims=True), sink)
        e = jnp.exp(s - mx)
        den = jnp.sum(e, axis=-1, keepdims=True) + jnp.exp(sink - mx)
        og = jnp.dot(e.astype(BF16), vg, preferred_element_type=F32) / den
        heads += [og[j * t:(j + 1) * t] for j in range(hg)]
    o_ref[0] = jnp.concatenate(heads, axis=1).astype(o_ref.dtype)


def _swa(q, k, v, sink, lc):
    b, ltot, qw = q.shape
    t = SWA_BLOCK
    nb = (ltot - lc) // t
    off = lc // t
    kvw = k.shape[-1]
    kern = functools.partial(_swa_kernel, scale=HEAD_DIM ** -0.5)
    hg = SWA_HEADS // SWA_KV_HEADS
    rr = (np.arange(hg * t) % t)[:, None]
    cc = np.arange(lc + 3 * t)[None, :] - lc
    variants = []
    for has_prev in (False, True):
        for has_next in (False, True):
            lo = rr if has_prev else t
            hi = rr + 2 * t if has_next else 2 * t - 1
            ok = np.broadcast_to((cc < 0) | ((cc >= lo) & (cc <= hi)), (hg * t, lc + 3 * t))
            variants.append(np.where(ok, 0.0, NEG_BIG).astype(np.float32))
    bias = jnp.asarray(np.stack(variants))
    bias_spec = pl.BlockSpec((1, hg * t, lc + 3 * t),
                             lambda bb, i: (2 * (i > 0).astype(jnp.int32) + (i < nb - 1).astype(jnp.int32), 0, 0))
    ctx = pl.BlockSpec((1, lc, kvw), lambda bb, i: (bb, 0, 0))
    prev = pl.BlockSpec((1, t, kvw), lambda bb, i: (bb, off + jnp.maximum(i - 1, 0), 0))
    cur = pl.BlockSpec((1, t, kvw), lambda bb, i: (bb, off + i, 0))
    nxt = pl.BlockSpec((1, t, kvw), lambda bb, i: (bb, off + jnp.minimum(i + 1, nb - 1), 0))
    return pl.pallas_call(
        kern, grid=(b, nb),
        in_specs=[pl.BlockSpec(memory_space=pltpu.SMEM), bias_spec,
                  pl.BlockSpec((1, t, qw), lambda bb, i: (bb, off + i, 0)),
                  ctx, prev, cur, nxt, ctx, prev, cur, nxt],
        out_specs=pl.BlockSpec((1, t, qw), lambda bb, i: (bb, i, 0)),
        out_shape=jax.ShapeDtypeStruct((b, ltot - lc, qw), BF16),
        compiler_params=_cparams(("parallel", "parallel")),
        name="swa_attn",
    )(sink, bias, q, k, k, k, k, v, v, v, v)


def _cd_out_kernel(x_ref, yf_ref, yb_ref, z_ref, att_ref, mods_ref, ng_ref, wout_ref, o_ref):
    u = (yf_ref[0].astype(F32) + yb_ref[0].astype(F32)) * _silu(z_ref[0].astype(F32))
    gw = SSM_INNER // SSM_G
    un = jnp.concatenate([_rms(u[:, g * gw:(g + 1) * gw]) for g in range(SSM_G)], axis=1) * ng_ref[...]
    feat = jnp.concatenate([un.astype(BF16), att_ref[0].astype(BF16)], axis=1)
    o = jnp.dot(feat, wout_ref[...], preferred_element_type=F32)
    o_ref[0] = x_ref[0] + mods_ref[0, 0][2:3] * o


def _cd_out(x, yf, yb, z, att, mods, p, lc):
    b, ltot, d = x.shape
    t = ROW_BLOCK
    off = lc // t
    nbl = (ltot - lc) // t
    full = lambda w: pl.BlockSpec((1, t, w), lambda bb, i: (bb, i + off, 0))
    return pl.pallas_call(
        _cd_out_kernel, grid=(b, nbl),
        in_specs=[full(d), full(SSM_INNER), full(SSM_INNER),
                  full(SSM_INNER), pl.BlockSpec((1, t, att.shape[-1]), lambda bb, i: (bb, i, 0)),
                  pl.BlockSpec((1, 1, 6, d), lambda bb, i: (bb, 1, 0, 0)),
                  _const_spec(p['ng'].shape), _const_spec(p['wout'].shape)],
        out_specs=pl.BlockSpec((1, t, d), lambda bb, i: (bb, i, 0)),
        out_shape=jax.ShapeDtypeStruct((b, ltot - lc, d), F32),
        compiler_params=_cparams(("parallel", "parallel")),
        name="cd_out",
    )(x, yf, yb, z, att, mods, p['ng'], p['wout'])


def _pad_cols(w, width):
    return jnp.pad(w, ((0, 0), (0, width - w.shape[1])))


def _rope_tables(n_tok, lc, rot_dim, place):
    rows = n_tok // GRID_W
    rp, cp = jnp.meshgrid(jnp.arange(rows, dtype=F32), jnp.arange(GRID_W, dtype=F32), indexing='ij')
    n_freq = rot_dim // 4
    inv_freq = ROPE_THETA ** (-jnp.arange(n_freq, dtype=F32) / n_freq)
    ang = jnp.concatenate([rp.reshape(-1, 1) * inv_freq, cp.reshape(-1, 1) * inv_freq], axis=-1)
    cos = jnp.concatenate([jnp.ones((lc, rot_dim // 2), F32), jnp.cos(ang)], axis=0)
    sin = jnp.concatenate([jnp.zeros((lc, rot_dim // 2), F32), jnp.sin(ang)], axis=0)
    zero = jnp.zeros_like(sin)
    return place(cos, cos, 1.0), place(-sin, zero, 0.0), place(zero, sin, 0.0)


def _mla_place(first, second, fill):
    n = first.shape[0]
    return jnp.concatenate([jnp.full((n, MLA_NOPE), fill, F32), first, second,
                            jnp.zeros((n, LANES - MLA_NOPE - MLA_ROPE), F32)], axis=1)


def _swa_place(first, second, fill):
    return jnp.concatenate([first, second] * (LANES // HEAD_DIM), axis=1)


def _block_diag_ones(n, group):
    idx = np.arange(n) // group
    return jnp.asarray(idx[:, None] == idx[None, :], BF16)


def _ffn_params(i, norm_ffn_g, ffn_w_up, ffn_conv_w, ffn_conv_b, ffn_w_down):
    d = ffn_w_up.shape[1]
    nch = D_FF // FFN_CHUNK
    chunks = lambda w: w.reshape(w.shape[0], nch, FFN_CHUNK).transpose(1, 0, 2)
    wup = ffn_w_up[i].astype(BF16)
    return {
        'norm_g': norm_ffn_g[i].reshape(1, d),
        'wg': chunks(wup[:, :D_FF]), 'wv': chunks(wup[:, D_FF:]),
        'cwg': chunks(ffn_conv_w[i][:, :D_FF]), 'cwv': chunks(ffn_conv_w[i][:, D_FF:]),
        'cbg': chunks(ffn_conv_b[i][None, :D_FF]), 'cbv': chunks(ffn_conv_b[i][None, D_FF:]),
        'wd': ffn_w_down[i].astype(BF16),
    }


def kernel(x, c, ctx, c_ctx, ada_w, ada_b, norm_mix_g, norm_ffn_g, ffn_w_up, ffn_conv_w, ffn_conv_b, ffn_w_down, final_norm_g, ab_w_in, ab_w_out, mla_q_norm_g, mla_w_q_up, mla_kv_norm_g, mla_w_kv_up, rwkv_mu_prev, rwkv_mu_next, rwkv_w0, rwkv_w2, rwkv_a0, rwkv_a2, rwkv_g2, rwkv_k_k, rwkv_k_a, rwkv_r_k, rwkv_ln_g, rwkv_ln_b, cd_w_in, cd_w_out, ssm_conv_w, ssm_conv_b, ssm_dt_bias, ssm_a_log, ssm_d, ssm_norm_g, swa_sink):
    b, l, d = x.shape
    lc = ctx.shape[1]
    assert ada_w.shape[0] == 2 and lc % ROW_BLOCK == 0 and l % ROW_BLOCK == 0
    ltot = lc + l

    nrow = -(-(b + 1) // 8) * 8
    cond = jnp.zeros((nrow, d), F32).at[:b].set(c).at[b].set(c_ctx)
    ada = _ada_mods(cond, ada_w, ada_b)
    def mods_of(i):
        lat = ada[i, :b].reshape(b, 1, 6, d)
        cx = jnp.broadcast_to(ada[i, b].reshape(1, 1, 6, d), (b, 1, 6, d))
        return jnp.concatenate([cx, lat], axis=1)

    row = lambda v: v.reshape(1, -1)
    nbc, nb = lc // ROW_BLOCK, ltot // ROW_BLOCK
    seg_of = lambda i: jnp.where(i >= nbc, 1, 0)

    w_in = ab_w_in[0]
    padw = lambda w, n: jnp.pad(w, ((0, 0), (0, n - w.shape[1])))
    o1, o2, o3 = MLA_Q_RANK, MLA_Q_RANK + MLA_KV_RANK, MLA_Q_RANK + MLA_KV_RANK + MLA_ROPE
    w_kr = jnp.pad(w_in[:, o2:o3], ((0, 0), (MLA_NOPE, LANES - MLA_NOPE - MLA_ROPE)))
    rw = w_in[:, o3:]
    rw_sizes = (3 * RWKV_W, DECAY_LORA, ICLR_LORA, GATE_LORA)
    def pad_rw(v):
        a, bq, cq_, dq = jnp.split(v, np.cumsum(rw_sizes)[:-1].tolist(), axis=-1)
        z = jnp.zeros(v.shape[:-1] + (LANES - DECAY_LORA,), v.dtype)
        return jnp.concatenate([a, bq, z, cq_, z, dq], axis=-1)
    win_ab = jnp.concatenate([w_in[:, :o2], w_kr, pad_rw(rw)], axis=1).astype(BF16)
    qd = MLA_NOPE + MLA_ROPE
    wq = jnp.pad(mla_w_q_up[0].reshape(MLA_Q_RANK, MLA_HEADS, qd), ((0, 0), (0, 0), (0, LANES - qd)))
    wkv = mla_w_kv_up[0].reshape(MLA_KV_RANK, MLA_HEADS, MLA_NOPE + MLA_V)
    wkk = jnp.pad(wkv[:, :, :MLA_NOPE], ((0, 0), (0, 0), (0, LANES - MLA_NOPE)))
    padrows = lambda w: jnp.pad(w, ((0, 0), (0, LANES - w.shape[1]), (0, 0)))
    p_ab = {
        'norm_g': row(norm_mix_g[0]), 'win': win_ab,
        'gq': row(mla_q_norm_g[0]), 'wq': wq.reshape(MLA_Q_RANK, MLA_HEADS * LANES).astype(BF16),
        'gkv': row(mla_kv_norm_g[0]), 'wkk': wkk.reshape(MLA_KV_RANK, MLA_HEADS * LANES).astype(BF16),
        'wvv': wkv[:, :, MLA_NOPE:].reshape(MLA_KV_RANK, MLA_HEADS * MLA_V).astype(BF16),
        'mup': row(pad_rw(rwkv_mu_prev[0])), 'mun': row(pad_rw(rwkv_mu_next[0])),
        'g2': rwkv_g2[0].astype(BF16),
        'w0': rwkv_w0[0].reshape(2, 1, RWKV_W), 'w2': padrows(rwkv_w2[0]).astype(BF16),
        'a0': rwkv_a0[0].reshape(2, 1, RWKV_W), 'a2': padrows(rwkv_a2[0]).astype(BF16),
        'kkk': row(rwkv_k_k[0]), 'ka': row(rwkv_k_a[0]), 'rk': row(rwkv_r_k[0]),
        'bd': _block_diag_ones(RWKV_W, RWKV_N),
        'lng': row(rwkv_ln_g[0]), 'lnb': row(rwkv_ln_b[0]), 'wout': ab_w_out[0].astype(BF16),
    }
    mods0 = mods_of(0)
    rope_mla = _rope_tables(l, lc, MLA_ROPE, _mla_place)
    (q, k, v, r, vv, kk, g, bonus, lw, ke, bb) = _ab_in(ctx, x, mods0, p_ab, rope_mla)
    att_c, att_l = _mla_attn(q, k, v, lc)
    yf, yb = _rwkv_scan(r, vv, kk, lw, ke, bb, lc)
    x1 = _ab_out(ctx, x, att_c, att_l, yf, yb, bonus, g, mods0, p_ab)
    x2 = _ffn(x1, mods0, _ffn_params(0, norm_ffn_g, ffn_w_up, ffn_conv_w, ffn_conv_b, ffn_w_down),
              ROW_BLOCK, (0, lc, ltot), seg_of, None)

    cw_in = cd_w_in[0]
    s1 = SSM_INNER
    s2 = s1 + SSM_INNER + 2 * SSM_G * SSM_N
    s3 = s2 + 2 * SSM_HEADS
    win_cd = jnp.concatenate([cw_in[:, :s2], padw(cw_in[:, s2:s3], LANES), cw_in[:, s3:]], axis=1).astype(BF16)
    a_neg = -jnp.exp(ssm_a_log[0])
    heads_to_lanes = np.zeros((LANES, SSM_INNER), np.float32)
    for hd in range(SSM_HEADS):
        heads_to_lanes[hd, hd * SSM_P:(hd + 1) * SSM_P] = 1.0
    p_cd = {
        'norm_g': row(norm_mix_g[1]), 'win': win_cd,
        'cw': ssm_conv_w[0], 'cb': row(ssm_conv_b[0]),
        'dtb': row(jnp.pad(ssm_dt_bias[0].reshape(-1), (0, LANES - 2 * SSM_HEADS))),
        'a_row': jnp.pad(a_neg, ((0, 0), (0, LANES - SSM_HEADS))).reshape(2, 1, LANES),
        'a_col': a_neg.reshape(2, SSM_HEADS, 1),
        'ex': jnp.asarray(heads_to_lanes, BF16),
        'dsk': row(jnp.repeat(ssm_d[0], SSM_P)),
        'ng': row(ssm_norm_g[0]), 'wout': cd_w_out[0].astype(BF16),
    }
    mods1 = mods_of(1)
    rope_swa = _rope_tables(l, lc, HEAD_DIM, _swa_place)
    z, xs, bcm, dt, sq, sk, sv = _cd_in(x2, mods1, p_cd, rope_swa, lc)
    dt2 = jnp.stack([jnp.pad(dt[:, :, dd * SSM_HEADS:(dd + 1) * SSM_HEADS],
                             ((0, 0), (0, 0), (0, LANES - SSM_HEADS))) for dd in range(2)])
    dtt2 = jnp.stack([jnp.swapaxes(dt[:, :, dd * SSM_HEADS:(dd + 1) * SSM_HEADS], 1, 2) for dd in range(2)])
    ysf, ysb = _ssd(xs, bcm, dt2, dtt2, p_cd, lc)
    satt = _swa(sq, sk, sv, swa_sink[0], lc)
    x3 = _cd_out(x2, ysf, ysb, z, satt, mods1, p_cd, lc)
    t1 = FFN_ROW_BLOCK_LATENT if l % FFN_ROW_BLOCK_LATENT == 0 else ROW_BLOCK
    return _ffn(x3, mods1, _ffn_params(1, norm_ffn_g, ffn_w_up, ffn_conv_w, ffn_conv_b, ffn_w_down),
                t1, (0, l), lambda i: 1, row(final_norm_g))
```
